```python
import jax, jax.numpy as jnp
from jax import lax
import numpy as np

D_MODEL = 1024
BATCH = 8
SEQ = 2048
DEPTH = 2
DEC_BATCH = 128
DEC_SEQ = 4
PAST_LEN = 16384
PAGE_SIZE = 128

N_BRANCH = 4
BRANCH_W = D_MODEL // 2
CONV_A_W = 3
A_GROUPS = 8
RET_HEADS = 4
RET_DK = BRANCH_W // RET_HEADS
RET_DV = BRANCH_W // RET_HEADS
ROPE_BASE = 10000.0
CONV_C_W = 31
C_GROUPS = 8
ML_HEADS = 4
ML_DK = BRANCH_W // ML_HEADS
CHUNK = 128
EPS = 1e-6
N_IN = 16 * BRANCH_W + 2 * ML_HEADS + N_BRANCH * D_MODEL

kernel_name = "hybrid_gated_parallel_conv_retnet_conformer_mlstm_step"


def _split_points():
    W = BRANCH_W
    sizes = [W] * 4 + [W] * 4 + [2 * W, W] + [W] * 5 + [ML_HEADS, ML_HEADS]
    return [int(v) for v in np.cumsum(sizes)]


def rmsnorm(x, g):
    xf = x.astype(jnp.float32)
    y = xf * lax.rsqrt(jnp.mean(xf * xf, axis=-1, keepdims=True) + EPS)
    return (y * g.astype(jnp.float32)).astype(x.dtype)


def layernorm(x, g, b):
    xf = x.astype(jnp.float32)
    mu = jnp.mean(xf, axis=-1, keepdims=True)
    var = jnp.mean(jnp.square(xf - mu), axis=-1, keepdims=True)
    y = (xf - mu) * lax.rsqrt(var + EPS) * g.astype(jnp.float32) + b.astype(jnp.float32)
    return y.astype(x.dtype)


def head_norm(y, g):
    yf = y.astype(jnp.float32)
    mu = jnp.mean(yf, axis=-1, keepdims=True)
    var = jnp.mean(jnp.square(yf - mu), axis=-1, keepdims=True)
    yn = ((yf - mu) * lax.rsqrt(var + EPS)).reshape(*y.shape[:-2], -1)
    return yn * g.astype(jnp.float32)


def causal_dwconv(u, buf, w):
    width = w.shape[0]
    ext = jnp.concatenate([buf.astype(u.dtype), u], axis=1)
    y = lax.conv_general_dilated(ext, w[:, None, :].astype(u.dtype), window_strides=(1,),
                                 padding="VALID", dimension_numbers=("NWC", "WIO", "NWC"),
                                 feature_group_count=u.shape[-1])
    return y, ext[:, ext.shape[1] - (width - 1):]


def rotary(x, pos):
    d = x.shape[-1]
    inv = ROPE_BASE ** (-jnp.arange(0, d, 2, dtype=jnp.float32) / d)
    ang = pos.astype(jnp.float32)[:, None] * inv[None, :]
    cos = jnp.cos(ang)[None, :, None, :]
    sin = jnp.sin(ang)[None, :, None, :]
    xf = x.astype(jnp.float32)
    x1, x2 = xf[..., : d // 2], xf[..., d // 2:]
    return jnp.concatenate([x1 * cos - x2 * sin, x1 * sin + x2 * cos], axis=-1).astype(x.dtype)


def _chunk_len(T):
    return CHUNK if T % CHUNK == 0 else T


def _to_chunks(a, L):
    b, T = a.shape[0], a.shape[1]
    return a.reshape(b, T // L, L, *a.shape[2:]).swapaxes(0, 1).astype(jnp.float32)


def retention_chunked(q, k, v, S0):
    bsz, T, H, _ = q.shape
    L = _chunk_len(T)
    log_g = jnp.log1p(-(2.0 ** (-5.0 - jnp.arange(H, dtype=jnp.float32))))
    idx = jnp.arange(L, dtype=jnp.float32)
    diff = idx[:, None] - idx[None, :]
    decay = jnp.where(diff >= 0, jnp.exp(jnp.maximum(diff, 0.0)[None] * log_g[:, None, None]), 0.0)
    q_dec = jnp.exp((idx[:, None] + 1.0) * log_g[None, :])
    k_dec = jnp.exp((L - 1.0 - idx)[:, None] * log_g[None, :])
    c_dec = jnp.exp(L * log_g)

    def step(S, inp):
        qc, kc, vc = inp
        sc = jnp.einsum('blhd,bshd->bhls', qc, kc) * decay[None]
        intra = jnp.einsum('bhls,bshe->blhe', sc, vc)
        inter = jnp.einsum('blhd,bhde->blhe', qc, S) * q_dec[None, :, :, None]
        S_new = S * c_dec[None, :, None, None] + jnp.einsum('bshd,bshe->bhde', kc * k_dec[None, :, :, None], vc)
        return S_new, intra + inter

    S, out = lax.scan(step, S0.astype(jnp.float32), (_to_chunks(q, L), _to_chunks(k, L), _to_chunks(v, L)))
    return out.swapaxes(0, 1).reshape(bsz, T, H, -1), S


def mlstm_chunked(q, k, v, i_pre, f_pre, C0, n0, m0):
    bsz, T, H, _ = q.shape
    L = _chunk_len(T)
    causal = jnp.tril(jnp.ones((L, L), dtype=bool))

    def step(carry, inp):
        C, n, m = carry
        qc, kc, vc, ic, fc = inp
        F = jnp.cumsum(jax.nn.log_sigmoid(fc), axis=1).transpose(0, 2, 1)
        ig = ic.transpose(0, 2, 1)
        logw = jnp.where(causal, F[..., :, None] - F[..., None, :] + ig[..., None, :], -jnp.inf)
        b = F + m[..., None]
        m_t = jnp.maximum(jnp.max(logw, axis=-1), b)
        wts = jnp.exp(logw - m_t[..., None])
        inter = jnp.exp(b - m_t)
        s = jnp.einsum('blhd,bshd->bhls', qc, kc) * wts
        num = jnp.einsum('bhls,bshe->bhle', s, vc) + jnp.einsum('blhd,bhde->bhle', qc, C) * inter[..., None]
        den = jnp.sum(s, axis=-1) + jnp.einsum('blhd,bhd->bhl', qc, n) * inter
        h = num / jnp.maximum(jnp.abs(den), jnp.exp(-m_t))[..., None]
        F_last = F[..., -1]
        g = F_last[..., None] - F + ig
        m_new = jnp.maximum(F_last + m, jnp.max(g, axis=-1))
        ws = jnp.exp(g - m_new[..., None])
        dec = jnp.exp(F_last + m - m_new)
        C_new = C * dec[..., None, None] + jnp.einsum('bhs,bshd,bshe->bhde', ws, kc, vc)
        n_new = n * dec[..., None] + jnp.einsum('bhs,bshd->bhd', ws, kc)
        return (C_new, n_new, m_new), h.transpose(0, 2, 1, 3)

    carry0 = (C0.astype(jnp.float32), n0.astype(jnp.float32), m0.astype(jnp.float32))
    xs = (_to_chunks(q, L), _to_chunks(k, L), _to_chunks(v, L), _to_chunks(i_pre, L), _to_chunks(f_pre, L))
    (C, n, m), h = lax.scan(step, carry0, xs)
    return h.swapaxes(0, 1).reshape(bsz, T, H, -1), C, n, m


def hybrid_layer(x, pos, buf_a, s_ret, buf_c, c_ml, n_ml, m_ml,
                 norm_g, w_in, conv_a_w, conv_c_w, conv_c_b, ln_c_g, ln_c_b,
                 ret_gn_g, ml_gn_g, ml_gate_b, w_br, w_out):
    bsz, T, _ = x.shape
    dt = x.dtype
    h = rmsnorm(x, norm_g)
    p = jnp.einsum('btd,dn->btn', h, w_in)
    (a_b, a_c, a_u, a_z, r_q, r_k, r_v, r_z, c_glu, c_z,
     m_q, m_k, m_v, m_o, m_z, m_i, m_f, g_all) = jnp.split(p, _split_points(), axis=-1)

    def heads(t, n):
        return t.reshape(bsz, T, n, -1)

    conv_a, new_buf_a = causal_dwconv(a_c * a_u, buf_a, conv_a_w)
    o_a = jax.nn.silu(a_z) * (a_b * conv_a)

    rq = rotary(heads(r_q, RET_HEADS), pos)
    rk = rotary(heads(r_k, RET_HEADS), pos) * (RET_DK ** -0.5)
    ret, new_s_ret = retention_chunked(rq, rk, heads(r_v, RET_HEADS), s_ret)
    o_r = jax.nn.silu(r_z) * head_norm(ret, ret_gn_g).astype(dt)

    glu_v, glu_g = jnp.split(c_glu, 2, axis=-1)
    conv_c, new_buf_c = causal_dwconv(glu_v * jax.nn.sigmoid(glu_g), buf_c, conv_c_w)
    conv_c = conv_c + conv_c_b.astype(dt)
    o_c = jax.nn.silu(c_z) * jax.nn.silu(layernorm(conv_c, ln_c_g, ln_c_b))

    i_pre = m_i + ml_gate_b[:ML_HEADS]
    f_pre = m_f + ml_gate_b[ML_HEADS:]
    hm, new_c, new_n, new_m = mlstm_chunked(heads(m_q, ML_HEADS), heads(m_k, ML_HEADS) * (ML_DK ** -0.5),
                                           heads(m_v, ML_HEADS), i_pre, f_pre, c_ml, n_ml, m_ml)
    hm = jax.nn.sigmoid(heads(m_o, ML_HEADS).astype(jnp.float32)) * hm
    o_m = jax.nn.silu(m_z) * head_norm(hm, ml_gn_g).astype(dt)

    branches = jnp.stack([o_a, o_r, o_c, o_m], axis=2)
    proj = jnp.einsum('btnw,nwd->btnd', branches, w_br)
    gates = jax.nn.sigmoid(g_all.reshape(bsz, T, N_BRANCH, D_MODEL))
    y = jnp.einsum('btd,de->bte', jnp.sum(gates * proj, axis=2), w_out)
    states = (new_buf_a.astype(dt), new_s_ret.astype(dt), new_buf_c.astype(dt),
              new_c.astype(dt), new_n.astype(dt), new_m.astype(dt))
    return x + y, states


def setup_inputs(seed: int = 0) -> dict:
    key = jax.random.key(seed)
    ks = jax.random.split(key, 24)
    W = BRANCH_W
    f32 = jnp.float32

    def nrm(k, shape, s):
        return jax.random.normal(k, shape, f32) * s

    return {
        "x_prompt": nrm(ks[0], (BATCH, SEQ, D_MODEL), 1.0),
        "x_sample": nrm(ks[1], (DEC_BATCH, DEC_SEQ, D_MODEL), 1.0),
        "state_conv_a": nrm(ks[2], (DEPTH, DEC_BATCH, CONV_A_W - 1, W), 1.0),
        "state_ret": nrm(ks[3], (DEPTH, DEC_BATCH, RET_HEADS, RET_DK, RET_DV), 0.1),
        "state_conv_c": nrm(ks[4], (DEPTH, DEC_BATCH, CONV_C_W - 1, W), 1.0),
        "state_mlstm_C": nrm(ks[5], (DEPTH, DEC_BATCH, ML_HEADS, ML_DK, ML_DK), 0.1),
        "state_mlstm_n": nrm(ks[6], (DEPTH, DEC_BATCH, ML_HEADS, ML_DK), 0.1),
        "state_mlstm_m": nrm(ks[7], (DEPTH, DEC_BATCH, ML_HEADS), 1.0),
        "norm_g": 1.0 + nrm(ks[8], (DEPTH, D_MODEL), 0.05),
        "w_in": nrm(ks[9], (DEPTH, D_MODEL, N_IN), D_MODEL ** -0.5),
        "conv_a_w": nrm(ks[10], (DEPTH, CONV_A_W, W), CONV_A_W ** -0.5),
        "conv_c_w": nrm(ks[11], (DEPTH, CONV_C_W, W), CONV_C_W ** -0.5),
        "conv_c_b": nrm(ks[12], (DEPTH, W), 0.02),
        "ln_c_g": 1.0 + nrm(ks[13], (DEPTH, W), 0.05),
        "ln_c_b": nrm(ks[14], (DEPTH, W), 0.02),
        "ret_gn_g": 1.0 + nrm(ks[15], (DEPTH, W), 0.05),
        "ml_gn_g": 1.0 + nrm(ks[16], (DEPTH, W), 0.05),
        "ml_gate_b": jnp.concatenate([nrm(ks[17], (DEPTH, ML_HEADS), 0.1),
                                      jnp.linspace(3.0, 6.0, ML_HEADS, dtype=f32)[None, :]
                                      + nrm(ks[18], (DEPTH, ML_HEADS), 0.1)], axis=-1),
        "w_br": nrm(ks[19], (DEPTH, N_BRANCH, W, D_MODEL), W ** -0.5),
        "w_out": nrm(ks[20], (DEPTH, D_MODEL, D_MODEL), D_MODEL ** -0.5),
        "final_g": 1.0 + nrm(ks[21], (D_MODEL,), 0.05),
    }


def reference(x_prompt, x_sample, state_conv_a, state_ret, state_conv_c, state_mlstm_C,
              state_mlstm_n, state_mlstm_m, norm_g, w_in, conv_a_w, conv_c_w, conv_c_b,
              ln_c_g, ln_c_b, ret_gn_g, ml_gn_g, ml_gate_b, w_br, w_out, final_g):
    W = BRANCH_W
    dt = x_prompt.dtype
    bp, tp = x_prompt.shape[0], x_prompt.shape[1]
    pos_p = jnp.arange(tp, dtype=jnp.int32)
    pos_s = PAST_LEN + jnp.arange(x_sample.shape[1], dtype=jnp.int32)
    z_buf_a = jnp.zeros((bp, CONV_A_W - 1, W), dt)
    z_ret = jnp.zeros((bp, RET_HEADS, RET_DK, RET_DV), jnp.float32)
    z_buf_c = jnp.zeros((bp, CONV_C_W - 1, W), dt)
    z_c = jnp.zeros((bp, ML_HEADS, ML_DK, ML_DK), jnp.float32)
    z_n = jnp.zeros((bp, ML_HEADS, ML_DK), jnp.float32)
    z_m = jnp.zeros((bp, ML_HEADS), jnp.float32)

    hp, hs = x_prompt, x_sample
    new_p = [[] for _ in range(6)]
    new_s = [[] for _ in range(6)]
    for l in range(DEPTH):
        wl = (norm_g[l], w_in[l], conv_a_w[l], conv_c_w[l], conv_c_b[l], ln_c_g[l], ln_c_b[l],
              ret_gn_g[l], ml_gn_g[l], ml_gate_b[l], w_br[l], w_out[l])
        hp, st_p = hybrid_layer(hp, pos_p, z_buf_a, z_ret, z_buf_c, z_c, z_n, z_m, *wl)
        hs, st_s = hybrid_layer(hs, pos_s, state_conv_a[l], state_ret[l], state_conv_c[l],
                                state_mlstm_C[l], state_mlstm_n[l], state_mlstm_m[l], *wl)
        for j in range(6):
            new_p[j].append(st_p[j])
            new_s[j].append(st_s[j])
    y_prompt = rmsnorm(hp, final_g)
    y_sample = rmsnorm(hs, final_g)
    sp = [jnp.stack(a, axis=0) for a in new_p]
    ss = [jnp.stack(a, axis=0) for a in new_s]
    return (y_prompt, y_sample, sp[0], ss[0], sp[1], ss[1], sp[2], ss[2],
            sp[3], ss[3], sp[4], ss[4], sp[5], ss[5])
```

```python
import functools

import numpy as np
import jax
import jax.numpy as jnp
from jax import lax
from jax.experimental import pallas as pl
from jax.experimental.pallas import tpu as pltpu

F32 = jnp.float32
BF16 = jnp.bfloat16

D_MODEL = 1024
BRANCH_W = 512
N_BRANCH = 4
HEADS = 4
HEAD_D = 128
CONV_A_W = 3
CONV_C_W = 31
ROPE_BASE = 10000.0
EPS = 1e-6
N_MAIN = 16 * BRANCH_W
N_GATE = N_BRANCH * D_MODEL
ROWS = 128
SAMPLE_SEQS = 32
VMEM_LIMIT = 56 * 1024 * 1024

_COL = dict(a_b=0, a_c=512, a_u=1024, a_z=1536, r_q=2048, r_k=2560, r_v=3072, r_z=3584,
            glu_v=4096, glu_g=4608, c_z=5120, m_q=5632, m_k=6144, m_v=6656, m_o=7168, m_z=7680)


def _bdot(a, b):
    return jnp.dot(a, b, preferred_element_type=F32)


def _bdot_nt(a, b):
    return lax.dot_general(a, b, (((1,), (1,)), ((), ())), preferred_element_type=F32)


def _exact_mm(mat_b16, x):
    hi = x.astype(BF16)
    r1 = x - hi.astype(F32)
    mid = r1.astype(BF16)
    lo = (r1 - mid.astype(F32)).astype(BF16)
    return _bdot(mat_b16, hi) + _bdot(mat_b16, mid) + _bdot(mat_b16, lo)


def _silu(x):
    return x * jax.nn.sigmoid(x)


def _log_sigmoid(x):
    return jnp.minimum(x, 0.0) - jnp.log1p(jnp.exp(-jnp.abs(x)))


def _head_norm(y):
    mu = jnp.mean(y, axis=-1, keepdims=True)
    yc = y - mu
    var = jnp.mean(yc * yc, axis=-1, keepdims=True)
    return yc * lax.rsqrt(var + EPS)


def _rotary(x, c2, s2):
    return x * c2 + pltpu.roll(x, HEAD_D // 2, 1) * s2


def _inproj_kernel(x_ref, g_ref, w_ref, wif_ref, p_ref, pif_ref, hb_ref):
    @pl.when(pl.program_id(1) == 0)
    def _():
        x = x_ref[...]
        ms = jnp.mean(x * x, axis=-1, keepdims=True)
        hb = ((x * lax.rsqrt(ms + EPS)) * g_ref[...]).astype(BF16)
        hb_ref[...] = hb
        pif_ref[...] = _bdot(hb, wif_ref[...])

    p_ref[...] = _bdot(hb_ref[...], w_ref[...])


def _inproj(x, g, w, wif, tm, tn=2048):
    m = x.shape[0]
    n = w.shape[1]
    nif = wif.shape[1]
    return pl.pallas_call(
        _inproj_kernel,
        grid=(m // tm, n // tn),
        in_specs=[
            pl.BlockSpec((tm, D_MODEL), lambda i, j: (i, 0)),
            pl.BlockSpec((1, D_MODEL), lambda i, j: (0, 0)),
            pl.BlockSpec((D_MODEL, tn), lambda i, j: (0, j)),
            pl.BlockSpec((D_MODEL, nif), lambda i, j: (0, 0)),
        ],
        out_specs=[
            pl.BlockSpec((tm, tn), lambda i, j: (i, j)),
            pl.BlockSpec((tm, nif), lambda i, j: (i, 0)),
        ],
        out_shape=[jax.ShapeDtypeStruct((m, n), F32), jax.ShapeDtypeStruct((m, nif), F32)],
        scratch_shapes=[pltpu.VMEM((tm, D_MODEL), BF16)],
        compiler_params=pltpu.CompilerParams(
            dimension_semantics=("arbitrary", "arbitrary"), vmem_limit_bytes=VMEM_LIMIT),
        name="inproj",
    )(x, g, w, wif)


def _merge_kernel(x_ref, g_ref, o_ref, wbr_ref, wout_ref, fg_ref, y_ref, *, final):
    acc = None
    for n in range(N_BRANCH):
        proj = _bdot(o_ref[:, n * BRANCH_W:(n + 1) * BRANCH_W], wbr_ref[n])
        term = jax.nn.sigmoid(g_ref[:, n * D_MODEL:(n + 1) * D_MODEL]) * proj
        acc = term if acc is None else acc + term
    y = x_ref[...] + _bdot(acc.astype(BF16), wout_ref[...])
    if final:
        ms = jnp.mean(y * y, axis=-1, keepdims=True)
        y = (y * lax.rsqrt(ms + EPS)) * fg_ref[...]
    y_ref[...] = y


def _merge(x, p, o, wbr, wout, fg, final, tm=256):
    m = x.shape[0]
    gate_block = N_MAIN // N_GATE
    return pl.pallas_call(
        functools.partial(_merge_kernel, final=final),
        grid=(m // tm,),
        in_specs=[
            pl.BlockSpec((tm, D_MODEL), lambda i: (i, 0)),
            pl.BlockSpec((tm, N_GATE), lambda i: (i, gate_block)),
            pl.BlockSpec((tm, N_BRANCH * BRANCH_W), lambda i: (i, 0)),
            pl.BlockSpec((N_BRANCH, BRANCH_W, D_MODEL), lambda i: (0, 0, 0)),
            pl.BlockSpec((D_MODEL, D_MODEL), lambda i: (0, 0)),
            pl.BlockSpec((1, D_MODEL), lambda i: (0, 0)),
        ],
        out_specs=pl.BlockSpec((tm, D_MODEL), lambda i: (i, 0)),
        out_shape=jax.ShapeDtypeStruct((m, D_MODEL), F32),
        compiler_params=pltpu.CompilerParams(
            dimension_semantics=("arbitrary",), vmem_limit_bytes=VMEM_LIMIT),
        name="merge",
    )(x, p, o, wbr, wout, fg)


def _tile_masks(sample):
    row = lax.broadcasted_iota(jnp.int32, (ROWS, ROWS), 0)
    col = lax.broadcasted_iota(jnp.int32, (ROWS, ROWS), 1)
    if not sample:
        return row >= col, None
    same = (row & (SAMPLE_SEQS - 1)) == (col & (SAMPLE_SEQS - 1))
    return jnp.logical_and(same, row >= col), same


def _ml_prelims(pi, pf, bi, bfo, cmask, smask):
    ai = pi + bi
    lf = _log_sigmoid(pf + bfo)
    tri = jnp.where(cmask, 1.0, 0.0).astype(BF16)
    if smask is None:
        bones = jnp.ones((ROWS, ROWS), BF16)
    else:
        bones = jnp.where(smask, 1.0, 0.0).astype(BF16)
    fcum = _exact_mm(tri, lf)
    ftot = _exact_mm(bones, lf)
    g = (ftot - fcum) + ai
    return dict(ai_t=ai.T, fcum=fcum, fcum_t=fcum.T, ftot=ftot, g=g, g_t=g.T, bones=bones)


def _ml_gates(pre, col, m_prev_c, cmask, smask):
    f_c = pre["fcum"][:, col:col + 1]
    f_r = pre["fcum_t"][col:col + 1, :]
    ig_r = pre["ai_t"][col:col + 1, :]
    g_c = pre["g"][:, col:col + 1]
    g_r = pre["g_t"][col:col + 1, :]
    ftot_c = pre["ftot"][:, col:col + 1]
    logw = jnp.where(cmask, (f_c - f_r) + ig_r, -jnp.inf)
    b = f_c + m_prev_c
    m_t = jnp.maximum(jnp.max(logw, axis=1, keepdims=True), b)
    wts = jnp.exp(logw - m_t)
    inter = jnp.exp(b - m_t)
    if smask is None:
        gmax = jnp.max(g_r, axis=1, keepdims=True)
    else:
        gmax = jnp.max(jnp.where(smask, g_r, -jnp.inf), axis=1, keepdims=True)
    m_new = jnp.maximum(ftot_c + m_prev_c, gmax)
    ws = jnp.exp(g_c - m_new)
    dec = jnp.exp((ftot_c + m_prev_c) - m_new)
    return dict(m_t=m_t, wts=wts, inter=inter, m_new=m_new, ws=ws, dec=dec)


def _ml_output(gt, qb, kb, vb, inter_term, qn):
    s = _bdot_nt(qb, kb) * gt["wts"]
    num = _bdot(s.astype(BF16), vb) + inter_term * gt["inter"]
    den = jnp.sum(s, axis=1, keepdims=True) + qn * gt["inter"]
    return num / jnp.maximum(jnp.abs(den), jnp.exp(-gt["m_t"]))


def _branch_c_post(conv, c_z, cb, lng, lnb):
    y = conv + cb
    mu = jnp.mean(y, axis=-1, keepdims=True)
    yc = y - mu
    var = jnp.mean(yc * yc, axis=-1, keepdims=True)
    ln = (yc * lax.rsqrt(var + EPS)) * lng + lnb
    return _silu(c_z) * _silu(ln)


def _mix_prompt_kernel(pm_ref, pif_ref, c2_ref, s2_ref, decay_ref, qdec_ref, kdec_ref,
                       wa_ref, wc_ref, cb_ref, lng_ref, lnb_ref, rgn_ref, mgn_ref, bi_ref, bf_ref,
                       o_ref, bufa_ref, sret_ref, bufc_ref, cml_ref, nml_ref, mml_ref,
                       exta, extc, s_scr, c_scr, n_scr, m_scr, *, cdec):
    c = pl.program_id(1)
    last = pl.num_programs(1) - 1

    @pl.when(c == 0)
    def _():
        exta[0:8, :] = jnp.zeros((8, BRANCH_W), F32)
        extc[0:32, :] = jnp.zeros((32, BRANCH_W), F32)
        s_scr[...] = jnp.zeros_like(s_scr)
        c_scr[...] = jnp.zeros_like(c_scr)
        n_scr[...] = jnp.zeros_like(n_scr)
        m_scr[...] = jnp.zeros_like(m_scr)

    ua = pm_ref[:, 512:1024] * pm_ref[:, 1024:1536]
    exta[8:8 + ROWS, :] = ua
    conv_a = (wa_ref[0:1, :] * exta[6:6 + ROWS, :] + wa_ref[1:2, :] * exta[7:7 + ROWS, :]
              + wa_ref[2:3, :] * ua)
    o_ref[:, 0:512] = (_silu(pm_ref[:, 1536:2048]) * (pm_ref[:, 0:512] * conv_a)).astype(BF16)
    exta[0:8, :] = exta[ROWS:ROWS + 8, :]

    uc = pm_ref[:, 4096:4608] * jax.nn.sigmoid(pm_ref[:, 4608:5120])
    extc[32:32 + ROWS, :] = uc
    conv_c = wc_ref[0:1, :] * extc[2:2 + ROWS, :]
    for j in range(1, CONV_C_W):
        conv_c = conv_c + wc_ref[j:j + 1, :] * extc[2 + j:2 + j + ROWS, :]
    o_ref[:, 1024:1536] = _branch_c_post(conv_c, pm_ref[:, 5120:5632], cb_ref[...], lng_ref[...],
                                         lnb_ref[...]).astype(BF16)

    @pl.when(c == last)
    def _():
        bufa_ref[0] = exta[ROWS + 6:ROWS + 8, :]
        bufc_ref[0] = extc[ROWS + 2:ROWS + 32, :]

    extc[0:32, :] = extc[ROWS:ROWS + 32, :]

    c2 = c2_ref[...]
    s2 = s2_ref[...]
    for h in range(HEADS):
        lo = h * HEAD_D
        q = _rotary(pm_ref[:, 2048 + lo:2048 + lo + HEAD_D], c2, s2)
        k = _rotary(pm_ref[:, 2560 + lo:2560 + lo + HEAD_D], c2, s2) * (HEAD_D ** -0.5)
        qb = q.astype(BF16)
        kb = k.astype(BF16)
        vb = pm_ref[:, 3072 + lo:3072 + lo + HEAD_D].astype(BF16)
        s_old = s_scr[h]
        sc = _bdot_nt(qb, kb) * decay_ref[h]
        ret = _bdot(sc.astype(BF16), vb) + _bdot(qb, s_old.astype(BF16)) * qdec_ref[:, lo:lo + HEAD_D]
        s_scr[h] = s_old * cdec[h] + _bdot((k * kdec_ref[:, lo:lo + HEAD_D]).T.astype(BF16), vb)
        o_ref[:, 512 + lo:512 + lo + HEAD_D] = (
            _silu(pm_ref[:, 3584 + lo:3584 + lo + HEAD_D])
            * (_head_norm(ret) * rgn_ref[:, lo:lo + HEAD_D])).astype(BF16)

    cmask, smask = _tile_masks(False)
    pre = _ml_prelims(pif_ref[:, 0:128], pif_ref[:, 128:256], bi_ref[...], bf_ref[...], cmask, smask)
    lane = lax.broadcasted_iota(jnp.int32, (ROWS, ROWS), 1)
    m_rows = m_scr[...]
    m_rows_new = jnp.zeros((ROWS, ROWS), F32)
    for h in range(HEADS):
        lo = h * HEAD_D
        gt = _ml_gates(pre, h, m_rows[:, h:h + 1], cmask, smask)
        q = pm_ref[:, 5632 + lo:5632 + lo + HEAD_D]
        k = pm_ref[:, 6144 + lo:6144 + lo + HEAD_D] * (HEAD_D ** -0.5)
        qb = q.astype(BF16)
        kb = k.astype(BF16)
        vb = pm_ref[:, 6656 + lo:6656 + lo + HEAD_D].astype(BF16)
        c_old = c_scr[h]
        n_old = n_scr[:, lo:lo + HEAD_D]
        qn = jnp.sum(q * n_old, axis=1, keepdims=True)
        hout = _ml_output(gt, qb, kb, vb, _bdot(qb, c_old.astype(BF16)), qn)
        kw = k * gt["ws"]
        kwb = kw.astype(BF16)
        dec_full = jnp.broadcast_to(gt["dec"], (ROWS, ROWS))
        c_scr[h] = c_old * dec_full + _bdot(kw.T.astype(BF16), vb)
        n_scr[:, lo:lo + HEAD_D] = n_old * dec_full + _bdot(pre["bones"], kwb)
        m_rows_new = jnp.where(lane == h, jnp.broadcast_to(gt["m_new"], (ROWS, ROWS)), m_rows_new)
        hm = jax.nn.sigmoid(pm_ref[:, 7168 + lo:7168 + lo + HEAD_D]) * hout
        o_ref[:, 1536 + lo:1536 + lo + HEAD_D] = (
            _silu(pm_ref[:, 7680 + lo:7680 + lo + HEAD_D])
            * (_head_norm(hm) * mgn_ref[:, lo:lo + HEAD_D])).astype(BF16)
    m_scr[...] = m_rows_new

    @pl.when(c == last)
    def _():
        for h in range(HEADS):
            sret_ref[0, h] = s_scr[h]
            cml_ref[0, h] = c_scr[h]
            nml_ref[0, h:h + 1, :] = n_scr[0:1, h * HEAD_D:(h + 1) * HEAD_D]
        mml_ref[0] = m_rows_new[0:1, :]


def _mix_prompt(p, pif, consts, prm, batch, seq):
    nchunk = seq // ROWS
    rows = lambda b, c: (b * nchunk + c, 0)
    const2 = lambda b, c: (0, 0)
    const3 = lambda b, c: (0, 0, 0)
    full = lambda shape, imap: pl.BlockSpec(shape, imap)
    in_specs = [
        full((ROWS, N_MAIN), rows),
        full((ROWS, 256), rows),
        full((ROWS, HEAD_D), lambda b, c: (c, 0)),
        full((ROWS, HEAD_D), lambda b, c: (c, 0)),
        full((HEADS, ROWS, ROWS), const3),
        full((ROWS, BRANCH_W), const2),
        full((ROWS, BRANCH_W), const2),
        full((CONV_A_W, BRANCH_W), const2),
        full((CONV_C_W, BRANCH_W), const2),
    ] + [full((1, BRANCH_W), const2)] * 5 + [full((1, 128), const2)] * 2
    per_b3 = lambda b, c: (b, 0, 0)
    per_b4 = lambda b, c: (b, 0, 0, 0)
    out_specs = [
        full((ROWS, N_BRANCH * BRANCH_W), rows),
        full((1, CONV_A_W - 1, BRANCH_W), per_b3),
        full((1, HEADS, HEAD_D, HEAD_D), per_b4),
        full((1, CONV_C_W - 1, BRANCH_W), per_b3),
        full((1, HEADS, HEAD_D, HEAD_D), per_b4),
        full((1, HEADS, HEAD_D), per_b3),
        full((1, 1, 128), per_b3),
    ]
    out_shape = [
        jax.ShapeDtypeStruct((batch * seq, N_BRANCH * BRANCH_W), BF16),
        jax.ShapeDtypeStruct((batch, CONV_A_W - 1, BRANCH_W), F32),
        jax.ShapeDtypeStruct((batch, HEADS, HEAD_D, HEAD_D), F32),
        jax.ShapeDtypeStruct((batch, CONV_C_W - 1, BRANCH_W), F32),
        jax.ShapeDtypeStruct((batch, HEADS, HEAD_D, HEAD_D), F32),
        jax.ShapeDtypeStruct((batch, HEADS, HEAD_D), F32),
        jax.ShapeDtypeStruct((batch, 1, 128), F32),
    ]
    scratch = [
        pltpu.VMEM((ROWS + 8, BRANCH_W), F32),
        pltpu.VMEM((ROWS + 32, BRANCH_W), F32),
        pltpu.VMEM((HEADS, HEAD_D, HEAD_D), F32),
        pltpu.VMEM((HEADS, HEAD_D, HEAD_D), F32),
        pltpu.VMEM((ROWS, BRANCH_W), F32),
        pltpu.VMEM((ROWS, ROWS), F32),
    ]
    return pl.pallas_call(
        functools.partial(_mix_prompt_kernel, cdec=consts["cdec"]),
        grid=(batch, nchunk),
        in_specs=in_specs, out_specs=out_specs, out_shape=out_shape, scratch_shapes=scratch,
        compiler_params=pltpu.CompilerParams(
            dimension_semantics=("arbitrary", "arbitrary"), vmem_limit_bytes=VMEM_LIMIT),
        name="mix_prompt",
    )(p, pif, consts["c2"], consts["s2"], consts["decay"], consts["qdec"], consts["kdec"],
      prm["wa"], prm["wc"], prm["cb"], prm["lng"], prm["lnb"], prm["rgn"], prm["mgn"],
      prm["bi"], prm["bf"])


def _mix_sample_kernel(pa_ref, pc_ref, rq_ref, rk_ref, rv_ref, rz_ref, mq_ref, mk_ref, mv_ref,
                       mo_ref, mz_ref, pif_ref, c2_ref, s2_ref, decay_ref, qdec_ref, kdec_ref,
                       cdec_ref, wa_ref, wc_ref, cb_ref, lng_ref, lnb_ref, rgn_ref, mgn_ref,
                       bi_ref, bf_ref, bufa_ref, sret_ref, bufc_ref, cml_ref, nrows_ref, mrows_ref,
                       o_ref, bufa_o, sret_o, bufc_o, cml_o, nrows_o, mrows_o, dec_scr):
    h = pl.program_id(1)
    sq = SAMPLE_SEQS
    nt = ROWS // sq

    @pl.when(h == 0)
    def _():
        ua = pa_ref[:, 512:1024] * pa_ref[:, 1024:1536]
        ea = [bufa_ref[:, 0, :], bufa_ref[:, 1, :]] + [ua[t * sq:(t + 1) * sq, :] for t in range(nt)]
        conv_a = jnp.concatenate(
            [wa_ref[0:1, :] * ea[t] + wa_ref[1:2, :] * ea[t + 1] + wa_ref[2:3, :] * ea[t + 2]
             for t in range(nt)], axis=0)
        o_ref[:, 0:512] = (_silu(pa_ref[:, 1536:2048]) * (pa_ref[:, 0:512] * conv_a)).astype(BF16)
        bufa_o[:, 0, :] = ea[nt]
        bufa_o[:, 1, :] = ea[nt + 1]
        uc = pc_ref[:, 0:512] * jax.nn.sigmoid(pc_ref[:, 512:1024])
        nb = CONV_C_W - 1
        ec = [bufc_ref[:, i, :] for i in range(nb)] + [uc[t * sq:(t + 1) * sq, :] for t in range(nt)]
        planes = []
        for t in range(nt):
            acc = wc_ref[0:1, :] * ec[t]
            for j in range(1, CONV_C_W):
                acc = acc + wc_ref[j:j + 1, :] * ec[t + j]
            planes.append(acc)
        conv_c = jnp.concatenate(planes, axis=0)
        o_ref[:, 1024:1536] = _branch_c_post(conv_c, pc_ref[:, 1024:1536], cb_ref[...], lng_ref[...],
                                             lnb_ref[...]).astype(BF16)
        for i in range(nb):
            bufc_o[:, i, :] = ec[i + nt]

    cmask, smask = _tile_masks(True)
    rowseq = lax.broadcasted_iota(jnp.int32, (ROWS, ROWS), 0) & (sq - 1)
    laneseq = lax.broadcasted_iota(jnp.int32, (ROWS, ROWS), 1) & (sq - 1)

    c2 = c2_ref[...]
    s2 = s2_ref[...]
    rq = _rotary(rq_ref[...], c2, s2)
    rk = _rotary(rk_ref[...], c2, s2) * (HEAD_D ** -0.5)
    rqb = rq.astype(BF16)
    rkb = rk.astype(BF16)
    rvb = rv_ref[...].astype(BF16)
    rkd_t = (rk * kdec_ref[...]).T
    cdec = cdec_ref[0, 0:1, :]

    pre = _ml_prelims(pif_ref[:, 0:128], pif_ref[:, 128:256], bi_ref[...], bf_ref[...], cmask, smask)
    m_prev = mrows_ref[...]
    gt = _ml_gates(pre, 0, m_prev[:, 0:1], cmask, smask)
    mq = mq_ref[...]
    mk = mk_ref[...] * (HEAD_D ** -0.5)
    mqb = mq.astype(BF16)
    mkb = mk.astype(BF16)
    mvb = mv_ref[...].astype(BF16)
    kw = mk * gt["ws"]
    kw_t = kw.T
    dec_full = jnp.broadcast_to(gt["dec"], (ROWS, ROWS))
    dec_scr[...] = dec_full

    def per_seq(b, carry):
        inter_r, inter_m = carry
        sb = sret_ref[b, 0]
        inter_r = jnp.where(rowseq == b, _bdot(rqb, sb.astype(BF16)), inter_r)
        k_sel = jnp.where(laneseq == b, rkd_t, 0.0).astype(BF16)
        sret_o[b, 0] = sb * cdec + _bdot(k_sel, rvb)
        cb_ = cml_ref[b, 0]
        inter_m = jnp.where(rowseq == b, _bdot(mqb, cb_.astype(BF16)), inter_m)
        kw_sel = jnp.where(laneseq == b, kw_t, 0.0).astype(BF16)
        cml_o[b, 0] = cb_ * dec_scr[pl.ds(b, 1), :] + _bdot(kw_sel, mvb)
        return inter_r, inter_m

    zero = jnp.zeros((ROWS, HEAD_D), F32)
    inter_r, inter_m = lax.fori_loop(0, sq, per_seq, (zero, zero))

    sc = _bdot_nt(rqb, rkb) * decay_ref[0]
    ret = _bdot(sc.astype(BF16), rvb) + inter_r * qdec_ref[...]
    o_r = (_silu(rz_ref[...]) * (_head_norm(ret) * rgn_ref[...])).astype(BF16)

    n_old = nrows_ref[...]
    qn = jnp.sum(mq * n_old, axis=1, keepdims=True)
    hout = _ml_output(gt, mqb, mkb, mvb, inter_m, qn)
    nrows_o[...] = n_old * dec_full + _bdot(pre["bones"], kw.astype(BF16))
    mrows_o[...] = jnp.broadcast_to(gt["m_new"], (ROWS, ROWS))
    hm = jax.nn.sigmoid(mo_ref[...]) * hout
    o_m = (_silu(mz_ref[...]) * (_head_norm(hm) * mgn_ref[...])).astype(BF16)

    for hh in range(HEADS):
        @pl.when(h == hh)
        def _(hh=hh):
            o_ref[:, 512 + hh * HEAD_D:512 + (hh + 1) * HEAD_D] = o_r
            o_ref[:, 1536 + hh * HEAD_D:1536 + (hh + 1) * HEAD_D] = o_m


def _mix_sample(p, pif, consts, prm, st, nseq):
    ntile = nseq // SAMPLE_SEQS
    sq = SAMPLE_SEQS
    head_cols = lambda name: pl.BlockSpec(
        (ROWS, HEAD_D), lambda i, h, _o=_COL[name] // HEAD_D: (i, _o + h))
    const2 = lambda i, h: (0, 0)
    per_head2 = lambda i, h: (0, h)
    in_specs = [
        pl.BlockSpec((ROWS, 2048), lambda i, h: (i, 0)),
        pl.BlockSpec((ROWS, 2048), lambda i, h: (i, 2)),
    ] + [head_cols(n) for n in ("r_q", "r_k", "r_v", "r_z", "m_q", "m_k", "m_v", "m_o", "m_z")] + [
        pl.BlockSpec((ROWS, 256), lambda i, h: (i, h)),
        pl.BlockSpec((ROWS, HEAD_D), const2),
        pl.BlockSpec((ROWS, HEAD_D), const2),
        pl.BlockSpec((1, ROWS, ROWS), lambda i, h: (h, 0, 0)),
        pl.BlockSpec((ROWS, HEAD_D), per_head2),
        pl.BlockSpec((ROWS, HEAD_D), per_head2),
        pl.BlockSpec((1, 8, 128), lambda i, h: (h, 0, 0)),
        pl.BlockSpec((CONV_A_W, BRANCH_W), const2),
        pl.BlockSpec((CONV_C_W, BRANCH_W), const2),
        pl.BlockSpec((1, BRANCH_W), const2),
        pl.BlockSpec((1, BRANCH_W), const2),
        pl.BlockSpec((1, BRANCH_W), const2),
        pl.BlockSpec((1, HEAD_D), per_head2),
        pl.BlockSpec((1, HEAD_D), per_head2),
        pl.BlockSpec((1, 128), per_head2),
        pl.BlockSpec((1, 128), per_head2),
        pl.BlockSpec((sq, CONV_A_W - 1, BRANCH_W), lambda i, h: (i, 0, 0)),
        pl.BlockSpec((sq, 1, HEAD_D, HEAD_D), lambda i, h: (i, h, 0, 0)),
        pl.BlockSpec((sq, CONV_C_W - 1, BRANCH_W), lambda i, h: (i, 0, 0)),
        pl.BlockSpec((sq, 1, HEAD_D, HEAD_D), lambda i, h: (i, h, 0, 0)),
        pl.BlockSpec((ROWS, HEAD_D), lambda i, h: (i, h)),
        pl.BlockSpec((ROWS, HEAD_D), lambda i, h: (i, h)),
    ]
    out_specs = [
        pl.BlockSpec((ROWS, N_BRANCH * BRANCH_W), lambda i, h: (i, 0)),
        pl.BlockSpec((sq, CONV_A_W - 1, BRANCH_W), lambda i, h: (i, 0, 0)),
        pl.BlockSpec((sq, 1, HEAD_D, HEAD_D), lambda i, h: (i, h, 0, 0)),
        pl.BlockSpec((sq, CONV_C_W - 1, BRANCH_W), lambda i, h: (i, 0, 0)),
        pl.BlockSpec((sq, 1, HEAD_D, HEAD_D), lambda i, h: (i, h, 0, 0)),
        pl.BlockSpec((ROWS, HEAD_D), lambda i, h: (i, h)),
        pl.BlockSpec((ROWS, HEAD_D), lambda i, h: (i, h)),
    ]
    nrow = ntile * ROWS
    out_shape = [
        jax.ShapeDtypeStruct((nrow, N_BRANCH * BRANCH_W), BF16),
        jax.ShapeDtypeStruct((nseq, CONV_A_W - 1, BRANCH_W), F32),
        jax.ShapeDtypeStruct((nseq, HEADS, HEAD_D, HEAD_D), F32),
        jax.ShapeDtypeStruct((nseq, CONV_C_W - 1, BRANCH_W), F32),
        jax.ShapeDtypeStruct((nseq, HEADS, HEAD_D, HEAD_D), F32),
        jax.ShapeDtypeStruct((nrow, HEADS * HEAD_D), F32),
        jax.ShapeDtypeStruct((nrow, HEADS * HEAD_D), F32),
    ]
    return pl.pallas_call(
        _mix_sample_kernel,
        grid=(ntile, HEADS),
        in_specs=in_specs, out_specs=out_specs, out_shape=out_shape,
        scratch_shapes=[pltpu.VMEM((ROWS, ROWS), F32)],
        compiler_params=pltpu.CompilerParams(
            dimension_semantics=("arbitrary", "arbitrary"), vmem_limit_bytes=VMEM_LIMIT),
        name="mix_sample",
    )(p, p, p, p, p, p, p, p, p, p, p, pif, consts["c2"], consts["s2"], consts["decay"],
      consts["qdec"], consts["kdec"], consts["cdec"], prm["wa"], prm["wc"], prm["cb"], prm["lng"],
      prm["lnb"], prm["rgn"], prm["mgn"], prm["bi_h"], prm["bf_h"],
      st["bufa"], st["sret"], st["bufc"], st["cml"], st["nrows"], st["mrows"])


def _rope_tables(pos):
    inv = ROPE_BASE ** (-jnp.arange(0, HEAD_D, 2, dtype=F32) / HEAD_D)
    ang = pos.astype(F32)[:, None] * inv[None, :]
    cos = jnp.cos(ang)
    sin = jnp.sin(ang)
    return jnp.concatenate([cos, cos], axis=1), jnp.concatenate([-sin, sin], axis=1)


def _decay_tables(t_of_row, seq_of_row, chunk_len):
    log_g = np.log1p(-(2.0 ** (-5.0 - np.arange(HEADS, dtype=np.float64))))
    dt = t_of_row[:, None] - t_of_row[None, :]
    ok = (seq_of_row[:, None] == seq_of_row[None, :]) & (dt >= 0)
    decay = np.where(ok[None], np.exp(np.maximum(dt, 0)[None] * log_g[:, None, None]), 0.0)
    qdec = np.exp((t_of_row[:, None] + 1.0) * log_g[None, :])
    kdec = np.exp((chunk_len - 1.0 - t_of_row)[:, None] * log_g[None, :])
    cdec = np.exp(chunk_len * log_g)
    rep = lambda a: jnp.asarray(np.repeat(a, HEAD_D, axis=1), F32)
    return jnp.asarray(decay, F32), rep(qdec), rep(kdec), cdec


def kernel(x_prompt, x_sample, state_conv_a, state_ret, state_conv_c, state_mlstm_C,
           state_mlstm_n, state_mlstm_m, norm_g, w_in, conv_a_w, conv_c_w, conv_c_b,
           ln_c_g, ln_c_b, ret_gn_g, ml_gn_g, ml_gate_b, w_br, w_out, final_g):
    bp, tp, _ = x_prompt.shape
    bs, ts, _ = x_sample.shape
    depth = w_in.shape[0]
    past_len = 16384
    sq = SAMPLE_SEQS
    ntile = bs // sq
    assert tp % ROWS == 0 and ts * sq == ROWS and bs % sq == 0

    r = np.arange(ROWS)
    c2p, s2p = _rope_tables(jnp.arange(tp, dtype=jnp.int32))
    decay_p, qdec_p, kdec_p, cdec_p = _decay_tables(r.astype(np.float64), np.zeros(ROWS), float(ROWS))
    consts_p = dict(c2=c2p, s2=s2p, decay=decay_p, qdec=qdec_p, kdec=kdec_p,
                    cdec=tuple(float(v) for v in cdec_p))
    t_s = (r // sq).astype(np.float64)
    c2s, s2s = _rope_tables(past_len + jnp.asarray(r // sq, jnp.int32))
    decay_s, qdec_s, kdec_s, cdec_s = _decay_tables(t_s, r % sq, float(ts))
    cdec_s_arr = jnp.asarray(np.broadcast_to(cdec_s[:, None, None], (HEADS, 8, 128)), F32)
    consts_s = dict(c2=c2s, s2=s2s, decay=decay_s, qdec=qdec_s, kdec=kdec_s, cdec=cdec_s_arr)

    xp = x_prompt.reshape(bp * tp, D_MODEL)
    xs = x_sample.reshape(ntile, sq, ts, D_MODEL).transpose(0, 2, 1, 3).reshape(bs * ts, D_MODEL)

    fg = final_g.reshape(1, D_MODEL)
    outs_p = [[] for _ in range(6)]
    outs_s = [[] for _ in range(6)]
    for l in range(depth):
        wl = w_in[l]
        w_mg = jnp.concatenate([wl[:, :N_MAIN], wl[:, N_MAIN + 2 * HEADS:]], axis=1).astype(BF16)
        wi = wl[:, N_MAIN:N_MAIN + HEADS]
        wf = wl[:, N_MAIN + HEADS:N_MAIN + 2 * HEADS]
        zpad = jnp.zeros((D_MODEL, 128 - HEADS), F32)
        wif_p = jnp.concatenate([wi, zpad, wf, zpad], axis=1).astype(BF16)
        z127 = jnp.zeros((D_MODEL, 127), F32)
        wif_s = jnp.concatenate(
            [blk for h in range(HEADS) for blk in (wi[:, h:h + 1], z127, wf[:, h:h + 1], z127)],
            axis=1).astype(BF16)
        gb = ml_gate_b[l]
        pad_row = lambda v: jnp.pad(v, (0, 128 - v.shape[0])).reshape(1, 128)
        prm = dict(
            wa=conv_a_w[l], wc=conv_c_w[l], cb=conv_c_b[l].reshape(1, -1),
            lng=ln_c_g[l].reshape(1, -1), lnb=ln_c_b[l].reshape(1, -1),
            rgn=ret_gn_g[l].reshape(1, -1), mgn=ml_gn_g[l].reshape(1, -1),
            bi=pad_row(gb[:HEADS]), bf=pad_row(gb[HEADS:]),
            bi_h=jnp.concatenate([pad_row(gb[h:h + 1]) for h in range(HEADS)], axis=1),
            bf_h=jnp.concatenate([pad_row(gb[HEADS + h:HEADS + h + 1]) for h in range(HEADS)], axis=1),
        )
        gnorm = norm_g[l].reshape(1, D_MODEL)
        wbr = w_br[l].astype(BF16)
        wout = w_out[l].astype(BF16)
        final = l == depth - 1

        pp, pifp = _inproj(xp, gnorm, w_mg, wif_p, tm=1024)
        res = _mix_prompt(pp, pifp, consts_p, prm, bp, tp)
        o_p = res[0]
        outs_p[0].append(res[1])
        outs_p[1].append(res[2])
        outs_p[2].append(res[3])
        outs_p[3].append(res[4])
        outs_p[4].append(res[5])
        outs_p[5].append(res[6][:, 0, :HEADS])
        xp = _merge(xp, pp, o_p, wbr, wout, fg, final)

        ps, pifs = _inproj(xs, gnorm, w_mg, wif_s, tm=bs * ts)
        n_rows = jnp.broadcast_to(
            state_mlstm_n[l].reshape(ntile, 1, sq, HEADS * HEAD_D),
            (ntile, ts, sq, HEADS * HEAD_D)).reshape(bs * ts, HEADS * HEAD_D)
        m_rows = jnp.broadcast_to(
            state_mlstm_m[l].reshape(ntile, 1, sq, HEADS, 1),
            (ntile, ts, sq, HEADS, HEAD_D)).reshape(bs * ts, HEADS * HEAD_D)
        st = dict(bufa=state_conv_a[l], sret=state_ret[l], bufc=state_conv_c[l],
                  cml=state_mlstm_C[l], nrows=n_rows, mrows=m_rows)
        res = _mix_sample(ps, pifs, consts_s, prm, st, bs)
        o_s = res[0]
        outs_s[0].append(res[1])
        outs_s[1].append(res[2])
        outs_s[2].append(res[3])
        outs_s[3].append(res[4])
        outs_s[4].append(res[5].reshape(ntile, ts, sq, HEADS, HEAD_D)[:, 0].reshape(bs, HEADS, HEAD_D))
        outs_s[5].append(res[6].reshape(ntile, ts, sq, HEADS, HEAD_D)[:, 0, :, :, 0].reshape(bs, HEADS))
        xs = _merge(xs, ps, o_s, wbr, wout, fg, final)

    y_prompt = xp.reshape(bp, tp, D_MODEL)
    y_sample = xs.reshape(ntile, ts, sq, D_MODEL).transpose(0, 2, 1, 3).reshape(bs, ts, D_MODEL)
    sp = [jnp.stack(a, axis=0) for a in outs_p]
    ss = [jnp.stack(a, axis=0) for a in outs_s]
    return (y_prompt, y_sample, sp[0], ss[0], sp[1], ss[1], sp[2], ss[2],
            sp[3], ss[3], sp[4], ss[4], sp[5], ss[5])
```

```python
import functools

import numpy as np
import jax
import jax.numpy as jnp
from jax import lax
from jax.experimental import pallas as pl
from jax.experimental.pallas import tpu as pltpu

F32 = jnp.float32
BF16 = jnp.bfloat16

D_MODEL = 1024
BRANCH_W = 512
N_BRANCH = 4
HEADS = 4
HEAD_D = 128
CONV_A_W = 3
CONV_C_W = 31
ROPE_BASE = 10000.0
EPS = 1e-6
N_MAIN = 16 * BRANCH_W
N_GATE = N_BRANCH * D_MODEL
ROWS = 128
SAMPLE_SEQS = 32
VMEM_LIMIT = 56 * 1024 * 1024

_COL = dict(a_b=0, a_c=512, a_u=1024, a_z=1536, r_q=2048, r_k=2560, r_v=3072, r_z=3584,
            glu_v=4096, glu_g=4608, c_z=5120, m_q=5632, m_k=6144, m_v=6656, m_o=7168, m_z=7680)


def _bdot(a, b):
    return jnp.dot(a, b, preferred_element_type=F32)


def _bdot_nt(a, b):
    return lax.dot_general(a, b, (((1,), (1,)), ((), ())), preferred_element_type=F32)


def _exact_mm(mat_b16, x):
    hi = x.astype(BF16)
    r1 = x - hi.astype(F32)
    mid = r1.astype(BF16)
    lo = (r1 - mid.astype(F32)).astype(BF16)
    return _bdot(mat_b16, hi) + _bdot(mat_b16, mid) + _bdot(mat_b16, lo)


def _silu(x):
    return x * jax.nn.sigmoid(x)


def _log_sigmoid(x):
    return jnp.minimum(x, 0.0) - jnp.log1p(jnp.exp(-jnp.abs(x)))


def _head_norm(y):
    mu = jnp.mean(y, axis=-1, keepdims=True)
    yc = y - mu
    var = jnp.mean(yc * yc, axis=-1, keepdims=True)
    return yc * lax.rsqrt(var + EPS)


def _rotary(x, c2, s2):
    return x * c2 + pltpu.roll(x, HEAD_D // 2, 1) * s2


def _rmsnorm(x, g):
    ms = jnp.mean(x * x, axis=-1, keepdims=True)
    return (x * lax.rsqrt(ms + EPS)) * g


def _inproj_kernel(x_ref, g_ref, w_ref, p_ref, hb_ref):
    @pl.when(pl.program_id(1) == 0)
    def _():
        hb_ref[...] = _rmsnorm(x_ref[...], g_ref[...]).astype(BF16)

    p_ref[...] = _bdot(hb_ref[...], w_ref[...])


def _inproj(x, g, w, tm, tn):
    m = x.shape[0]
    n = w.shape[1]
    return pl.pallas_call(
        _inproj_kernel,
        grid=(m // tm, n // tn),
        in_specs=[
            pl.BlockSpec((tm, D_MODEL), lambda i, j: (i, 0)),
            pl.BlockSpec((1, D_MODEL), lambda i, j: (0, 0)),
            pl.BlockSpec((D_MODEL, tn), lambda i, j: (0, j)),
        ],
        out_specs=pl.BlockSpec((tm, tn), lambda i, j: (i, j)),
        out_shape=jax.ShapeDtypeStruct((m, n), F32),
        scratch_shapes=[pltpu.VMEM((tm, D_MODEL), BF16)],
        compiler_params=pltpu.CompilerParams(
            dimension_semantics=("arbitrary", "arbitrary"), vmem_limit_bytes=VMEM_LIMIT),
        name="inproj",
    )(x, g, w)


def _merge_kernel(x_ref, g_ref, o_ref, wg_ref, wbr_ref, wout_ref, fg_ref, y_ref, *, final):
    x = x_ref[...]
    hb = _rmsnorm(x, g_ref[...]).astype(BF16)
    acc = None
    for n in range(N_BRANCH):
        proj = _bdot(o_ref[:, n * BRANCH_W:(n + 1) * BRANCH_W], wbr_ref[n])
        gate = jax.nn.sigmoid(_bdot(hb, wg_ref[:, n * D_MODEL:(n + 1) * D_MODEL]))
        term = gate * proj
        acc = term if acc is None else acc + term
    y = x + _bdot(acc.astype(BF16), wout_ref[...])
    if final:
        y = _rmsnorm(y, fg_ref[...])
    y_ref[...] = y


def _merge(x, g, o, wg, wbr, wout, fg, final, tm):
    m = x.shape[0]
    resident = lambda shape: pl.BlockSpec(shape, lambda i: (0,) * len(shape),
                                          pipeline_mode=pl.Buffered(1))
    return pl.pallas_call(
        functools.partial(_merge_kernel, final=final),
        grid=(m // tm,),
        in_specs=[
            pl.BlockSpec((tm, D_MODEL), lambda i: (i, 0)),
            resident((1, D_MODEL)),
            pl.BlockSpec((tm, N_BRANCH * BRANCH_W), lambda i: (i, 0)),
            resident((D_MODEL, N_GATE)),
            resident((N_BRANCH, BRANCH_W, D_MODEL)),
            resident((D_MODEL, D_MODEL)),
            resident((1, D_MODEL)),
        ],
        out_specs=pl.BlockSpec((tm, D_MODEL), lambda i: (i, 0)),
        out_shape=jax.ShapeDtypeStruct((m, D_MODEL), F32),
        compiler_params=pltpu.CompilerParams(
            dimension_semantics=("arbitrary",), vmem_limit_bytes=VMEM_LIMIT),
        name="merge",
    )(x, g, o, wg, wbr, wout, fg)


def _tile_masks(sample):
    row = lax.broadcasted_iota(jnp.int32, (ROWS, ROWS), 0)
    col = lax.broadcasted_iota(jnp.int32, (ROWS, ROWS), 1)
    if not sample:
        return row >= col, None
    same = (row & (SAMPLE_SEQS - 1)) == (col & (SAMPLE_SEQS - 1))
    return jnp.logical_and(same, row >= col), same


def _ml_prelims(pi, pf, bi, bfo, cmask, smask):
    ai = pi + bi
    lf = _log_sigmoid(pf + bfo)
    tri = jnp.where(cmask, 1.0, 0.0).astype(BF16)
    if smask is None:
        bones = jnp.ones((ROWS, ROWS), BF16)
    else:
        bones = jnp.where(smask, 1.0, 0.0).astype(BF16)
    fcum = _exact_mm(tri, lf)
    ftot = _exact_mm(bones, lf)
    g = (ftot - fcum) + ai
    return dict(ai_t=ai.T, fcum=fcum, fcum_t=fcum.T, ftot=ftot, g=g, g_t=g.T, bones=bones)


def _ml_gates(pre, col, m_prev_c, cmask, smask):
    f_c = pre["fcum"][:, col:col + 1]
    f_r = pre["fcum_t"][col:col + 1, :]
    ig_r = pre["ai_t"][col:col + 1, :]
    g_c = pre["g"][:, col:col + 1]
    g_r = pre["g_t"][col:col + 1, :]
    ftot_c = pre["ftot"][:, col:col + 1]
    logw = jnp.where(cmask, (f_c - f_r) + ig_r, -jnp.inf)
    b = f_c + m_prev_c
    m_t = jnp.maximum(jnp.max(logw, axis=1, keepdims=True), b)
    wts = jnp.exp(logw - m_t)
    inter = jnp.exp(b - m_t)
    if smask is None:
        gmax = jnp.max(g_r, axis=1, keepdims=True)
    else:
        gmax = jnp.max(jnp.where(smask, g_r, -jnp.inf), axis=1, keepdims=True)
    m_new = jnp.maximum(ftot_c + m_prev_c, gmax)
    ws = jnp.exp(g_c - m_new)
    dec = jnp.exp((ftot_c + m_prev_c) - m_new)
    return dict(m_t=m_t, wts=wts, inter=inter, m_new=m_new, ws=ws, dec=dec)


def _ml_output(gt, qb, kb, vb, inter_term, qn):
    s = _bdot_nt(qb, kb) * gt["wts"]
    num = _bdot(s.astype(BF16), vb) + inter_term * gt["inter"]
    den = jnp.sum(s, axis=1, keepdims=True) + qn * gt["inter"]
    return num / jnp.maximum(jnp.abs(den), jnp.exp(-gt["m_t"]))


def _branch_c_post(conv, c_z, cb, lng, lnb):
    y = conv + cb
    mu = jnp.mean(y, axis=-1, keepdims=True)
    yc = y - mu
    var = jnp.mean(yc * yc, axis=-1, keepdims=True)
    ln = (yc * lax.rsqrt(var + EPS)) * lng + lnb
    return _silu(c_z) * _silu(ln)


def _mix_prompt_kernel(pm_ref, pif_ref, c2_ref, s2_ref, decay_ref, qdec_ref, kdec_ref,
                       wa_ref, wc_ref, cb_ref, lng_ref, lnb_ref, rgn_ref, mgn_ref, bi_ref, bf_ref,
                       o_ref, bufa_ref, sret_ref, bufc_ref, cml_ref, nml_ref, mml_ref,
                       exta, extc, s_scr, c_scr, n_scr, m_scr, *, cdec):
    c = pl.program_id(1)
    last = pl.num_programs(1) - 1

    @pl.when(c == 0)
    def _():
        exta[0:8, :] = jnp.zeros((8, BRANCH_W), F32)
        extc[0:32, :] = jnp.zeros((32, BRANCH_W), F32)
        s_scr[...] = jnp.zeros_like(s_scr)
        c_scr[...] = jnp.zeros_like(c_scr)
        n_scr[...] = jnp.zeros_like(n_scr)
        m_scr[...] = jnp.zeros_like(m_scr)

    ua = pm_ref[:, 512:1024] * pm_ref[:, 1024:1536]
    exta[8:8 + ROWS, :] = ua
    conv_a = (wa_ref[0:1, :] * exta[6:6 + ROWS, :] + wa_ref[1:2, :] * exta[7:7 + ROWS, :]
              + wa_ref[2:3, :] * ua)
    o_ref[:, 0:512] = (_silu(pm_ref[:, 1536:2048]) * (pm_ref[:, 0:512] * conv_a)).astype(BF16)
    exta[0:8, :] = exta[ROWS:ROWS + 8, :]

    uc = pm_ref[:, 4096:4608] * jax.nn.sigmoid(pm_ref[:, 4608:5120])
    extc[32:32 + ROWS, :] = uc
    conv_c = wc_ref[0:1, :] * extc[2:2 + ROWS, :]
    for j in range(1, CONV_C_W):
        conv_c = conv_c + wc_ref[j:j + 1, :] * extc[2 + j:2 + j + ROWS, :]
    o_ref[:, 1024:1536] = _branch_c_post(conv_c, pm_ref[:, 5120:5632], cb_ref[...], lng_ref[...],
                                         lnb_ref[...]).astype(BF16)

    @pl.when(c == last)
    def _():
        bufa_ref[0] = exta[ROWS + 6:ROWS + 8, :]
        bufc_ref[0] = extc[ROWS + 2:ROWS + 32, :]

    extc[0:32, :] = extc[ROWS:ROWS + 32, :]

    c2 = c2_ref[...]
    s2 = s2_ref[...]
    for h in range(HEADS):
        lo = h * HEAD_D
        q = _rotary(pm_ref[:, 2048 + lo:2048 + lo + HEAD_D], c2, s2)
        k = _rotary(pm_ref[:, 2560 + lo:2560 + lo + HEAD_D], c2, s2) * (HEAD_D ** -0.5)
        qb = q.astype(BF16)
        kb = k.astype(BF16)
        vb = pm_ref[:, 3072 + lo:3072 + lo + HEAD_D].astype(BF16)
        s_old = s_scr[h]
        sc = _bdot_nt(qb, kb) * decay_ref[h]
        ret = _bdot(sc.astype(BF16), vb) + _bdot(qb, s_old.astype(BF16)) * qdec_ref[:, lo:lo + HEAD_D]
        s_scr[h] = s_old * cdec[h] + _bdot((k * kdec_ref[:, lo:lo + HEAD_D]).T.astype(BF16), vb)
        o_ref[:, 512 + lo:512 + lo + HEAD_D] = (
            _silu(pm_ref[:, 3584 + lo:3584 + lo + HEAD_D])
            * (_head_norm(ret) * rgn_ref[:, lo:lo + HEAD_D])).astype(BF16)

    cmask, smask = _tile_masks(False)
    pre = _ml_prelims(pif_ref[:, 0:128], pif_ref[:, 128:256], bi_ref[...], bf_ref[...], cmask, smask)
    lane = lax.broadcasted_iota(jnp.int32, (ROWS, ROWS), 1)
    m_rows = m_scr[...]
    m_rows_new = jnp.zeros((ROWS, ROWS), F32)
    for h in range(HEADS):
        lo = h * HEAD_D
        gt = _ml_gates(pre, h, m_rows[:, h:h + 1], cmask, smask)
        q = pm_ref[:, 5632 + lo:5632 + lo + HEAD_D]
        k = pm_ref[:, 6144 + lo:6144 + lo + HEAD_D] * (HEAD_D ** -0.5)
        qb = q.astype(BF16)
        kb = k.astype(BF16)
        vb = pm_ref[:, 6656 + lo:6656 + lo + HEAD_D].astype(BF16)
        c_old = c_scr[h]
        n_old = n_scr[:, lo:lo + HEAD_D]
        qn = jnp.sum(q * n_old, axis=1, keepdims=True)
        hout = _ml_output(gt, qb, kb, vb, _bdot(qb, c_old.astype(BF16)), qn)
        kw = k * gt["ws"]
        kwb = kw.astype(BF16)
        dec_full = jnp.broadcast_to(gt["dec"], (ROWS, ROWS))
        c_scr[h] = c_old * dec_full + _bdot(kw.T.astype(BF16), vb)
        n_scr[:, lo:lo + HEAD_D] = n_old * dec_full + _bdot(pre["bones"], kwb)
        m_rows_new = jnp.where(lane == h, jnp.broadcast_to(gt["m_new"], (ROWS, ROWS)), m_rows_new)
        hm = jax.nn.sigmoid(pm_ref[:, 7168 + lo:7168 + lo + HEAD_D]) * hout
        o_ref[:, 1536 + lo:1536 + lo + HEAD_D] = (
            _silu(pm_ref[:, 7680 + lo:7680 + lo + HEAD_D])
            * (_head_norm(hm) * mgn_ref[:, lo:lo + HEAD_D])).astype(BF16)
    m_scr[...] = m_rows_new

    @pl.when(c == last)
    def _():
        for h in range(HEADS):
            sret_ref[0, h] = s_scr[h]
            cml_ref[0, h] = c_scr[h]
            nml_ref[0, h:h + 1, :] = n_scr[0:1, h * HEAD_D:(h + 1) * HEAD_D]
        mml_ref[0] = m_rows_new[0:1, :]


def _mix_prompt(p, consts, prm, batch, seq):
    nchunk = seq // ROWS
    rows = lambda b, c: (b * nchunk + c, 0)
    const2 = lambda b, c: (0, 0)
    const3 = lambda b, c: (0, 0, 0)
    full = lambda shape, imap: pl.BlockSpec(shape, imap)
    in_specs = [
        full((ROWS, N_MAIN), rows),
        full((ROWS, 256), lambda b, c: (b * nchunk + c, N_MAIN // 256)),
        full((ROWS, HEAD_D), lambda b, c: (c, 0)),
        full((ROWS, HEAD_D), lambda b, c: (c, 0)),
        full((HEADS, ROWS, ROWS), const3),
        full((ROWS, BRANCH_W), const2),
        full((ROWS, BRANCH_W), const2),
        full((CONV_A_W, BRANCH_W), const2),
        full((CONV_C_W, BRANCH_W), const2),
    ] + [full((1, BRANCH_W), const2)] * 5 + [full((1, 128), const2)] * 2
    per_b3 = lambda b, c: (b, 0, 0)
    per_b4 = lambda b, c: (b, 0, 0, 0)
    out_specs = [
        full((ROWS, N_BRANCH * BRANCH_W), rows),
        full((1, CONV_A_W - 1, BRANCH_W), per_b3),
        full((1, HEADS, HEAD_D, HEAD_D), per_b4),
        full((1, CONV_C_W - 1, BRANCH_W), per_b3),
        full((1, HEADS, HEAD_D, HEAD_D), per_b4),
        full((1, HEADS, HEAD_D), per_b3),
        full((1, 1, 128), per_b3),
    ]
    out_shape = [
        jax.ShapeDtypeStruct((batch * seq, N_BRANCH * BRANCH_W), BF16),
        jax.ShapeDtypeStruct((batch, CONV_A_W - 1, BRANCH_W), F32),
        jax.ShapeDtypeStruct((batch, HEADS, HEAD_D, HEAD_D), F32),
        jax.ShapeDtypeStruct((batch, CONV_C_W - 1, BRANCH_W), F32),
        jax.ShapeDtypeStruct((batch, HEADS, HEAD_D, HEAD_D), F32),
        jax.ShapeDtypeStruct((batch, HEADS, HEAD_D), F32),
        jax.ShapeDtypeStruct((batch, 1, 128), F32),
    ]
    scratch = [
        pltpu.VMEM((ROWS + 8, BRANCH_W), F32),
        pltpu.VMEM((ROWS + 32, BRANCH_W), F32),
        pltpu.VMEM((HEADS, HEAD_D, HEAD_D), F32),
        pltpu.VMEM((HEADS, HEAD_D, HEAD_D), F32),
        pltpu.VMEM((ROWS, BRANCH_W), F32),
        pltpu.VMEM((ROWS, ROWS), F32),
    ]
    return pl.pallas_call(
        functools.partial(_mix_prompt_kernel, cdec=consts["cdec"]),
        grid=(batch, nchunk),
        in_specs=in_specs, out_specs=out_specs, out_shape=out_shape, scratch_shapes=scratch,
        compiler_params=pltpu.CompilerParams(
            dimension_semantics=("arbitrary", "arbitrary"), vmem_limit_bytes=VMEM_LIMIT),
        name="mix_prompt",
    )(p, p, consts["c2"], consts["s2"], consts["decay"], consts["qdec"], consts["kdec"],
      prm["wa"], prm["wc"], prm["cb"], prm["lng"], prm["lnb"], prm["rgn"], prm["mgn"],
      prm["bi"], prm["bf"])


def _mix_sample_kernel(pa_ref, pc_ref, rq_ref, rk_ref, rv_ref, rz_ref, mq_ref, mk_ref, mv_ref,
                       mo_ref, mz_ref, pif_ref, c2_ref, s2_ref, decay_ref, qdec_ref, kdec_ref,
                       cdec_ref, wa_ref, wc_ref, cb_ref, lng_ref, lnb_ref, rgn_ref, mgn_ref,
                       bi_ref, bf_ref, bufa_ref, sret_ref, bufc_ref, cml_ref, nrows_ref, mrows_ref,
                       o_ref, bufa_o, sret_o, bufc_o, cml_o, nrows_o, mrows_o, dec_scr):
    h = pl.program_id(1)
    sq = SAMPLE_SEQS
    nt = ROWS // sq

    @pl.when(h == 0)
    def _():
        ua = pa_ref[:, 512:1024] * pa_ref[:, 1024:1536]
        ea = [bufa_ref[:, 0, :], bufa_ref[:, 1, :]] + [ua[t * sq:(t + 1) * sq, :] for t in range(nt)]
        conv_a = jnp.concatenate(
            [wa_ref[0:1, :] * ea[t] + wa_ref[1:2, :] * ea[t + 1] + wa_ref[2:3, :] * ea[t + 2]
             for t in range(nt)], axis=0)
        o_ref[:, 0:512] = (_silu(pa_ref[:, 1536:2048]) * (pa_ref[:, 0:512] * conv_a)).astype(BF16)
        bufa_o[:, 0, :] = ea[nt]
        bufa_o[:, 1, :] = ea[nt + 1]
        uc = pc_ref[:, 0:512] * jax.nn.sigmoid(pc_ref[:, 512:1024])
        nb = CONV_C_W - 1
        ec = [bufc_ref[:, i, :] for i in range(nb)] + [uc[t * sq:(t + 1) * sq, :] for t in range(nt)]
        planes = []
        for t in range(nt):
            acc = wc_ref[0:1, :] * ec[t]
            for j in range(1, CONV_C_W):
                acc = acc + wc_ref[j:j + 1, :] * ec[t + j]
            planes.append(acc)
        conv_c = jnp.concatenate(planes, axis=0)
        o_ref[:, 1024:1536] = _branch_c_post(conv_c, pc_ref[:, 1024:1536], cb_ref[...], lng_ref[...],
                                             lnb_ref[...]).astype(BF16)
        for i in range(nb):
            bufc_o[:, i, :] = ec[i + nt]

    cmask, smask = _tile_masks(True)
    rowseq = lax.broadcasted_iota(jnp.int32, (ROWS, ROWS), 0) & (sq - 1)
    laneseq = lax.broadcasted_iota(jnp.int32, (ROWS, ROWS), 1) & (sq - 1)

    c2 = c2_ref[...]
    s2 = s2_ref[...]
    rq = _rotary(rq_ref[...], c2, s2)
    rk = _rotary(rk_ref[...], c2, s2) * (HEAD_D ** -0.5)
    rqb = rq.astype(BF16)
    rkb = rk.astype(BF16)
    rvb = rv_ref[...].astype(BF16)
    rkd_t = (rk * kdec_ref[...]).T
    cdec = cdec_ref[0, 0:1, :]

    pre = _ml_prelims(pif_ref[:, 0:128], pif_ref[:, 128:256], bi_ref[...], bf_ref[...], cmask, smask)
    m_prev = mrows_ref[...]
    gt = _ml_gates(pre, 0, m_prev[:, 0:1], cmask, smask)
    mq = mq_ref[...]
    mk = mk_ref[...] * (HEAD_D ** -0.5)
    mqb = mq.astype(BF16)
    mkb = mk.astype(BF16)
    mvb = mv_ref[...].astype(BF16)
    kw = mk * gt["ws"]
    kw_t = kw.T
    dec_full = jnp.broadcast_to(gt["dec"], (ROWS, ROWS))
    dec_scr[...] = dec_full

    def per_seq(b, carry):
        inter_r, inter_m = carry
        sb = sret_ref[b, 0]
        inter_r = jnp.where(rowseq == b, _bdot(rqb, sb.astype(BF16)), inter_r)
        k_sel = jnp.where(laneseq == b, rkd_t, 0.0).astype(BF16)
        sret_o[b, 0] = sb * cdec + _bdot(k_sel, rvb)
        cb_ = cml_ref[b, 0]
        inter_m = jnp.where(rowseq == b, _bdot(mqb, cb_.astype(BF16)), inter_m)
        kw_sel = jnp.where(laneseq == b, kw_t, 0.0).astype(BF16)
        cml_o[b, 0] = cb_ * dec_scr[pl.ds(b, 1), :] + _bdot(kw_sel, mvb)
        return inter_r, inter_m

    zero = jnp.zeros((ROWS, HEAD_D), F32)
    inter_r, inter_m = lax.fori_loop(0, sq, per_seq, (zero, zero))

    sc = _bdot_nt(rqb, rkb) * decay_ref[0]
    ret = _bdot(sc.astype(BF16), rvb) + inter_r * qdec_ref[...]
    o_r = (_silu(rz_ref[...]) * (_head_norm(ret) * rgn_ref[...])).astype(BF16)

    n_old = nrows_ref[...]
    qn = jnp.sum(mq * n_old, axis=1, keepdims=True)
    hout = _ml_output(gt, mqb, mkb, mvb, inter_m, qn)
    nrows_o[...] = n_old * dec_full + _bdot(pre["bones"], kw.astype(BF16))
    mrows_o[...] = jnp.broadcast_to(gt["m_new"], (ROWS, ROWS))
    hm = jax.nn.sigmoid(mo_ref[...]) * hout
    o_m = (_silu(mz_ref[...]) * (_head_norm(hm) * mgn_ref[...])).astype(BF16)

    for hh in range(HEADS):
        @pl.when(h == hh)
        def _(hh=hh):
            o_ref[:, 512 + hh * HEAD_D:512 + (hh + 1) * HEAD_D] = o_r
            o_ref[:, 1536 + hh * HEAD_D:1536 + (hh + 1) * HEAD_D] = o_m


def _mix_sample(p, consts, prm, st, nseq, layer, depth, stacked):
    ntile = nseq // SAMPLE_SEQS
    sq = SAMPLE_SEQS
    head_cols = lambda name: pl.BlockSpec(
        (ROWS, HEAD_D), lambda i, h, _o=_COL[name] // HEAD_D: (i, _o + h))
    const2 = lambda i, h: (0, 0)
    per_head2 = lambda i, h: (0, h)
    in_specs = [
        pl.BlockSpec((ROWS, 2048), lambda i, h: (i, 0)),
        pl.BlockSpec((ROWS, 2048), lambda i, h: (i, 2)),
    ] + [head_cols(n) for n in ("r_q", "r_k", "r_v", "r_z", "m_q", "m_k", "m_v", "m_o", "m_z")] + [
        pl.BlockSpec((ROWS, 256), lambda i, h: (i, N_MAIN // 256 + h)),
        pl.BlockSpec((ROWS, HEAD_D), const2),
        pl.BlockSpec((ROWS, HEAD_D), const2),
        pl.BlockSpec((1, ROWS, ROWS), lambda i, h: (h, 0, 0)),
        pl.BlockSpec((ROWS, HEAD_D), per_head2),
        pl.BlockSpec((ROWS, HEAD_D), per_head2),
        pl.BlockSpec((1, 8, 128), lambda i, h: (h, 0, 0)),
        pl.BlockSpec((CONV_A_W, BRANCH_W), const2),
        pl.BlockSpec((CONV_C_W, BRANCH_W), const2),
        pl.BlockSpec((1, BRANCH_W), const2),
        pl.BlockSpec((1, BRANCH_W), const2),
        pl.BlockSpec((1, BRANCH_W), const2),
        pl.BlockSpec((1, HEAD_D), per_head2),
        pl.BlockSpec((1, HEAD_D), per_head2),
        pl.BlockSpec((1, 128), per_head2),
        pl.BlockSpec((1, 128), per_head2),
        pl.BlockSpec((sq, CONV_A_W - 1, BRANCH_W), lambda i, h: (i, 0, 0)),
        pl.BlockSpec((sq, 1, HEAD_D, HEAD_D), lambda i, h: (i, h, 0, 0)),
        pl.BlockSpec((sq, CONV_C_W - 1, BRANCH_W), lambda i, h: (i, 0, 0)),
        pl.BlockSpec((sq, 1, HEAD_D, HEAD_D), lambda i, h: (i, h, 0, 0)),
        pl.BlockSpec((ROWS, HEAD_D), lambda i, h: (i, h)),
        pl.BlockSpec((ROWS, HEAD_D), lambda i, h: (i, h)),
    ]
    out_specs = [
        pl.BlockSpec((ROWS, N_BRANCH * BRANCH_W), lambda i, h: (i, 0)),
        pl.BlockSpec((sq, CONV_A_W - 1, BRANCH_W), lambda i, h: (i, 0, 0)),
        pl.BlockSpec((None, sq, 1, HEAD_D, HEAD_D), lambda i, h: (layer, i, h, 0, 0)),
        pl.BlockSpec((sq, CONV_C_W - 1, BRANCH_W), lambda i, h: (i, 0, 0)),
        pl.BlockSpec((None, sq, 1, HEAD_D, HEAD_D), lambda i, h: (layer, i, h, 0, 0)),
        pl.BlockSpec((ROWS, HEAD_D), lambda i, h: (i, h)),
        pl.BlockSpec((ROWS, HEAD_D), lambda i, h: (i, h)),
    ]
    nrow = ntile * ROWS
    out_shape = [
        jax.ShapeDtypeStruct((nrow, N_BRANCH * BRANCH_W), BF16),
        jax.ShapeDtypeStruct((nseq, CONV_A_W - 1, BRANCH_W), F32),
        jax.ShapeDtypeStruct((depth, nseq, HEADS, HEAD_D, HEAD_D), F32),
        jax.ShapeDtypeStruct((nseq, CONV_C_W - 1, BRANCH_W), F32),
        jax.ShapeDtypeStruct((depth, nseq, HEADS, HEAD_D, HEAD_D), F32),
        jax.ShapeDtypeStruct((nrow, HEADS * HEAD_D), F32),
        jax.ShapeDtypeStruct((nrow, HEADS * HEAD_D), F32),
    ]
    args = [p] * 12 + [
        consts["c2"], consts["s2"], consts["decay"], consts["qdec"], consts["kdec"], consts["cdec"],
        prm["wa"], prm["wc"], prm["cb"], prm["lng"], prm["lnb"], prm["rgn"], prm["mgn"],
        prm["bi_h"], prm["bf_h"],
        st["bufa"], st["sret"], st["bufc"], st["cml"], st["nrows"], st["mrows"]]
    assert len(args) == len(in_specs)
    aliases = {}
    if stacked is not None:
        aliases = {len(args): 2, len(args) + 1: 4}
        in_specs = in_specs + [pl.BlockSpec(memory_space=pl.ANY)] * 2
        args = args + list(stacked)

    def entry(*refs):
        n_in = len(args) - len(aliases)
        _mix_sample_kernel(*refs[:n_in], *refs[n_in + len(aliases):])

    return pl.pallas_call(
        entry,
        grid=(ntile, HEADS),
        in_specs=in_specs, out_specs=out_specs, out_shape=out_shape,
        scratch_shapes=[pltpu.VMEM((ROWS, ROWS), F32)],
        input_output_aliases=aliases,
        compiler_params=pltpu.CompilerParams(
            dimension_semantics=("arbitrary", "arbitrary"), vmem_limit_bytes=VMEM_LIMIT),
        name="mix_sample",
    )(*args)


def _rope_tables(pos):
    inv = ROPE_BASE ** (-jnp.arange(0, HEAD_D, 2, dtype=F32) / HEAD_D)
    ang = pos.astype(F32)[:, None] * inv[None, :]
    cos = jnp.cos(ang)
    sin = jnp.sin(ang)
    return jnp.concatenate([cos, cos], axis=1), jnp.concatenate([-sin, sin], axis=1)


def _decay_tables(t_of_row, seq_of_row, chunk_len):
    log_g = np.log1p(-(2.0 ** (-5.0 - np.arange(HEADS, dtype=np.float64))))
    dt = t_of_row[:, None] - t_of_row[None, :]
    ok = (seq_of_row[:, None] == seq_of_row[None, :]) & (dt >= 0)
    decay = np.where(ok[None], np.exp(np.maximum(dt, 0)[None] * log_g[:, None, None]), 0.0)
    qdec = np.exp((t_of_row[:, None] + 1.0) * log_g[None, :])
    kdec = np.exp((chunk_len - 1.0 - t_of_row)[:, None] * log_g[None, :])
    cdec = np.exp(chunk_len * log_g)
    rep = lambda a: jnp.asarray(np.repeat(a, HEAD_D, axis=1), F32)
    return jnp.asarray(decay, F32), rep(qdec), rep(kdec), cdec


def kernel(x_prompt, x_sample, state_conv_a, state_ret, state_conv_c, state_mlstm_C,
           state_mlstm_n, state_mlstm_m, norm_g, w_in, conv_a_w, conv_c_w, conv_c_b,
           ln_c_g, ln_c_b, ret_gn_g, ml_gn_g, ml_gate_b, w_br, w_out, final_g):
    bp, tp, _ = x_prompt.shape
    bs, ts, _ = x_sample.shape
    depth = w_in.shape[0]
    past_len = 16384
    sq = SAMPLE_SEQS
    ntile = bs // sq
    assert tp % ROWS == 0 and ts * sq == ROWS and bs % sq == 0

    r = np.arange(ROWS)
    c2p, s2p = _rope_tables(jnp.arange(tp, dtype=jnp.int32))
    decay_p, qdec_p, kdec_p, cdec_p = _decay_tables(r.astype(np.float64), np.zeros(ROWS), float(ROWS))
    consts_p = dict(c2=c2p, s2=s2p, decay=decay_p, qdec=qdec_p, kdec=kdec_p,
                    cdec=tuple(float(v) for v in cdec_p))
    t_s = (r // sq).astype(np.float64)
    c2s, s2s = _rope_tables(past_len + jnp.asarray(r // sq, jnp.int32))
    decay_s, qdec_s, kdec_s, cdec_s = _decay_tables(t_s, r % sq, float(ts))
    cdec_s_arr = jnp.asarray(np.broadcast_to(cdec_s[:, None, None], (HEADS, 8, 128)), F32)
    consts_s = dict(c2=c2s, s2=s2s, decay=decay_s, qdec=qdec_s, kdec=kdec_s, cdec=cdec_s_arr)

    xp = x_prompt.reshape(bp * tp, D_MODEL)
    xs = x_sample.reshape(ntile, sq, ts, D_MODEL).transpose(0, 2, 1, 3).reshape(bs * ts, D_MODEL)

    fg = final_g.reshape(1, D_MODEL)
    outs_p = [[] for _ in range(6)]
    outs_s = [[] for _ in range(6)]
    stacked = None
    sel_p = np.zeros((2 * HEADS, 256), np.float32)
    sel_s = np.zeros((2 * HEADS, HEADS * 256), np.float32)
    for h in range(HEADS):
        sel_p[h, h] = 1.0
        sel_p[HEADS + h, 128 + h] = 1.0
        sel_s[h, h * 256] = 1.0
        sel_s[HEADS + h, h * 256 + 128] = 1.0
    w_bf = w_in.astype(BF16)
    for l in range(depth):
        w_if = w_in[l, :, N_MAIN:N_MAIN + 2 * HEADS]
        spread = lambda sel: jnp.dot(w_if, jnp.asarray(sel), precision=lax.Precision.HIGHEST).astype(BF16)
        w_main_p = jnp.concatenate([w_bf[l, :, :N_MAIN], spread(sel_p)], axis=1)
        w_main_s = jnp.concatenate([w_bf[l, :, :N_MAIN], spread(sel_s)], axis=1)
        w_gate = w_bf[l, :, N_MAIN + 2 * HEADS:]
        gb = ml_gate_b[l]
        pad_row = lambda v: jnp.pad(v, (0, 128 - v.shape[0])).reshape(1, 128)
        prm = dict(
            wa=conv_a_w[l], wc=conv_c_w[l], cb=conv_c_b[l].reshape(1, -1),
            lng=ln_c_g[l].reshape(1, -1), lnb=ln_c_b[l].reshape(1, -1),
            rgn=ret_gn_g[l].reshape(1, -1), mgn=ml_gn_g[l].reshape(1, -1),
            bi=pad_row(gb[:HEADS]), bf=pad_row(gb[HEADS:]),
            bi_h=jnp.concatenate([pad_row(gb[h:h + 1]) for h in range(HEADS)], axis=1),
            bf_h=jnp.concatenate([pad_row(gb[HEADS + h:HEADS + h + 1]) for h in range(HEADS)], axis=1),
        )
        gnorm = norm_g[l].reshape(1, D_MODEL)
        wbr = w_br[l].astype(BF16)
        wout = w_out[l].astype(BF16)
        final = l == depth - 1

        pp = _inproj(xp, gnorm, w_main_p, tm=1024, tn=1408)
        res = _mix_prompt(pp, consts_p, prm, bp, tp)
        o_p = res[0]
        outs_p[0].append(res[1])
        outs_p[1].append(res[2])
        outs_p[2].append(res[3])
        outs_p[3].append(res[4])
        outs_p[4].append(res[5])
        outs_p[5].append(res[6][:, 0, :HEADS])
        xp = _merge(xp, gnorm, o_p, w_gate, wbr, wout, fg, final, tm=512)

        ps = _inproj(xs, gnorm, w_main_s, tm=bs * ts, tn=1536)
        n_rows = jnp.broadcast_to(
            state_mlstm_n[l].reshape(ntile, 1, sq, HEADS * HEAD_D),
            (ntile, ts, sq, HEADS * HEAD_D)).reshape(bs * ts, HEADS * HEAD_D)
        m_rows = jnp.broadcast_to(
            state_mlstm_m[l].reshape(ntile, 1, sq, HEADS, 1),
            (ntile, ts, sq, HEADS, HEAD_D)).reshape(bs * ts, HEADS * HEAD_D)
        st = dict(bufa=state_conv_a[l], sret=state_ret[l], bufc=state_conv_c[l],
                  cml=state_mlstm_C[l], nrows=n_rows, mrows=m_rows)
        res = _mix_sample(ps, consts_s, prm, st, bs, l, depth, stacked)
        o_s = res[0]
        stacked = (res[2], res[4])
        outs_s[0].append(res[1])
        outs_s[2].append(res[3])
        outs_s[4].append(res[5].reshape(ntile, ts, sq, HEADS, HEAD_D)[:, 0].reshape(bs, HEADS, HEAD_D))
        outs_s[5].append(res[6].reshape(ntile, ts, sq, HEADS, HEAD_D)[:, 0, :, :, 0].reshape(bs, HEADS))
        xs = _merge(xs, gnorm, o_s, w_gate, wbr, wout, fg, final, tm=bs * ts)

    y_prompt = xp.reshape(bp, tp, D_MODEL)
    y_sample = xs.reshape(ntile, ts, sq, D_MODEL).transpose(0, 2, 1, 3).reshape(bs, ts, D_MODEL)
    sp = [jnp.stack(a, axis=0) for a in outs_p]
    ss = [jnp.stack(a, axis=0) if a else None for a in outs_s]
    ss[1], ss[3] = stacked
    return (y_prompt, y_sample, sp[0], ss[0], sp[1], ss[1], sp[2], ss[2],
            sp[3], ss[3], sp[4], ss[4], sp[5], ss[5])
```

```python
import functools

import numpy as np
import jax
import jax.numpy as jnp
from jax import lax
from jax.experimental import pallas as pl
from jax.experimental.pallas import tpu as pltpu

F32 = jnp.float32
BF16 = jnp.bfloat16

D_MODEL = 1024
BRANCH_W = 512
N_BRANCH = 4
HEADS = 4
HEAD_D = 128
CONV_A_W = 3
CONV_C_W = 31
ROPE_BASE = 10000.0
EPS = 1e-6
N_MAIN = 16 * BRANCH_W
N_IF = 256
N_GATE = N_BRANCH * D_MODEL
ROWS = 128
TILE = 2 * ROWS
PROJ_COLS = 512
SAMPLE_SEQS = 32
SUBLANES = 8
VMEM_LIMIT = 56 * 1024 * 1024

_COL = dict(a_b=0, a_c=512, a_u=1024, a_z=1536, r_q=2048, r_k=2560, r_v=3072, r_z=3584,
            glu_v=4096, glu_g=4608, c_z=5120, m_q=5632, m_k=6144, m_v=6656, m_o=7168, m_z=7680)


def _bdot(a, b):
    return jnp.dot(a, b, preferred_element_type=F32)


def _bdot_nt(a, b):
    return lax.dot_general(a, b, (((1,), (1,)), ((), ())), preferred_element_type=F32)


def _exact_mm(mat_b16, x):
    hi = x.astype(BF16)
    r1 = x - hi.astype(F32)
    mid = r1.astype(BF16)
    lo = (r1 - mid.astype(F32)).astype(BF16)
    return _bdot(mat_b16, hi) + _bdot(mat_b16, mid) + _bdot(mat_b16, lo)


def _silu(x):
    return x * jax.nn.sigmoid(x)


def _log_sigmoid(x):
    return jnp.minimum(x, 0.0) - jnp.log1p(jnp.exp(-jnp.abs(x)))


def _head_norm(y):
    mu = jnp.mean(y, axis=-1, keepdims=True)
    yc = y - mu
    var = jnp.mean(yc * yc, axis=-1, keepdims=True)
    return yc * lax.rsqrt(var + EPS)


def _rotary(x, c2, s2):
    return x * c2 + pltpu.roll(x, HEAD_D // 2, 1) * s2


def _rmsnorm(x, g):
    ms = jnp.mean(x * x, axis=-1, keepdims=True)
    return (x * lax.rsqrt(ms + EPS)) * g


def _resident(shape, imap):
    return pl.BlockSpec(shape, imap, pipeline_mode=pl.Buffered(1))


def _inproj_kernel(x_ref, g_ref, w_ref, wif_ref, p_ref, pif_ref, hb_ref):
    @pl.when(pl.program_id(1) == 0)
    def _():
        hb = _rmsnorm(x_ref[...], g_ref[...]).astype(BF16)
        hb_ref[...] = hb
        pif_ref[...] = _bdot(hb, wif_ref[...])

    p_ref[...] = _bdot(hb_ref[...], w_ref[...])


def _inproj(x, g, w_all, wif, layer, tm, tn):
    m = x.shape[0]
    nif = wif.shape[1]
    return pl.pallas_call(
        _inproj_kernel,
        grid=(m // tm, N_MAIN // tn),
        in_specs=[
            pl.BlockSpec((tm, D_MODEL), lambda i, j: (i, 0)),
            pl.BlockSpec((1, D_MODEL), lambda i, j: (0, 0)),
            pl.BlockSpec((None, D_MODEL, tn), lambda i, j: (layer, 0, j)),
            pl.BlockSpec((D_MODEL, nif), lambda i, j: (0, 0)),
        ],
        out_specs=[
            pl.BlockSpec((tm, tn), lambda i, j: (i, j)),
            pl.BlockSpec((tm, nif), lambda i, j: (i, 0)),
        ],
        out_shape=[jax.ShapeDtypeStruct((m, N_MAIN), F32), jax.ShapeDtypeStruct((m, nif), F32)],
        scratch_shapes=[pltpu.VMEM((tm, D_MODEL), BF16)],
        compiler_params=pltpu.CompilerParams(
            dimension_semantics=("arbitrary", "arbitrary"), vmem_limit_bytes=VMEM_LIMIT),
        name="inproj",
    )(x, g, w_all, wif)


def _merge_kernel(x_ref, g_ref, o_ref, wg_ref, wbr_ref, wout_ref, fg_ref, y_ref, hb_ref, *, final):
    x = x_ref[...]
    hb_ref[...] = _rmsnorm(x, g_ref[...]).astype(BF16)
    acc = None
    for n in range(N_BRANCH):
        proj = _bdot(o_ref[:, n * BRANCH_W:(n + 1) * BRANCH_W], wbr_ref[n])
        gate = jax.nn.sigmoid(_bdot(hb_ref[...], wg_ref[:, n * D_MODEL:(n + 1) * D_MODEL]))
        term = gate * proj
        acc = term if acc is None else acc + term
    y = x + _bdot(acc.astype(BF16), wout_ref[...])
    if final:
        y = _rmsnorm(y, fg_ref[...])
    y_ref[...] = y


def _merge(x, g, o, wg, wbr, wout, fg, final, tm):
    m = x.shape[0]
    resident = lambda shape: _resident(shape, lambda i: (0,) * len(shape))
    return pl.pallas_call(
        functools.partial(_merge_kernel, final=final),
        grid=(m // tm,),
        in_specs=[
            pl.BlockSpec((tm, D_MODEL), lambda i: (i, 0)),
            resident((1, D_MODEL)),
            pl.BlockSpec((tm, N_BRANCH * BRANCH_W), lambda i: (i, 0)),
            resident((D_MODEL, N_GATE)),
            resident((N_BRANCH, BRANCH_W, D_MODEL)),
            resident((D_MODEL, D_MODEL)),
            resident((1, D_MODEL)),
        ],
        out_specs=pl.BlockSpec((tm, D_MODEL), lambda i: (i, 0)),
        out_shape=jax.ShapeDtypeStruct((m, D_MODEL), F32),
        scratch_shapes=[pltpu.VMEM((tm, D_MODEL), BF16)],
        compiler_params=pltpu.CompilerParams(
            dimension_semantics=("arbitrary",), vmem_limit_bytes=VMEM_LIMIT),
        name="merge",
    )(x, g, o, wg, wbr, wout, fg)


def _tile_masks(sample):
    row = lax.broadcasted_iota(jnp.int32, (ROWS, ROWS), 0)
    col = lax.broadcasted_iota(jnp.int32, (ROWS, ROWS), 1)
    if not sample:
        return row >= col, None
    same = (row & (SAMPLE_SEQS - 1)) == (col & (SAMPLE_SEQS - 1))
    return jnp.logical_and(same, row >= col), same


def _ml_prelims(pi, pf, bi, bfo, cmask, smask):
    ai = pi + bi
    lf = _log_sigmoid(pf + bfo)
    tri = jnp.where(cmask, 1.0, 0.0).astype(BF16)
    if smask is None:
        bones = jnp.ones((ROWS, ROWS), BF16)
    else:
        bones = jnp.where(smask, 1.0, 0.0).astype(BF16)
    fcum = _exact_mm(tri, lf)
    ftot = _exact_mm(bones, lf)
    g = (ftot - fcum) + ai
    return dict(ai_t=ai.T, fcum=fcum, fcum_t=fcum.T, ftot=ftot, g=g, g_t=g.T, bones=bones)


def _ml_gates(pre, col, m_prev_c, cmask, smask):
    f_c = pre["fcum"][:, col:col + 1]
    f_r = pre["fcum_t"][col:col + 1, :]
    ig_r = pre["ai_t"][col:col + 1, :]
    g_c = pre["g"][:, col:col + 1]
    g_r = pre["g_t"][col:col + 1, :]
    ftot_c = pre["ftot"][:, col:col + 1]
    logw = jnp.where(cmask, (f_c - f_r) + ig_r, -jnp.inf)
    b = f_c + m_prev_c
    m_t = jnp.maximum(jnp.max(logw, axis=1, keepdims=True), b)
    wts = jnp.exp(logw - m_t)
    inter = jnp.exp(b - m_t)
    if smask is None:
        gmax = jnp.max(g_r, axis=1, keepdims=True)
    else:
        gmax = jnp.max(jnp.where(smask, g_r, -jnp.inf), axis=1, keepdims=True)
    m_new = jnp.maximum(ftot_c + m_prev_c, gmax)
    ws = jnp.exp(g_c - m_new)
    dec = jnp.exp((ftot_c + m_prev_c) - m_new)
    return dict(m_t=m_t, wts=wts, inter=inter, m_new=m_new, ws=ws, dec=dec)


def _ml_output(gt, qb, kb, vb, inter_term, qn):
    s = _bdot_nt(qb, kb) * gt["wts"]
    num = _bdot(s.astype(BF16), vb) + inter_term * gt["inter"]
    den = jnp.sum(s, axis=1, keepdims=True) + qn * gt["inter"]
    return num / jnp.maximum(jnp.abs(den), jnp.exp(-gt["m_t"]))


def _branch_c_post(conv, c_z, cb, lng, lnb):
    y = conv + cb
    mu = jnp.mean(y, axis=-1, keepdims=True)
    yc = y - mu
    var = jnp.mean(yc * yc, axis=-1, keepdims=True)
    ln = (yc * lax.rsqrt(var + EPS)) * lng + lnb
    return _silu(c_z) * _silu(ln)


def _proj_tile(x_ref, g_ref, w_ref, wif_ref, hb_ref, p_ref):
    hb_ref[...] = _rmsnorm(x_ref[...], g_ref[...]).astype(BF16)
    for j in range(N_MAIN // PROJ_COLS):
        cols = slice(j * PROJ_COLS, (j + 1) * PROJ_COLS)
        p_ref[:, cols] = _bdot(hb_ref[...], w_ref[:, cols])
    p_ref[:, N_MAIN:N_MAIN + N_IF] = _bdot(hb_ref[...], wif_ref[...])


def _mix_chunk(p_ref, r0, pos0, o_ref, or0, tab, prm, scr, cdec):
    c2_ref, s2_ref, decay_ref, qdec_ref, kdec_ref = tab
    wa_ref, wc_ref, cb_ref, lng_ref, lnb_ref, rgn_ref, mgn_ref, bi_ref, bf_ref = prm
    exta, extc, shc, s_scr, c_scr, n_scr, m_scr = scr
    rows = slice(r0, r0 + ROWS)
    orows = slice(or0, or0 + ROWS)
    pcol = lambda name, lo=0, w=BRANCH_W: p_ref[rows, _COL[name] + lo:_COL[name] + lo + w]

    ua = pcol("a_c") * pcol("a_u")
    exta[SUBLANES:SUBLANES + ROWS, :] = ua
    conv_a = (wa_ref[0:1, :] * exta[SUBLANES - 2:SUBLANES - 2 + ROWS, :]
              + wa_ref[1:2, :] * exta[SUBLANES - 1:SUBLANES - 1 + ROWS, :] + wa_ref[2:3, :] * ua)
    o_ref[orows, 0:512] = (_silu(pcol("a_z")) * (pcol("a_b") * conv_a)).astype(BF16)
    exta[0:SUBLANES, :] = exta[ROWS:ROWS + SUBLANES, :]

    extc[32:32 + ROWS, :] = pcol("glu_v") * jax.nn.sigmoid(pcol("glu_g"))
    span = ROWS + 32 - SUBLANES
    for ph in range(1, SUBLANES):
        shc[ph - 1] = extc[ph:ph + span, :]
    conv_blocks = []
    for lb in range(BRANCH_W // 128):
        lanes = slice(lb * 128, (lb + 1) * 128)
        acc = None
        for ph in range(SUBLANES):
            for a in range(span // SUBLANES - ROWS // SUBLANES + 1):
                j = a * SUBLANES + ph - 2
                if 0 <= j < CONV_C_W:
                    arows = slice(a * SUBLANES, a * SUBLANES + ROWS)
                    shifted = extc[arows, lanes] if ph == 0 else shc[ph - 1, arows, lanes]
                    term = wc_ref[j:j + 1, lanes] * shifted
                    acc = term if acc is None else acc + term
        conv_blocks.append(acc + wc_ref[CONV_C_W - 1:CONV_C_W, lanes] * extc[32:32 + ROWS, lanes])
    conv_c = jnp.concatenate(conv_blocks, axis=1)
    o_ref[orows, 1024:1536] = _branch_c_post(conv_c, pcol("c_z"), cb_ref[...], lng_ref[...],
                                             lnb_ref[...]).astype(BF16)
    extc[0:32, :] = extc[ROWS:ROWS + 32, :]

    c2 = c2_ref[pl.ds(pos0, ROWS), :]
    s2 = s2_ref[pl.ds(pos0, ROWS), :]
    for h in range(HEADS):
        lo = h * HEAD_D
        hs = slice(lo, lo + HEAD_D)
        q = _rotary(pcol("r_q", lo, HEAD_D), c2, s2)
        k = _rotary(pcol("r_k", lo, HEAD_D), c2, s2) * (HEAD_D ** -0.5)
        qb = q.astype(BF16)
        kb = k.astype(BF16)
        vb = pcol("r_v", lo, HEAD_D).astype(BF16)
        s_old = s_scr[h]
        sc = _bdot_nt(qb, kb) * decay_ref[h]
        ret = _bdot(sc.astype(BF16), vb) + _bdot(qb, s_old.astype(BF16)) * qdec_ref[:, hs]
        s_scr[h] = s_old * cdec[h] + _bdot((k * kdec_ref[:, hs]).T.astype(BF16), vb)
        o_ref[orows, 512 + lo:512 + lo + HEAD_D] = (
            _silu(pcol("r_z", lo, HEAD_D)) * (_head_norm(ret) * rgn_ref[:, hs])).astype(BF16)

    cmask, smask = _tile_masks(False)
    pre = _ml_prelims(p_ref[rows, N_MAIN:N_MAIN + 128], p_ref[rows, N_MAIN + 128:N_MAIN + 256],
                      bi_ref[...], bf_ref[...], cmask, smask)
    lane = lax.broadcasted_iota(jnp.int32, (ROWS, ROWS), 1)
    m_rows = m_scr[...]
    m_rows_new = jnp.zeros((ROWS, ROWS), F32)
    for h in range(HEADS):
        lo = h * HEAD_D
        hs = slice(lo, lo + HEAD_D)
        gt = _ml_gates(pre, h, m_rows[:, h:h + 1], cmask, smask)
        q = pcol("m_q", lo, HEAD_D)
        k = pcol("m_k", lo, HEAD_D) * (HEAD_D ** -0.5)
        qb = q.astype(BF16)
        kb = k.astype(BF16)
        vb = pcol("m_v", lo, HEAD_D).astype(BF16)
        c_old = c_scr[h]
        n_old = n_scr[:, hs]
        qn = jnp.sum(q * n_old, axis=1, keepdims=True)
        hout = _ml_output(gt, qb, kb, vb, _bdot(qb, c_old.astype(BF16)), qn)
        kw = k * gt["ws"]
        dec_full = jnp.broadcast_to(gt["dec"], (ROWS, ROWS))
        c_scr[h] = c_old * dec_full + _bdot(kw.T.astype(BF16), vb)
        n_scr[:, hs] = n_old * dec_full + _bdot(pre["bones"], kw.astype(BF16))
        m_rows_new = jnp.where(lane == h, jnp.broadcast_to(gt["m_new"], (ROWS, ROWS)), m_rows_new)
        hm = jax.nn.sigmoid(pcol("m_o", lo, HEAD_D)) * hout
        o_ref[orows, 1536 + lo:1536 + lo + HEAD_D] = (
            _silu(pcol("m_z", lo, HEAD_D)) * (_head_norm(hm) * mgn_ref[:, hs])).astype(BF16)
    m_scr[...] = m_rows_new


def _xmix_kernel(x0_ref, xb_ref, xa_ref, g_ref, w_ref, wif_ref,
                 c2_ref, s2_ref, decay_ref, qdec_ref, kdec_ref,
                 wa_ref, wc_ref, cb_ref, lng_ref, lnb_ref, rgn_ref, mgn_ref, bi_ref, bf_ref,
                 o_ref, bufa_ref, sret_ref, bufc_ref, cml_ref, nml_ref, mml_ref,
                 pa_scr, pb_scr, hba_scr, hbb_scr, exta, extc, shc, s_scr, c_scr, n_scr, m_scr,
                 *, cdec, steps_per_seq):
    s = pl.program_id(0)
    seq_step = s % steps_per_seq
    tab = (c2_ref, s2_ref, decay_ref, qdec_ref, kdec_ref)
    prm = (wa_ref, wc_ref, cb_ref, lng_ref, lnb_ref, rgn_ref, mgn_ref, bi_ref, bf_ref)
    scr = (exta, extc, shc, s_scr, c_scr, n_scr, m_scr)

    @pl.when(s == 0)
    def _():
        _proj_tile(x0_ref, g_ref, w_ref, wif_ref, hba_scr, pa_scr)

    @pl.when(seq_step == 0)
    def _():
        exta[0:SUBLANES, :] = jnp.zeros((SUBLANES, BRANCH_W), F32)
        extc[0:32, :] = jnp.zeros((32, BRANCH_W), F32)
        s_scr[...] = jnp.zeros_like(s_scr)
        c_scr[...] = jnp.zeros_like(c_scr)
        n_scr[...] = jnp.zeros_like(n_scr)
        m_scr[...] = jnp.zeros_like(m_scr)

    pos_base = seq_step * (2 * TILE)
    _proj_tile(xb_ref, g_ref, w_ref, wif_ref, hbb_scr, pb_scr)
    for ci in range(TILE // ROWS):
        _mix_chunk(pa_scr, ci * ROWS, pl.multiple_of(pos_base + ci * ROWS, ROWS),
                   o_ref, ci * ROWS, tab, prm, scr, cdec)
    _proj_tile(xa_ref, g_ref, w_ref, wif_ref, hba_scr, pa_scr)
    for ci in range(TILE // ROWS):
        _mix_chunk(pb_scr, ci * ROWS, pl.multiple_of(pos_base + TILE + ci * ROWS, ROWS),
                   o_ref, TILE + ci * ROWS, tab, prm, scr, cdec)

    @pl.when(seq_step == steps_per_seq - 1)
    def _():
        bufa_ref[0] = exta[SUBLANES - (CONV_A_W - 1):SUBLANES, :]
        bufc_ref[0] = extc[32 - (CONV_C_W - 1):32, :]
        for h in range(HEADS):
            sret_ref[0, h] = s_scr[h]
            cml_ref[0, h] = c_scr[h]
            nml_ref[0, h:h + 1, :] = n_scr[0:1, h * HEAD_D:(h + 1) * HEAD_D]
        mml_ref[0] = m_scr[0:1, :]


def _xmix_prompt(x, gnorm, w_all, wif, layer, consts, prm, batch, seq):
    steps_per_seq = seq // (2 * TILE)
    nstep = batch * steps_per_seq
    ntile = 2 * nstep
    c1 = lambda shape: _resident(shape, lambda s: (0,) * len(shape))
    in_specs = [
        pl.BlockSpec((TILE, D_MODEL), lambda s: (0, 0)),
        pl.BlockSpec((TILE, D_MODEL), lambda s: (2 * s + 1, 0)),
        pl.BlockSpec((TILE, D_MODEL), lambda s: (jnp.minimum(2 * s + 2, ntile - 1), 0)),
        c1((1, D_MODEL)),
        _resident((None, D_MODEL, N_MAIN), lambda s: (layer, 0, 0)),
        c1((D_MODEL, N_IF)),
        c1((seq, HEAD_D)),
        c1((seq, HEAD_D)),
        c1((HEADS, ROWS, ROWS)),
        c1((ROWS, BRANCH_W)),
        c1((ROWS, BRANCH_W)),
        c1((CONV_A_W, BRANCH_W)),
        c1((CONV_C_W, BRANCH_W)),
    ] + [c1((1, BRANCH_W))] * 5 + [c1((1, 128))] * 2
    per_b3 = lambda s: (s // steps_per_seq, 0, 0)
    per_b4 = lambda s: (s // steps_per_seq, 0, 0, 0)
    out_specs = [
        pl.BlockSpec((2 * TILE, N_BRANCH * BRANCH_W), lambda s: (s, 0)),
        pl.BlockSpec((1, CONV_A_W - 1, BRANCH_W), per_b3),
        pl.BlockSpec((1, HEADS, HEAD_D, HEAD_D), per_b4),
        pl.BlockSpec((1, CONV_C_W - 1, BRANCH_W), per_b3),
        pl.BlockSpec((1, HEADS, HEAD_D, HEAD_D), per_b4),
        pl.BlockSpec((1, HEADS, HEAD_D), per_b3),
        pl.BlockSpec((1, 1, 128), per_b3),
    ]
    out_shape = [
        jax.ShapeDtypeStruct((batch * seq, N_BRANCH * BRANCH_W), BF16),
        jax.ShapeDtypeStruct((batch, CONV_A_W - 1, BRANCH_W), F32),
        jax.ShapeDtypeStruct((batch, HEADS, HEAD_D, HEAD_D), F32),
        jax.ShapeDtypeStruct((batch, CONV_C_W - 1, BRANCH_W), F32),
        jax.ShapeDtypeStruct((batch, HEADS, HEAD_D, HEAD_D), F32),
        jax.ShapeDtypeStruct((batch, HEADS, HEAD_D), F32),
        jax.ShapeDtypeStruct((batch, 1, 128), F32),
    ]
    scratch = [
        pltpu.VMEM((TILE, N_MAIN + N_IF), F32),
        pltpu.VMEM((TILE, N_MAIN + N_IF), F32),
        pltpu.VMEM((TILE, D_MODEL), BF16),
        pltpu.VMEM((TILE, D_MODEL), BF16),
        pltpu.VMEM((ROWS + SUBLANES, BRANCH_W), F32),
        pltpu.VMEM((ROWS + 32, BRANCH_W), F32),
        pltpu.VMEM((SUBLANES - 1, ROWS + 32 - SUBLANES, BRANCH_W), F32),
        pltpu.VMEM((HEADS, HEAD_D, HEAD_D), F32),
        pltpu.VMEM((HEADS, HEAD_D, HEAD_D), F32),
        pltpu.VMEM((ROWS, BRANCH_W), F32),
        pltpu.VMEM((ROWS, ROWS), F32),
    ]
    return pl.pallas_call(
        functools.partial(_xmix_kernel, cdec=consts["cdec"], steps_per_seq=steps_per_seq),
        grid=(nstep,),
        in_specs=in_specs, out_specs=out_specs, out_shape=out_shape, scratch_shapes=scratch,
        compiler_params=pltpu.CompilerParams(
            dimension_semantics=("arbitrary",), vmem_limit_bytes=VMEM_LIMIT),
        name="xmix_prompt",
    )(x, x, x, gnorm, w_all, wif, consts["c2"], consts["s2"], consts["decay"], consts["qdec"],
      consts["kdec"], prm["wa"], prm["wc"], prm["cb"], prm["lng"], prm["lnb"], prm["rgn"],
      prm["mgn"], prm["bi"], prm["bf"])


def _mix_sample_kernel(pa_ref, pc_ref, rq_ref, rk_ref, rv_ref, rz_ref, mq_ref, mk_ref, mv_ref,
                       mo_ref, mz_ref, pif_ref, c2_ref, s2_ref, decay_ref, qdec_ref, kdec_ref,
                       cdec_ref, wa_ref, wc_ref, cb_ref, lng_ref, lnb_ref, rgn_ref, mgn_ref,
                       bi_ref, bf_ref, bufa_ref, sret_ref, bufc_ref, cml_ref, nrows_ref, mrows_ref,
                       o_ref, bufa_o, sret_o, bufc_o, cml_o, nrows_o, mrows_o, dec_scr):
    h = pl.program_id(1)
    sq = SAMPLE_SEQS
    nt = ROWS // sq

    @pl.when(h == 0)
    def _():
        ua = pa_ref[:, 512:1024] * pa_ref[:, 1024:1536]
        ea = [bufa_ref[:, 0, :], bufa_ref[:, 1, :]] + [ua[t * sq:(t + 1) * sq, :] for t in range(nt)]
        conv_a = jnp.concatenate(
            [wa_ref[0:1, :] * ea[t] + wa_ref[1:2, :] * ea[t + 1] + wa_ref[2:3, :] * ea[t + 2]
             for t in range(nt)], axis=0)
        o_ref[:, 0:512] = (_silu(pa_ref[:, 1536:2048]) * (pa_ref[:, 0:512] * conv_a)).astype(BF16)
        bufa_o[:, 0, :] = ea[nt]
        bufa_o[:, 1, :] = ea[nt + 1]
        uc = pc_ref[:, 0:512] * jax.nn.sigmoid(pc_ref[:, 512:1024])
        nb = CONV_C_W - 1
        ec = [bufc_ref[:, i, :] for i in range(nb)] + [uc[t * sq:(t + 1) * sq, :] for t in range(nt)]
        planes = []
        for t in range(nt):
            acc = wc_ref[0:1, :] * ec[t]
            for j in range(1, CONV_C_W):
                acc = acc + wc_ref[j:j + 1, :] * ec[t + j]
            planes.append(acc)
        conv_c = jnp.concatenate(planes, axis=0)
        o_ref[:, 1024:1536] = _branch_c_post(conv_c, pc_ref[:, 1024:1536], cb_ref[...], lng_ref[...],
                                             lnb_ref[...]).astype(BF16)
        for i in range(nb):
            bufc_o[:, i, :] = ec[i + nt]

    cmask, smask = _tile_masks(True)
    rowseq = lax.broadcasted_iota(jnp.int32, (ROWS, ROWS), 0) & (sq - 1)
    laneseq = lax.broadcasted_iota(jnp.int32, (ROWS, ROWS), 1) & (sq - 1)

    c2 = c2_ref[...]
    s2 = s2_ref[...]
    rq = _rotary(rq_ref[...], c2, s2)
    rk = _rotary(rk_ref[...], c2, s2) * (HEAD_D ** -0.5)
    rqb = rq.astype(BF16)
    rkb = rk.astype(BF16)
    rvb = rv_ref[...].astype(BF16)
    rkd_t = (rk * kdec_ref[...]).T
    cdec = cdec_ref[0, 0:1, :]

    pre = _ml_prelims(pif_ref[:, 0:128], pif_ref[:, 128:256], bi_ref[...], bf_ref[...], cmask, smask)
    m_prev = mrows_ref[...]
    gt = _ml_gates(pre, 0, m_prev[:, 0:1], cmask, smask)
    mq = mq_ref[...]
    mk = mk_ref[...] * (HEAD_D ** -0.5)
    mqb = mq.astype(BF16)
    mkb = mk.astype(BF16)
    mvb = mv_ref[...].astype(BF16)
    kw = mk * gt["ws"]
    kw_t = kw.T
    dec_full = jnp.broadcast_to(gt["dec"], (ROWS, ROWS))
    dec_scr[...] = dec_full

    def per_seq(b, carry):
        inter_r, inter_m = carry
        sb = sret_ref[b, 0]
        inter_r = jnp.where(rowseq == b, _bdot(rqb, sb.astype(BF16)), inter_r)
        k_sel = jnp.where(laneseq == b, rkd_t, 0.0).astype(BF16)
        sret_o[b, 0] = sb * cdec + _bdot(k_sel, rvb)
        cb_ = cml_ref[b, 0]
        inter_m = jnp.where(rowseq == b, _bdot(mqb, cb_.astype(BF16)), inter_m)
        kw_sel = jnp.where(laneseq == b, kw_t, 0.0).astype(BF16)
        cml_o[b, 0] = cb_ * dec_scr[pl.ds(b, 1), :] + _bdot(kw_sel, mvb)
        return inter_r, inter_m

    zero = jnp.zeros((ROWS, HEAD_D), F32)
    inter_r, inter_m = lax.fori_loop(0, sq, per_seq, (zero, zero), unroll=4)

    sc = _bdot_nt(rqb, rkb) * decay_ref[0]
    ret = _bdot(sc.astype(BF16), rvb) + inter_r * qdec_ref[...]
    o_r = (_silu(rz_ref[...]) * (_head_norm(ret) * rgn_ref[...])).astype(BF16)

    n_old = nrows_ref[...]
    qn = jnp.sum(mq * n_old, axis=1, keepdims=True)
    hout = _ml_output(gt, mqb, mkb, mvb, inter_m, qn)
    nrows_o[...] = n_old * dec_full + _bdot(pre["bones"], kw.astype(BF16))
    mrows_o[...] = jnp.broadcast_to(gt["m_new"], (ROWS, ROWS))
    hm = jax.nn.sigmoid(mo_ref[...]) * hout
    o_m = (_silu(mz_ref[...]) * (_head_norm(hm) * mgn_ref[...])).astype(BF16)

    for hh in range(HEADS):
        @pl.when(h == hh)
        def _(hh=hh):
            o_ref[:, 512 + hh * HEAD_D:512 + (hh + 1) * HEAD_D] = o_r
            o_ref[:, 1536 + hh * HEAD_D:1536 + (hh + 1) * HEAD_D] = o_m


def _mix_sample(p, pif, consts, prm, st, nseq, layer, depth, stacked):
    ntile = nseq // SAMPLE_SEQS
    sq = SAMPLE_SEQS
    head_cols = lambda name: pl.BlockSpec(
        (ROWS, HEAD_D), lambda i, h, _o=_COL[name] // HEAD_D: (i, _o + h))
    const2 = lambda i, h: (0, 0)
    per_head2 = lambda i, h: (0, h)
    conv_a_state = pl.BlockSpec((None, sq, CONV_A_W - 1, BRANCH_W), lambda i, h: (layer, i, 0, 0))
    conv_c_state = pl.BlockSpec((None, sq, CONV_C_W - 1, BRANCH_W), lambda i, h: (layer, i, 0, 0))
    matrix_state = pl.BlockSpec((None, sq, 1, HEAD_D, HEAD_D), lambda i, h: (layer, i, h, 0, 0))
    in_specs = [
        pl.BlockSpec((ROWS, 2048), lambda i, h: (i, 0)),
        pl.BlockSpec((ROWS, 2048), lambda i, h: (i, 2)),
    ] + [head_cols(n) for n in ("r_q", "r_k", "r_v", "r_z", "m_q", "m_k", "m_v", "m_o", "m_z")] + [
        pl.BlockSpec((ROWS, 256), lambda i, h: (i, h)),
        pl.BlockSpec((ROWS, HEAD_D), const2),
        pl.BlockSpec((ROWS, HEAD_D), const2),
        pl.BlockSpec((1, ROWS, ROWS), lambda i, h: (h, 0, 0)),
        pl.BlockSpec((ROWS, HEAD_D), per_head2),
        pl.BlockSpec((ROWS, HEAD_D), per_head2),
        pl.BlockSpec((1, 8, 128), lambda i, h: (h, 0, 0)),
        pl.BlockSpec((CONV_A_W, BRANCH_W), const2),
        pl.BlockSpec((CONV_C_W, BRANCH_W), const2),
        pl.BlockSpec((1, BRANCH_W), const2),
        pl.BlockSpec((1, BRANCH_W), const2),
        pl.BlockSpec((1, BRANCH_W), const2),
        pl.BlockSpec((1, HEAD_D), per_head2),
        pl.BlockSpec((1, HEAD_D), per_head2),
        pl.BlockSpec((1, 128), per_head2),
        pl.BlockSpec((1, 128), per_head2),
        conv_a_state, matrix_state, conv_c_state, matrix_state,
        pl.BlockSpec((ROWS, HEAD_D), lambda i, h: (i, h)),
        pl.BlockSpec((ROWS, HEAD_D), lambda i, h: (i, h)),
    ]
    out_specs = [
        pl.BlockSpec((ROWS, N_BRANCH * BRANCH_W), lambda i, h: (i, 0)),
        conv_a_state, matrix_state, conv_c_state, matrix_state,
        pl.BlockSpec((ROWS, HEAD_D), lambda i, h: (i, h)),
        pl.BlockSpec((ROWS, HEAD_D), lambda i, h: (i, h)),
    ]
    nrow = ntile * ROWS
    out_shape = [
        jax.ShapeDtypeStruct((nrow, N_BRANCH * BRANCH_W), BF16),
        jax.ShapeDtypeStruct((depth, nseq, CONV_A_W - 1, BRANCH_W), F32),
        jax.ShapeDtypeStruct((depth, nseq, HEADS, HEAD_D, HEAD_D), F32),
        jax.ShapeDtypeStruct((depth, nseq, CONV_C_W - 1, BRANCH_W), F32),
        jax.ShapeDtypeStruct((depth, nseq, HEADS, HEAD_D, HEAD_D), F32),
        jax.ShapeDtypeStruct((nrow, HEADS * HEAD_D), F32),
        jax.ShapeDtypeStruct((nrow, HEADS * HEAD_D), F32),
    ]
    args = [p] * 11 + [
        pif, consts["c2"], consts["s2"], consts["decay"], consts["qdec"], consts["kdec"],
        consts["cdec"], prm["wa"], prm["wc"], prm["cb"], prm["lng"], prm["lnb"], prm["rgn"],
        prm["mgn"], prm["bi_h"], prm["bf_h"],
        st["bufa"], st["sret"], st["bufc"], st["cml"], st["nrows"], st["mrows"]]
    assert len(args) == len(in_specs)
    n_in = len(args)
    aliases = {}
    if stacked is not None:
        aliases = {n_in + k: 1 + k for k in range(4)}
        in_specs = in_specs + [pl.BlockSpec(memory_space=pl.ANY)] * 4
        args = args + list(stacked)

    def entry(*refs):
        _mix_sample_kernel(*refs[:n_in], *refs[n_in + len(aliases):])

    return pl.pallas_call(
        entry,
        grid=(ntile, HEADS),
        in_specs=in_specs, out_specs=out_specs, out_shape=out_shape,
        scratch_shapes=[pltpu.VMEM((ROWS, ROWS), F32)],
        input_output_aliases=aliases,
        compiler_params=pltpu.CompilerParams(
            dimension_semantics=("arbitrary", "arbitrary"), vmem_limit_bytes=VMEM_LIMIT),
        name="mix_sample",
    )(*args)


def _rope_tables(pos):
    inv = ROPE_BASE ** (-jnp.arange(0, HEAD_D, 2, dtype=F32) / HEAD_D)
    ang = pos.astype(F32)[:, None] * inv[None, :]
    cos = jnp.cos(ang)
    sin = jnp.sin(ang)
    return jnp.concatenate([cos, cos], axis=1), jnp.concatenate([-sin, sin], axis=1)


def _decay_tables(t_of_row, seq_of_row, chunk_len):
    log_g = np.log1p(-(2.0 ** (-5.0 - np.arange(HEADS, dtype=np.float64))))
    dt = t_of_row[:, None] - t_of_row[None, :]
    ok = (seq_of_row[:, None] == seq_of_row[None, :]) & (dt >= 0)
    decay = np.where(ok[None], np.exp(np.maximum(dt, 0)[None] * log_g[:, None, None]), 0.0)
    qdec = np.exp((t_of_row[:, None] + 1.0) * log_g[None, :])
    kdec = np.exp((chunk_len - 1.0 - t_of_row)[:, None] * log_g[None, :])
    cdec = np.exp(chunk_len * log_g)
    rep = lambda a: jnp.asarray(np.repeat(a, HEAD_D, axis=1), F32)
    return jnp.asarray(decay, F32), rep(qdec), rep(kdec), cdec


def kernel(x_prompt, x_sample, state_conv_a, state_ret, state_conv_c, state_mlstm_C,
           state_mlstm_n, state_mlstm_m, norm_g, w_in, conv_a_w, conv_c_w, conv_c_b,
           ln_c_g, ln_c_b, ret_gn_g, ml_gn_g, ml_gate_b, w_br, w_out, final_g):
    bp, tp, _ = x_prompt.shape
    bs, ts, _ = x_sample.shape
    depth = w_in.shape[0]
    past_len = 16384
    sq = SAMPLE_SEQS
    ntile = bs // sq
    assert tp % (2 * TILE) == 0 and ts * sq == ROWS and bs % sq == 0

    r = np.arange(ROWS)
    c2p, s2p = _rope_tables(jnp.arange(tp, dtype=jnp.int32))
    decay_p, qdec_p, kdec_p, cdec_p = _decay_tables(r.astype(np.float64), np.zeros(ROWS), float(ROWS))
    consts_p = dict(c2=c2p, s2=s2p, decay=decay_p, qdec=qdec_p, kdec=kdec_p,
                    cdec=tuple(float(v) for v in cdec_p))
    t_s = (r // sq).astype(np.float64)
    c2s, s2s = _rope_tables(past_len + jnp.asarray(r // sq, jnp.int32))
    decay_s, qdec_s, kdec_s, cdec_s = _decay_tables(t_s, r % sq, float(ts))
    cdec_s_arr = jnp.asarray(np.broadcast_to(cdec_s[:, None, None], (HEADS, 8, 128)), F32)
    consts_s = dict(c2=c2s, s2=s2s, decay=decay_s, qdec=qdec_s, kdec=kdec_s, cdec=cdec_s_arr)

    xp = x_prompt.reshape(bp * tp, D_MODEL)
    xs = x_sample.reshape(ntile, sq, ts, D_MODEL).transpose(0, 2, 1, 3).reshape(bs * ts, D_MODEL)

    fg = final_g.reshape(1, D_MODEL)
    outs_p = [[] for _ in range(6)]
    n_s, m_s = [], []
    stacked = None
    sel_p = np.zeros((2 * HEADS, N_IF), np.float32)
    sel_s = np.zeros((2 * HEADS, HEADS * N_IF), np.float32)
    for h in range(HEADS):
        sel_p[h, h] = 1.0
        sel_p[HEADS + h, 128 + h] = 1.0
        sel_s[h, h * N_IF] = 1.0
        sel_s[HEADS + h, h * N_IF + 128] = 1.0
    w_bf = w_in.astype(BF16)
    pad_row = lambda v: jnp.pad(v, (0, 128 - v.shape[0])).reshape(1, 128)
    for l in range(depth):
        w_if = w_in[l, :, N_MAIN:N_MAIN + 2 * HEADS]
        spread = lambda sel: jnp.dot(w_if, jnp.asarray(sel), precision=lax.Precision.HIGHEST).astype(BF16)
        w_gate = w_bf[l, :, N_MAIN + 2 * HEADS:]
        gb = ml_gate_b[l]
        prm = dict(
            wa=conv_a_w[l], wc=conv_c_w[l], cb=conv_c_b[l].reshape(1, -1),
            lng=ln_c_g[l].reshape(1, -1), lnb=ln_c_b[l].reshape(1, -1),
            rgn=ret_gn_g[l].reshape(1, -1), mgn=ml_gn_g[l].reshape(1, -1),
            bi=pad_row(gb[:HEADS]), bf=pad_row(gb[HEADS:]),
            bi_h=jnp.concatenate([pad_row(gb[h:h + 1]) for h in range(HEADS)], axis=1),
            bf_h=jnp.concatenate([pad_row(gb[HEADS + h:HEADS + h + 1]) for h in range(HEADS)], axis=1),
        )
        gnorm = norm_g[l].reshape(1, D_MODEL)
        wbr = w_br[l].astype(BF16)
        wout = w_out[l].astype(BF16)
        final = l == depth - 1

        res = _xmix_prompt(xp, gnorm, w_bf, spread(sel_p), l, consts_p, prm, bp, tp)
        for k in range(5):
            outs_p[k].append(res[1 + k])
        outs_p[5].append(res[6][:, 0, :HEADS])
        xp = _merge(xp, gnorm, res[0], w_gate, wbr, wout, fg, final, tm=512)

        ps, pifs = _inproj(xs, gnorm, w_bf, spread(sel_s), l, tm=bs * ts, tn=2048)
        n_rows = jnp.broadcast_to(
            state_mlstm_n[l].reshape(ntile, 1, sq, HEADS * HEAD_D),
            (ntile, ts, sq, HEADS * HEAD_D)).reshape(bs * ts, HEADS * HEAD_D)
        m_rows = jnp.broadcast_to(
            state_mlstm_m[l].reshape(ntile, 1, sq, HEADS, 1),
            (ntile, ts, sq, HEADS, HEAD_D)).reshape(bs * ts, HEADS * HEAD_D)
        st = dict(bufa=state_conv_a, sret=state_ret, bufc=state_conv_c, cml=state_mlstm_C,
                  nrows=n_rows, mrows=m_rows)
        res = _mix_sample(ps, pifs, consts_s, prm, st, bs, l, depth, stacked)
        stacked = tuple(res[1:5])
        n_s.append(res[5].reshape(ntile, ts, sq, HEADS, HEAD_D)[:, 0].reshape(bs, HEADS, HEAD_D))
        m_s.append(res[6].reshape(ntile, ts, sq, HEADS, HEAD_D)[:, 0, :, :, 0].reshape(bs, HEADS))
        xs = _merge(xs, gnorm, res[0], w_gate, wbr, wout, fg, final, tm=bs * ts)

    y_prompt = xp.reshape(bp, tp, D_MODEL)
    y_sample = xs.reshape(ntile, ts, sq, D_MODEL).transpose(0, 2, 1, 3).reshape(bs, ts, D_MODEL)
    sp = [jnp.stack(a, axis=0) for a in outs_p]
    return (y_prompt, y_sample, sp[0], stacked[0], sp[1], stacked[1], sp[2], stacked[2],
            sp[3], stacked[3], sp[4], jnp.stack(n_s, axis=0), sp[5], jnp.stack(m_s, axis=0))
```

```python
import functools

import numpy as np
import jax
import jax.numpy as jnp
from jax import lax
from jax.experimental import pallas as pl
from jax.experimental.pallas import tpu as pltpu

F32 = jnp.float32
BF16 = jnp.bfloat16

D_MODEL = 1024
BRANCH_W = 512
N_BRANCH = 4
HEADS = 4
HEAD_D = 128
CONV_A_W = 3
CONV_C_W = 31
ROPE_BASE = 10000.0
EPS = 1e-6
N_MAIN = 16 * BRANCH_W
N_IF = 256
N_GATE = N_BRANCH * D_MODEL
ROWS = 128
STEP_ROWS = 4 * ROWS
PROJ_COLS = 512
W_BLOCK = N_MAIN + 512
SAMPLE_SEQS = 32
SUBLANES = 8
VMEM_LIMIT = 56 * 1024 * 1024

_COL = dict(a_b=0, a_c=512, a_u=1024, a_z=1536, r_q=2048, r_k=2560, r_v=3072, r_z=3584,
            glu_v=4096, glu_g=4608, c_z=5120, m_q=5632, m_k=6144, m_v=6656, m_o=7168, m_z=7680)


def _bdot(a, b):
    return jnp.dot(a, b, preferred_element_type=F32)


def _bdot_nt(a, b):
    return lax.dot_general(a, b, (((1,), (1,)), ((), ())), preferred_element_type=F32)


def _exact_mm(mat_b16, x):
    hi = x.astype(BF16)
    r1 = x - hi.astype(F32)
    mid = r1.astype(BF16)
    lo = (r1 - mid.astype(F32)).astype(BF16)
    return _bdot(mat_b16, hi) + _bdot(mat_b16, mid) + _bdot(mat_b16, lo)


def _silu(x):
    return x * jax.nn.sigmoid(x)


def _log_sigmoid(x):
    return jnp.minimum(x, 0.0) - jnp.log1p(jnp.exp(-jnp.abs(x)))


def _head_norm(y):
    mu = jnp.mean(y, axis=-1, keepdims=True)
    yc = y - mu
    var = jnp.mean(yc * yc, axis=-1, keepdims=True)
    return yc * lax.rsqrt(var + EPS)


def _rotary(x, c2, s2):
    return x * c2 + pltpu.roll(x, HEAD_D // 2, 1) * s2


def _rmsnorm(x, g):
    ms = jnp.mean(x * x, axis=-1, keepdims=True)
    return (x * lax.rsqrt(ms + EPS)) * g


def _resident(shape, imap):
    return pl.BlockSpec(shape, imap, pipeline_mode=pl.Buffered(1))


def _inproj_kernel(x_ref, g_ref, w_ref, wif_ref, p_ref, pif_ref, hb_ref):
    @pl.when(pl.program_id(1) == 0)
    def _():
        hb = _rmsnorm(x_ref[...], g_ref[...]).astype(BF16)
        hb_ref[...] = hb
        pif_ref[...] = _bdot(hb, wif_ref[...])

    p_ref[...] = _bdot(hb_ref[...], w_ref[...])


def _inproj(x, g, w_all, wif, layer, tm, tn):
    m = x.shape[0]
    nif = wif.shape[1]
    return pl.pallas_call(
        _inproj_kernel,
        grid=(m // tm, N_MAIN // tn),
        in_specs=[
            pl.BlockSpec((tm, D_MODEL), lambda i, j: (i, 0)),
            pl.BlockSpec((1, D_MODEL), lambda i, j: (0, 0)),
            pl.BlockSpec((None, D_MODEL, tn), lambda i, j: (layer, 0, j)),
            pl.BlockSpec((D_MODEL, nif), lambda i, j: (0, 0)),
        ],
        out_specs=[
            pl.BlockSpec((tm, tn), lambda i, j: (i, j)),
            pl.BlockSpec((tm, nif), lambda i, j: (i, 0)),
        ],
        out_shape=[jax.ShapeDtypeStruct((m, N_MAIN), F32), jax.ShapeDtypeStruct((m, nif), F32)],
        scratch_shapes=[pltpu.VMEM((tm, D_MODEL), BF16)],
        compiler_params=pltpu.CompilerParams(
            dimension_semantics=("arbitrary", "arbitrary"), vmem_limit_bytes=VMEM_LIMIT),
        name="inproj",
    )(x, g, w_all, wif)


def _merge_kernel(x_ref, g_ref, o_ref, wg_ref, wbr_ref, wout_ref, fg_ref, y_ref, hb_ref, *, final):
    x = x_ref[...]
    hb_ref[...] = _rmsnorm(x, g_ref[...]).astype(BF16)
    acc = None
    for n in range(N_BRANCH):
        proj = _bdot(o_ref[:, n * BRANCH_W:(n + 1) * BRANCH_W], wbr_ref[n])
        gate = jax.nn.sigmoid(_bdot(hb_ref[...], wg_ref[:, n * D_MODEL:(n + 1) * D_MODEL]))
        term = gate * proj
        acc = term if acc is None else acc + term
    y = x + _bdot(acc.astype(BF16), wout_ref[...])
    if final:
        y = _rmsnorm(y, fg_ref[...])
    y_ref[...] = y


def _merge(x, g, o, wg, wbr, wout, fg, final, tm):
    m = x.shape[0]
    resident = lambda shape: _resident(shape, lambda i: (0,) * len(shape))
    return pl.pallas_call(
        functools.partial(_merge_kernel, final=final),
        grid=(m // tm,),
        in_specs=[
            pl.BlockSpec((tm, D_MODEL), lambda i: (i, 0)),
            resident((1, D_MODEL)),
            pl.BlockSpec((tm, N_BRANCH * BRANCH_W), lambda i: (i, 0)),
            resident((D_MODEL, N_GATE)),
            resident((N_BRANCH, BRANCH_W, D_MODEL)),
            resident((D_MODEL, D_MODEL)),
            resident((1, D_MODEL)),
        ],
        out_specs=pl.BlockSpec((tm, D_MODEL), lambda i: (i, 0)),
        out_shape=jax.ShapeDtypeStruct((m, D_MODEL), F32),
        scratch_shapes=[pltpu.VMEM((tm, D_MODEL), BF16)],
        compiler_params=pltpu.CompilerParams(
            dimension_semantics=("arbitrary",), vmem_limit_bytes=VMEM_LIMIT),
        name="merge",
    )(x, g, o, wg, wbr, wout, fg)


def _tile_masks(sample):
    row = lax.broadcasted_iota(jnp.int32, (ROWS, ROWS), 0)
    col = lax.broadcasted_iota(jnp.int32, (ROWS, ROWS), 1)
    if not sample:
        return row >= col, None
    same = (row & (SAMPLE_SEQS - 1)) == (col & (SAMPLE_SEQS - 1))
    return jnp.logical_and(same, row >= col), same


def _ml_prelims(pi, pf, bi, bfo, cmask, smask):
    ai = pi + bi
    lf = _log_sigmoid(pf + bfo)
    tri = jnp.where(cmask, 1.0, 0.0).astype(BF16)
    if smask is None:
        bones = jnp.ones((ROWS, ROWS), BF16)
    else:
        bones = jnp.where(smask, 1.0, 0.0).astype(BF16)
    fcum = _exact_mm(tri, lf)
    ftot = _exact_mm(bones, lf)
    g = (ftot - fcum) + ai
    return dict(ai_t=ai.T, fcum=fcum, fcum_t=fcum.T, ftot=ftot, g=g, g_t=g.T, bones=bones)


def _ml_gates(pre, col, m_prev_c, cmask, smask):
    f_c = pre["fcum"][:, col:col + 1]
    f_r = pre["fcum_t"][col:col + 1, :]
    ig_r = pre["ai_t"][col:col + 1, :]
    g_c = pre["g"][:, col:col + 1]
    g_r = pre["g_t"][col:col + 1, :]
    ftot_c = pre["ftot"][:, col:col + 1]
    logw = jnp.where(cmask, (f_c - f_r) + ig_r, -jnp.inf)
    b = f_c + m_prev_c
    m_t = jnp.maximum(jnp.max(logw, axis=1, keepdims=True), b)
    wts = jnp.exp(logw - m_t)
    inter = jnp.exp(b - m_t)
    if smask is None:
        gmax = jnp.max(g_r, axis=1, keepdims=True)
    else:
        gmax = jnp.max(jnp.where(smask, g_r, -jnp.inf), axis=1, keepdims=True)
    m_new = jnp.maximum(ftot_c + m_prev_c, gmax)
    ws = jnp.exp(g_c - m_new)
    dec = jnp.exp((ftot_c + m_prev_c) - m_new)
    return dict(m_t=m_t, wts=wts, inter=inter, m_new=m_new, ws=ws, dec=dec)


def _ml_output(gt, qb, kb, vb, inter_term, qn):
    s = _bdot_nt(qb, kb) * gt["wts"]
    num = _bdot(s.astype(BF16), vb) + inter_term * gt["inter"]
    den = jnp.sum(s, axis=1, keepdims=True) + qn * gt["inter"]
    return num / jnp.maximum(jnp.abs(den), jnp.exp(-gt["m_t"]))


def _branch_c_post(conv, c_z, cb, lng, lnb):
    y = conv + cb
    mu = jnp.mean(y, axis=-1, keepdims=True)
    yc = y - mu
    var = jnp.mean(yc * yc, axis=-1, keepdims=True)
    ln = (yc * lax.rsqrt(var + EPS)) * lng + lnb
    return _silu(c_z) * _silu(ln)


MIX_STAGES = 15
_FREED_AFTER_STAGE = {1: (0, 4), 6: (8, 11), 10: (4, 8), 15: (11, 16)}
_GATES_FREED_AFTER_STAGE = 15


def _project_rows(x_rows, g_ref, hb_ref):
    hb_ref[...] = _rmsnorm(x_rows, g_ref[...]).astype(BF16)


def _project_cols(hb_ref, w_ref, wif_ref, p_ref, stage):
    if stage in _FREED_AFTER_STAGE:
        for j in range(*_FREED_AFTER_STAGE[stage]):
            cols = slice(j * PROJ_COLS, (j + 1) * PROJ_COLS)
            p_ref[:, cols] = _bdot(hb_ref[...], w_ref[:, cols])
    if stage == _GATES_FREED_AFTER_STAGE:
        p_ref[:, N_MAIN:N_MAIN + N_IF] = _bdot(hb_ref[...], wif_ref[...])


def _mix_chunk(p_ref, or0, o_ref, tab, prm, scr, cdec):
    c2_ref, s2_ref, decay_ref, qdec_ref, kdec_ref = tab
    wa_ref, wc_ref, cb_ref, lng_ref, lnb_ref, rgn_ref, mgn_ref, bi_ref, bf_ref = prm
    exta, extc, shc, s_scr, c_scr, n_scr, m_scr = scr
    rows = slice(0, ROWS)
    orows = slice(or0, or0 + ROWS)
    pcol = lambda name, lo=0, w=BRANCH_W: p_ref[rows, _COL[name] + lo:_COL[name] + lo + w]

    ua = pcol("a_c") * pcol("a_u")
    exta[SUBLANES:SUBLANES + ROWS, :] = ua
    conv_a = (wa_ref[0:1, :] * exta[SUBLANES - 2:SUBLANES - 2 + ROWS, :]
              + wa_ref[1:2, :] * exta[SUBLANES - 1:SUBLANES - 1 + ROWS, :] + wa_ref[2:3, :] * ua)
    o_a = _silu(pcol("a_z")) * (pcol("a_b") * conv_a)
    o_ref[orows, 0:512] = o_a.astype(BF16)
    exta[0:SUBLANES, :] = exta[ROWS:ROWS + SUBLANES, :]
    yield

    extc[32:32 + ROWS, :] = pcol("glu_v") * jax.nn.sigmoid(pcol("glu_g"))
    span = ROWS + 32 - SUBLANES
    conv_blocks = []
    for lb in range(BRANCH_W // 128):
        lanes = slice(lb * 128, (lb + 1) * 128)
        sh = shc.at[lb % 2]
        for ph in range(1, SUBLANES):
            sh[ph - 1] = extc[ph:ph + span, lanes]
        acc = None
        for ph in range(SUBLANES):
            for a in range(span // SUBLANES - ROWS // SUBLANES + 1):
                j = a * SUBLANES + ph - 2
                if 0 <= j < CONV_C_W:
                    arows = slice(a * SUBLANES, a * SUBLANES + ROWS)
                    shifted = extc[arows, lanes] if ph == 0 else sh[ph - 1, arows, :]
                    term = wc_ref[j:j + 1, lanes] * shifted
                    acc = term if acc is None else acc + term
        conv_blocks.append(acc + wc_ref[CONV_C_W - 1:CONV_C_W, lanes] * extc[32:32 + ROWS, lanes])
        yield
    conv_c = jnp.concatenate(conv_blocks, axis=1)
    o_c = _branch_c_post(conv_c, pcol("c_z"), cb_ref[...], lng_ref[...], lnb_ref[...])
    o_ref[orows, 1024:1536] = o_c.astype(BF16)
    extc[0:32, :] = extc[ROWS:ROWS + 32, :]
    yield

    c2 = c2_ref[orows, :]
    s2 = s2_ref[orows, :]
    for h in range(HEADS):
        lo = h * HEAD_D
        hs = slice(lo, lo + HEAD_D)
        q = _rotary(pcol("r_q", lo, HEAD_D), c2, s2)
        k = _rotary(pcol("r_k", lo, HEAD_D), c2, s2) * (HEAD_D ** -0.5)
        qb = q.astype(BF16)
        kb = k.astype(BF16)
        vb = pcol("r_v", lo, HEAD_D).astype(BF16)
        s_old = s_scr[h]
        sc = _bdot_nt(qb, kb) * decay_ref[h]
        ret = _bdot(sc.astype(BF16), vb) + _bdot(qb, s_old.astype(BF16)) * qdec_ref[:, hs]
        s_scr[h] = s_old * cdec[h] + _bdot((k * kdec_ref[:, hs]).T.astype(BF16), vb)
        o_r = _silu(pcol("r_z", lo, HEAD_D)) * (_head_norm(ret) * rgn_ref[:, hs])
        o_ref[orows, 512 + lo:512 + lo + HEAD_D] = o_r.astype(BF16)
        yield

    cmask, smask = _tile_masks(False)
    pre = _ml_prelims(p_ref[rows, N_MAIN:N_MAIN + 128], p_ref[rows, N_MAIN + 128:N_MAIN + 256],
                      bi_ref[...], bf_ref[...], cmask, smask)
    lane = lax.broadcasted_iota(jnp.int32, (ROWS, ROWS), 1)
    m_rows = m_scr[...]
    m_rows_new = jnp.zeros((ROWS, ROWS), F32)
    yield
    for h in range(HEADS):
        lo = h * HEAD_D
        hs = slice(lo, lo + HEAD_D)
        gt = _ml_gates(pre, h, m_rows[:, h:h + 1], cmask, smask)
        q = pcol("m_q", lo, HEAD_D)
        k = pcol("m_k", lo, HEAD_D) * (HEAD_D ** -0.5)
        qb = q.astype(BF16)
        kb = k.astype(BF16)
        vb = pcol("m_v", lo, HEAD_D).astype(BF16)
        c_old = c_scr[h]
        n_old = n_scr[:, hs]
        qn = jnp.sum(q * n_old, axis=1, keepdims=True)
        hout = _ml_output(gt, qb, kb, vb, _bdot(qb, c_old.astype(BF16)), qn)
        kw = k * gt["ws"]
        dec_full = jnp.broadcast_to(gt["dec"], (ROWS, ROWS))
        c_scr[h] = c_old * dec_full + _bdot(kw.T.astype(BF16), vb)
        n_scr[:, hs] = n_old * dec_full + _bdot(pre["bones"], kw.astype(BF16))
        m_rows_new = jnp.where(lane == h, jnp.broadcast_to(gt["m_new"], (ROWS, ROWS)), m_rows_new)
        hm = jax.nn.sigmoid(pcol("m_o", lo, HEAD_D)) * hout
        o_m = _silu(pcol("m_z", lo, HEAD_D)) * (_head_norm(hm) * mgn_ref[:, hs])
        o_ref[orows, 1536 + lo:1536 + lo + HEAD_D] = o_m.astype(BF16)
        yield
    m_scr[...] = m_rows_new


def _xmix_kernel(x0_ref, xc_ref, xn_ref, g_ref, w_ref, wif_ref,
                 c2_ref, s2_ref, decay_ref, qdec_ref, kdec_ref,
                 wa_ref, wc_ref, cb_ref, lng_ref, lnb_ref, rgn_ref, mgn_ref, bi_ref, bf_ref,
                 o_ref, bufa_ref, sret_ref, bufc_ref, cml_ref, nml_ref, mml_ref,
                 p_scr, hb_scr, exta, extc, shc, s_scr, c_scr, n_scr, m_scr,
                 *, cdec, steps_per_seq):
    s = pl.program_id(0)
    seq_step = s % steps_per_seq
    tab = (c2_ref, s2_ref, decay_ref, qdec_ref, kdec_ref)
    prm = (wa_ref, wc_ref, cb_ref, lng_ref, lnb_ref, rgn_ref, mgn_ref, bi_ref, bf_ref)
    scr = (exta, extc, shc, s_scr, c_scr, n_scr, m_scr)
    nchunk = STEP_ROWS // ROWS

    @pl.when(s == 0)
    def _():
        _project_rows(x0_ref[...], g_ref, hb_scr.at[0])
        for stage in range(1, MIX_STAGES + 1):
            _project_cols(hb_scr.at[0], w_ref, wif_ref, p_scr, stage)

    @pl.when(seq_step == 0)
    def _():
        exta[0:SUBLANES, :] = jnp.zeros((SUBLANES, BRANCH_W), F32)
        extc[0:32, :] = jnp.zeros((32, BRANCH_W), F32)
        s_scr[...] = jnp.zeros_like(s_scr)
        c_scr[...] = jnp.zeros_like(c_scr)
        n_scr[...] = jnp.zeros_like(n_scr)
        m_scr[...] = jnp.zeros_like(m_scr)

    for ci in range(nchunk):
        hb = hb_scr.at[(ci + 1) % 2]
        nxt = xc_ref[(ci + 1) * ROWS:(ci + 2) * ROWS, :] if ci + 1 < nchunk else xn_ref[...]
        _project_rows(nxt, g_ref, hb)
        for stage, _ in enumerate(_mix_chunk(p_scr, ci * ROWS, o_ref, tab, prm, scr, cdec), 1):
            _project_cols(hb, w_ref, wif_ref, p_scr, stage)

    @pl.when(seq_step == steps_per_seq - 1)
    def _():
        bufa_ref[0] = exta[SUBLANES - (CONV_A_W - 1):SUBLANES, :]
        bufc_ref[0] = extc[32 - (CONV_C_W - 1):32, :]
        for h in range(HEADS):
            sret_ref[0, h] = s_scr[h]
            cml_ref[0, h] = c_scr[h]
            nml_ref[0, h:h + 1, :] = n_scr[0:1, h * HEAD_D:(h + 1) * HEAD_D]
        mml_ref[0] = m_scr[0:1, :]


def _xmix_prompt(x, gnorm, w_all, wif, layer, consts, prm, batch, seq):
    steps_per_seq = seq // STEP_ROWS
    nstep = batch * steps_per_seq
    nchunk = STEP_ROWS // ROWS
    c1 = lambda shape: _resident(shape, lambda s: (0,) * len(shape))
    rope = pl.BlockSpec((STEP_ROWS, HEAD_D), lambda s: (s % steps_per_seq, 0))
    in_specs = [
        c1((ROWS, D_MODEL)),
        pl.BlockSpec((STEP_ROWS, D_MODEL), lambda s: (s, 0)),
        pl.BlockSpec((ROWS, D_MODEL), lambda s: (jnp.minimum(s + 1, nstep - 1) * nchunk, 0)),
        c1((1, D_MODEL)),
        _resident((None, D_MODEL, W_BLOCK), lambda s: (layer, 0, 0)),
        c1((D_MODEL, N_IF)),
        rope, rope,
        c1((HEADS, ROWS, ROWS)),
        c1((ROWS, BRANCH_W)),
        c1((ROWS, BRANCH_W)),
        c1((CONV_A_W, BRANCH_W)),
        c1((CONV_C_W, BRANCH_W)),
    ] + [c1((1, BRANCH_W))] * 5 + [c1((1, 128))] * 2
    per_b3 = lambda s: (s // steps_per_seq, 0, 0)
    per_b4 = lambda s: (s // steps_per_seq, 0, 0, 0)
    out_specs = [
        pl.BlockSpec((STEP_ROWS, N_BRANCH * BRANCH_W), lambda s: (s, 0)),
        pl.BlockSpec((1, CONV_A_W - 1, BRANCH_W), per_b3),
        pl.BlockSpec((1, HEADS, HEAD_D, HEAD_D), per_b4),
        pl.BlockSpec((1, CONV_C_W - 1, BRANCH_W), per_b3),
        pl.BlockSpec((1, HEADS, HEAD_D, HEAD_D), per_b4),
        pl.BlockSpec((1, HEADS, HEAD_D), per_b3),
        pl.BlockSpec((1, 1, 128), per_b3),
    ]
    out_shape = [
        jax.ShapeDtypeStruct((batch * seq, N_BRANCH * BRANCH_W), BF16),
        jax.ShapeDtypeStruct((batch, CONV_A_W - 1, BRANCH_W), F32),
        jax.ShapeDtypeStruct((batch, HEADS, HEAD_D, HEAD_D), F32),
        jax.ShapeDtypeStruct((batch, CONV_C_W - 1, BRANCH_W), F32),
        jax.ShapeDtypeStruct((batch, HEADS, HEAD_D, HEAD_D), F32),
        jax.ShapeDtypeStruct((batch, HEADS, HEAD_D), F32),
        jax.ShapeDtypeStruct((batch, 1, 128), F32),
    ]
    scratch = [
        pltpu.VMEM((ROWS, N_MAIN + N_IF), F32),
        pltpu.VMEM((2, ROWS, D_MODEL), BF16),
        pltpu.VMEM((ROWS + SUBLANES, BRANCH_W), F32),
        pltpu.VMEM((ROWS + 32, BRANCH_W), F32),
        pltpu.VMEM((2, SUBLANES - 1, ROWS + 32 - SUBLANES, 128), F32),
        pltpu.VMEM((HEADS, HEAD_D, HEAD_D), F32),
        pltpu.VMEM((HEADS, HEAD_D, HEAD_D), F32),
        pltpu.VMEM((ROWS, BRANCH_W), F32),
        pltpu.VMEM((ROWS, ROWS), F32),
    ]
    return pl.pallas_call(
        functools.partial(_xmix_kernel, cdec=consts["cdec"], steps_per_seq=steps_per_seq),
        grid=(nstep,),
        in_specs=in_specs, out_specs=out_specs, out_shape=out_shape, scratch_shapes=scratch,
        compiler_params=pltpu.CompilerParams(
            dimension_semantics=("arbitrary",), vmem_limit_bytes=VMEM_LIMIT),
        name="xmix_prompt",
    )(x, x, x, gnorm, w_all, wif, consts["c2"], consts["s2"], consts["decay"], consts["qdec"],
      consts["kdec"], prm["wa"], prm["wc"], prm["cb"], prm["lng"], prm["lnb"], prm["rgn"],
      prm["mgn"], prm["bi"], prm["bf"])


def _mix_sample_kernel(pa_ref, pc_ref, rq_ref, rk_ref, rv_ref, rz_ref, mq_ref, mk_ref, mv_ref,
                       mo_ref, mz_ref, pif_ref, c2_ref, s2_ref, decay_ref, qdec_ref, kdec_ref,
                       cdec_ref, wa_ref, wc_ref, cb_ref, lng_ref, lnb_ref, rgn_ref, mgn_ref,
                       bi_ref, bf_ref, bufa_ref, sret_ref, bufc_ref, cml_ref, nrows_ref, mrows_ref,
                       o_ref, bufa_o, sret_o, bufc_o, cml_o, nrows_o, mrows_o, dec_scr):
    h = pl.program_id(1)
    sq = SAMPLE_SEQS
    nt = ROWS // sq

    @pl.when(h == 0)
    def _():
        ua = pa_ref[:, 512:1024] * pa_ref[:, 1024:1536]
        ea = [bufa_ref[:, 0, :], bufa_ref[:, 1, :]] + [ua[t * sq:(t + 1) * sq, :] for t in range(nt)]
        conv_a = jnp.concatenate(
            [wa_ref[0:1, :] * ea[t] + wa_ref[1:2, :] * ea[t + 1] + wa_ref[2:3, :] * ea[t + 2]
             for t in range(nt)], axis=0)
        o_ref[:, 0:512] = (_silu(pa_ref[:, 1536:2048]) * (pa_ref[:, 0:512] * conv_a)).astype(BF16)
        bufa_o[:, 0, :] = ea[nt]
        bufa_o[:, 1, :] = ea[nt + 1]
        uc = pc_ref[:, 0:512] * jax.nn.sigmoid(pc_ref[:, 512:1024])
        nb = CONV_C_W - 1
        ec = [bufc_ref[:, i, :] for i in range(nb)] + [uc[t * sq:(t + 1) * sq, :] for t in range(nt)]
        planes = []
        for t in range(nt):
            acc = wc_ref[0:1, :] * ec[t]
            for j in range(1, CONV_C_W):
                acc = acc + wc_ref[j:j + 1, :] * ec[t + j]
            planes.append(acc)
        conv_c = jnp.concatenate(planes, axis=0)
        o_ref[:, 1024:1536] = _branch_c_post(conv_c, pc_ref[:, 1024:1536], cb_ref[...], lng_ref[...],
                                             lnb_ref[...]).astype(BF16)
        for i in range(nb):
            bufc_o[:, i, :] = ec[i + nt]

    cmask, smask = _tile_masks(True)
    rowseq = lax.broadcasted_iota(jnp.int32, (ROWS, ROWS), 0) & (sq - 1)
    laneseq = lax.broadcasted_iota(jnp.int32, (ROWS, ROWS), 1) & (sq - 1)

    c2 = c2_ref[...]
    s2 = s2_ref[...]
    rq = _rotary(rq_ref[...], c2, s2)
    rk = _rotary(rk_ref[...], c2, s2) * (HEAD_D ** -0.5)
    rqb = rq.astype(BF16)
    rkb = rk.astype(BF16)
    rvb = rv_ref[...].astype(BF16)
    rkd_t = (rk * kdec_ref[...]).T
    cdec = cdec_ref[0, 0:1, :]

    pre = _ml_prelims(pif_ref[:, 0:128], pif_ref[:, 128:256], bi_ref[...], bf_ref[...], cmask, smask)
    m_prev = mrows_ref[...]
    gt = _ml_gates(pre, 0, m_prev[:, 0:1], cmask, smask)
    mq = mq_ref[...]
    mk = mk_ref[...] * (HEAD_D ** -0.5)
    mqb = mq.astype(BF16)
    mkb = mk.astype(BF16)
    mvb = mv_ref[...].astype(BF16)
    kw = mk * gt["ws"]
    kw_t = kw.T
    dec_full = jnp.broadcast_to(gt["dec"], (ROWS, ROWS))
    dec_scr[...] = dec_full

    def per_seq(b, carry):
        inter_r, inter_m = carry
        sb = sret_ref[b, 0]
        inter_r = jnp.where(rowseq == b, _bdot(rqb, sb.astype(BF16)), inter_r)
        k_sel = jnp.where(laneseq == b, rkd_t, 0.0).astype(BF16)
        sret_o[b, 0] = sb * cdec + _bdot(k_sel, rvb)
        cb_ = cml_ref[b, 0]
        inter_m = jnp.where(rowseq == b, _bdot(mqb, cb_.astype(BF16)), inter_m)
        kw_sel = jnp.where(laneseq == b, kw_t, 0.0).astype(BF16)
        cml_o[b, 0] = cb_ * dec_scr[pl.ds(b, 1), :] + _bdot(kw_sel, mvb)
        return inter_r, inter_m

    zero = jnp.zeros((ROWS, HEAD_D), F32)
    inter_r, inter_m = lax.fori_loop(0, sq, per_seq, (zero, zero), unroll=4)

    sc = _bdot_nt(rqb, rkb) * decay_ref[0]
    ret = _bdot(sc.astype(BF16), rvb) + inter_r * qdec_ref[...]
    o_r = (_silu(rz_ref[...]) * (_head_norm(ret) * rgn_ref[...])).astype(BF16)

    n_old = nrows_ref[...]
    qn = jnp.sum(mq * n_old, axis=1, keepdims=True)
    hout = _ml_output(gt, mqb, mkb, mvb, inter_m, qn)
    nrows_o[...] = n_old * dec_full + _bdot(pre["bones"], kw.astype(BF16))
    mrows_o[...] = jnp.broadcast_to(gt["m_new"], (ROWS, ROWS))
    hm = jax.nn.sigmoid(mo_ref[...]) * hout
    o_m = (_silu(mz_ref[...]) * (_head_norm(hm) * mgn_ref[...])).astype(BF16)

    for hh in range(HEADS):
        @pl.when(h == hh)
        def _(hh=hh):
            o_ref[:, 512 + hh * HEAD_D:512 + (hh + 1) * HEAD_D] = o_r
            o_ref[:, 1536 + hh * HEAD_D:1536 + (hh + 1) * HEAD_D] = o_m


def _mix_sample(p, pif, consts, prm, st, nseq, layer, depth, stacked):
    ntile = nseq // SAMPLE_SEQS
    sq = SAMPLE_SEQS
    head_cols = lambda name: pl.BlockSpec(
        (ROWS, HEAD_D), lambda i, h, _o=_COL[name] // HEAD_D: (i, _o + h))
    const2 = lambda i, h: (0, 0)
    per_head2 = lambda i, h: (0, h)
    conv_a_state = pl.BlockSpec((None, sq, CONV_A_W - 1, BRANCH_W), lambda i, h: (layer, i, 0, 0))
    conv_c_state = pl.BlockSpec((None, sq, CONV_C_W - 1, BRANCH_W), lambda i, h: (layer, i, 0, 0))
    matrix_state = pl.BlockSpec((None, sq, 1, HEAD_D, HEAD_D), lambda i, h: (layer, i, h, 0, 0))
    in_specs = [
        pl.BlockSpec((ROWS, 2048), lambda i, h: (i, 0)),
        pl.BlockSpec((ROWS, 2048), lambda i, h: (i, 2)),
    ] + [head_cols(n) for n in ("r_q", "r_k", "r_v", "r_z", "m_q", "m_k", "m_v", "m_o", "m_z")] + [
        pl.BlockSpec((ROWS, 256), lambda i, h: (i, h)),
        pl.BlockSpec((ROWS, HEAD_D), const2),
        pl.BlockSpec((ROWS, HEAD_D), const2),
        pl.BlockSpec((1, ROWS, ROWS), lambda i, h: (h, 0, 0)),
        pl.BlockSpec((ROWS, HEAD_D), per_head2),
        pl.BlockSpec((ROWS, HEAD_D), per_head2),
        pl.BlockSpec((1, 8, 128), lambda i, h: (h, 0, 0)),
        pl.BlockSpec((CONV_A_W, BRANCH_W), const2),
        pl.BlockSpec((CONV_C_W, BRANCH_W), const2),
        pl.BlockSpec((1, BRANCH_W), const2),
        pl.BlockSpec((1, BRANCH_W), const2),
        pl.BlockSpec((1, BRANCH_W), const2),
        pl.BlockSpec((1, HEAD_D), per_head2),
        pl.BlockSpec((1, HEAD_D), per_head2),
        pl.BlockSpec((1, 128), per_head2),
        pl.BlockSpec((1, 128), per_head2),
        conv_a_state, matrix_state, conv_c_state, matrix_state,
        pl.BlockSpec((ROWS, HEAD_D), lambda i, h: (i, h)),
        pl.BlockSpec((ROWS, HEAD_D), lambda i, h: (i, h)),
    ]
    out_specs = [
        pl.BlockSpec((ROWS, N_BRANCH * BRANCH_W), lambda i, h: (i, 0)),
        conv_a_state, matrix_state, conv_c_state, matrix_state,
        pl.BlockSpec((ROWS, HEAD_D), lambda i, h: (i, h)),
        pl.BlockSpec((ROWS, HEAD_D), lambda i, h: (i, h)),
    ]
    nrow = ntile * ROWS
    out_shape = [
        jax.ShapeDtypeStruct((nrow, N_BRANCH * BRANCH_W), BF16),
        jax.ShapeDtypeStruct((depth, nseq, CONV_A_W - 1, BRANCH_W), F32),
        jax.ShapeDtypeStruct((depth, nseq, HEADS, HEAD_D, HEAD_D), F32),
        jax.ShapeDtypeStruct((depth, nseq, CONV_C_W - 1, BRANCH_W), F32),
        jax.ShapeDtypeStruct((depth, nseq, HEADS, HEAD_D, HEAD_D), F32),
        jax.ShapeDtypeStruct((nrow, HEADS * HEAD_D), F32),
        jax.ShapeDtypeStruct((nrow, HEADS * HEAD_D), F32),
    ]
    args = [p] * 11 + [
        pif, consts["c2"], consts["s2"], consts["decay"], consts["qdec"], consts["kdec"],
        consts["cdec"], prm["wa"], prm["wc"], prm["cb"], prm["lng"], prm["lnb"], prm["rgn"],
        prm["mgn"], prm["bi_h"], prm["bf_h"],
        st["bufa"], st["sret"], st["bufc"], st["cml"], st["nrows"], st["mrows"]]
    assert len(args) == len(in_specs)
    n_in = len(args)
    aliases = {}
    if stacked is not None:
        aliases = {n_in + k: 1 + k for k in range(4)}
        in_specs = in_specs + [pl.BlockSpec(memory_space=pl.ANY)] * 4
        args = args + list(stacked)

    def entry(*refs):
        _mix_sample_kernel(*refs[:n_in], *refs[n_in + len(aliases):])

    return pl.pallas_call(
        entry,
        grid=(ntile, HEADS),
        in_specs=in_specs, out_specs=out_specs, out_shape=out_shape,
        scratch_shapes=[pltpu.VMEM((ROWS, ROWS), F32)],
        input_output_aliases=aliases,
        compiler_params=pltpu.CompilerParams(
            dimension_semantics=("arbitrary", "arbitrary"), vmem_limit_bytes=VMEM_LIMIT),
        name="mix_sample",
    )(*args)


def _rope_tables(pos):
    inv = ROPE_BASE ** (-jnp.arange(0, HEAD_D, 2, dtype=F32) / HEAD_D)
    ang = pos.astype(F32)[:, None] * inv[None, :]
    cos = jnp.cos(ang)
    sin = jnp.sin(ang)
    return jnp.concatenate([cos, cos], axis=1), jnp.concatenate([-sin, sin], axis=1)


def _decay_tables(t_of_row, seq_of_row, chunk_len):
    log_g = np.log1p(-(2.0 ** (-5.0 - np.arange(HEADS, dtype=np.float64))))
    dt = t_of_row[:, None] - t_of_row[None, :]
    ok = (seq_of_row[:, None] == seq_of_row[None, :]) & (dt >= 0)
    decay = np.where(ok[None], np.exp(np.maximum(dt, 0)[None] * log_g[:, None, None]), 0.0)
    qdec = np.exp((t_of_row[:, None] + 1.0) * log_g[None, :])
    kdec = np.exp((chunk_len - 1.0 - t_of_row)[:, None] * log_g[None, :])
    cdec = np.exp(chunk_len * log_g)
    rep = lambda a: jnp.asarray(np.repeat(a, HEAD_D, axis=1), F32)
    return jnp.asarray(decay, F32), rep(qdec), rep(kdec), cdec


def kernel(x_prompt, x_sample, state_conv_a, state_ret, state_conv_c, state_mlstm_C,
           state_mlstm_n, state_mlstm_m, norm_g, w_in, conv_a_w, conv_c_w, conv_c_b,
           ln_c_g, ln_c_b, ret_gn_g, ml_gn_g, ml_gate_b, w_br, w_out, final_g):
    bp, tp, _ = x_prompt.shape
    bs, ts, _ = x_sample.shape
    depth = w_in.shape[0]
    past_len = 16384
    sq = SAMPLE_SEQS
    ntile = bs // sq
    assert tp % STEP_ROWS == 0 and ts * sq == ROWS and bs % sq == 0

    r = np.arange(ROWS)
    c2p, s2p = _rope_tables(jnp.arange(tp, dtype=jnp.int32))
    decay_p, qdec_p, kdec_p, cdec_p = _decay_tables(r.astype(np.float64), np.zeros(ROWS), float(ROWS))
    consts_p = dict(c2=c2p, s2=s2p, decay=decay_p, qdec=qdec_p, kdec=kdec_p,
                    cdec=tuple(float(v) for v in cdec_p))
    t_s = (r // sq).astype(np.float64)
    c2s, s2s = _rope_tables(past_len + jnp.asarray(r // sq, jnp.int32))
    decay_s, qdec_s, kdec_s, cdec_s = _decay_tables(t_s, r % sq, float(ts))
    cdec_s_arr = jnp.asarray(np.broadcast_to(cdec_s[:, None, None], (HEADS, 8, 128)), F32)
    consts_s = dict(c2=c2s, s2=s2s, decay=decay_s, qdec=qdec_s, kdec=kdec_s, cdec=cdec_s_arr)

    xp = x_prompt.reshape(bp * tp, D_MODEL)
    xs = x_sample.reshape(ntile, sq, ts, D_MODEL).transpose(0, 2, 1, 3).reshape(bs * ts, D_MODEL)

    fg = final_g.reshape(1, D_MODEL)
    outs_p = [[] for _ in range(6)]
    n_s, m_s = [], []
    stacked = None
    sel_p = np.zeros((2 * HEADS, N_IF), np.float32)
    sel_s = np.zeros((2 * HEADS, HEADS * N_IF), np.float32)
    for h in range(HEADS):
        sel_p[h, h] = 1.0
        sel_p[HEADS + h, 128 + h] = 1.0
        sel_s[h, h * N_IF] = 1.0
        sel_s[HEADS + h, h * N_IF + 128] = 1.0
    w_bf = w_in.astype(BF16)
    pad_row = lambda v: jnp.pad(v, (0, 128 - v.shape[0])).reshape(1, 128)
    for l in range(depth):
        w_if = w_in[l, :, N_MAIN:N_MAIN + 2 * HEADS]
        spread = lambda sel: jnp.dot(w_if, jnp.asarray(sel), precision=lax.Precision.HIGHEST).astype(BF16)
        w_gate = w_bf[l, :, N_MAIN + 2 * HEADS:]
        gb = ml_gate_b[l]
        prm = dict(
            wa=conv_a_w[l], wc=conv_c_w[l], cb=conv_c_b[l].reshape(1, -1),
            lng=ln_c_g[l].reshape(1, -1), lnb=ln_c_b[l].reshape(1, -1),
            rgn=ret_gn_g[l].reshape(1, -1), mgn=ml_gn_g[l].reshape(1, -1),
            bi=pad_row(gb[:HEADS]), bf=pad_row(gb[HEADS:]),
            bi_h=jnp.concatenate([pad_row(gb[h:h + 1]) for h in range(HEADS)], axis=1),
            bf_h=jnp.concatenate([pad_row(gb[HEADS + h:HEADS + h + 1]) for h in range(HEADS)], axis=1),
        )
        gnorm = norm_g[l].reshape(1, D_MODEL)
        wbr = w_br[l].astype(BF16)
        wout = w_out[l].astype(BF16)
        final = l == depth - 1

        res = _xmix_prompt(xp, gnorm, w_bf, spread(sel_p), l, consts_p, prm, bp, tp)
        for k in range(5):
            outs_p[k].append(res[1 + k])
        outs_p[5].append(res[6][:, 0, :HEADS])
        xp = _merge(xp, gnorm, res[0], w_gate, wbr, wout, fg, final, tm=512)

        ps, pifs = _inproj(xs, gnorm, w_bf, spread(sel_s), l, tm=bs * ts, tn=2048)
        n_rows = jnp.broadcast_to(
            state_mlstm_n[l].reshape(ntile, 1, sq, HEADS * HEAD_D),
            (ntile, ts, sq, HEADS * HEAD_D)).reshape(bs * ts, HEADS * HEAD_D)
        m_rows = jnp.broadcast_to(
            state_mlstm_m[l].reshape(ntile, 1, sq, HEADS, 1),
            (ntile, ts, sq, HEADS, HEAD_D)).reshape(bs * ts, HEADS * HEAD_D)
        st = dict(bufa=state_conv_a, sret=state_ret, bufc=state_conv_c, cml=state_mlstm_C,
                  nrows=n_rows, mrows=m_rows)
        res = _mix_sample(ps, pifs, consts_s, prm, st, bs, l, depth, stacked)
        stacked = tuple(res[1:5])
        n_s.append(res[5].reshape(ntile, ts, sq, HEADS, HEAD_D)[:, 0].reshape(bs, HEADS, HEAD_D))
        m_s.append(res[6].reshape(ntile, ts, sq, HEADS, HEAD_D)[:, 0, :, :, 0].reshape(bs, HEADS))
        xs = _merge(xs, gnorm, res[0], w_gate, wbr, wout, fg, final, tm=bs * ts)

    y_prompt = xp.reshape(bp, tp, D_MODEL)
    y_sample = xs.reshape(ntile, ts, sq, D_MODEL).transpose(0, 2, 1, 3).reshape(bs, ts, D_MODEL)
    sp = [jnp.stack(a, axis=0) for a in outs_p]
    return (y_prompt, y_sample, sp[0], stacked[0], sp[1], stacked[1], sp[2], stacked[2],
            sp[3], stacked[3], sp[4], jnp.stack(n_s, axis=0), sp[5], jnp.stack(m_s, axis=0))
```

```python
import functools

import numpy as np
import jax
import jax.numpy as jnp
from jax import lax
from jax.experimental import pallas as pl
from jax.experimental.pallas import tpu as pltpu

F32 = jnp.float32
BF16 = jnp.bfloat16

D_MODEL = 1024
BRANCH_W = 512
N_BRANCH = 4
HEADS = 4
HEAD_D = 128
CONV_A_W = 3
CONV_C_W = 31
ROPE_BASE = 10000.0
EPS = 1e-6
N_MAIN = 16 * BRANCH_W
N_IF = 256
N_GATE = N_BRANCH * D_MODEL
ROWS = 128
STEP_ROWS = 4 * ROWS
W_BLOCK = N_MAIN + 512
SAMPLE_SEQS = 32
SUBLANES = 8
VMEM_LIMIT = 56 * 1024 * 1024

_COL = dict(a_b=0, a_c=512, a_u=1024, a_z=1536, r_q=2048, r_k=2560, r_v=3072, r_z=3584,
            glu_v=4096, glu_g=4608, c_z=5120, m_q=5632, m_k=6144, m_v=6656, m_o=7168, m_z=7680)


def _bdot(a, b):
    return jnp.dot(a, b, preferred_element_type=F32)


def _bdot_nt(a, b):
    return lax.dot_general(a, b, (((1,), (1,)), ((), ())), preferred_element_type=F32)


def _exact_mm(mat_b16, x):
    hi = x.astype(BF16)
    r1 = x - hi.astype(F32)
    mid = r1.astype(BF16)
    lo = (r1 - mid.astype(F32)).astype(BF16)
    return _bdot(mat_b16, hi) + _bdot(mat_b16, mid) + _bdot(mat_b16, lo)


def _silu(x):
    return x * jax.nn.sigmoid(x)


def _log_sigmoid(x):
    return jnp.minimum(x, 0.0) - jnp.log1p(jnp.exp(-jnp.abs(x)))


def _head_norm(y):
    mu = jnp.mean(y, axis=-1, keepdims=True)
    yc = y - mu
    var = jnp.mean(yc * yc, axis=-1, keepdims=True)
    return yc * lax.rsqrt(var + EPS)


def _rotary(x, c2, s2):
    return x * c2 + pltpu.roll(x, HEAD_D // 2, 1) * s2


def _rmsnorm(x, g):
    ms = jnp.mean(x * x, axis=-1, keepdims=True)
    return (x * lax.rsqrt(ms + EPS)) * g


def _resident(shape, imap):
    return pl.BlockSpec(shape, imap, pipeline_mode=pl.Buffered(1))


def _inproj_kernel(x_ref, g_ref, w_ref, wif_ref, p_ref, pif_ref, hb_ref):
    @pl.when(pl.program_id(1) == 0)
    def _():
        hb = _rmsnorm(x_ref[...], g_ref[...]).astype(BF16)
        hb_ref[...] = hb
        pif_ref[...] = _bdot(hb, wif_ref[...])

    p_ref[...] = _bdot(hb_ref[...], w_ref[...])


def _inproj(x, g, w_all, wif, layer, tm, tn):
    m = x.shape[0]
    nif = wif.shape[1]
    return pl.pallas_call(
        _inproj_kernel,
        grid=(m // tm, N_MAIN // tn),
        in_specs=[
            pl.BlockSpec((tm, D_MODEL), lambda i, j: (i, 0)),
            pl.BlockSpec((1, D_MODEL), lambda i, j: (0, 0)),
            pl.BlockSpec((None, D_MODEL, tn), lambda i, j: (layer, 0, j)),
            pl.BlockSpec((D_MODEL, nif), lambda i, j: (0, 0)),
        ],
        out_specs=[
            pl.BlockSpec((tm, tn), lambda i, j: (i, j)),
            pl.BlockSpec((tm, nif), lambda i, j: (i, 0)),
        ],
        out_shape=[jax.ShapeDtypeStruct((m, N_MAIN), F32), jax.ShapeDtypeStruct((m, nif), F32)],
        scratch_shapes=[pltpu.VMEM((tm, D_MODEL), BF16)],
        compiler_params=pltpu.CompilerParams(
            dimension_semantics=("arbitrary", "arbitrary"), vmem_limit_bytes=VMEM_LIMIT),
        name="inproj",
    )(x, g, w_all, wif)


def _merge_kernel(x_ref, g_ref, o_ref, wg_ref, wbr_ref, wout_ref, fg_ref, y_ref, hb_ref, *, final):
    x = x_ref[...]
    hb_ref[...] = _rmsnorm(x, g_ref[...]).astype(BF16)
    acc = None
    for n in range(N_BRANCH):
        proj = _bdot(o_ref[:, n * BRANCH_W:(n + 1) * BRANCH_W], wbr_ref[n])
        gate = jax.nn.sigmoid(_bdot(hb_ref[...], wg_ref[:, n * D_MODEL:(n + 1) * D_MODEL]))
        term = gate * proj
        acc = term if acc is None else acc + term
    y = x + _bdot(acc.astype(BF16), wout_ref[...])
    if final:
        y = _rmsnorm(y, fg_ref[...])
    y_ref[...] = y


def _merge(x, g, o, wg, wbr, wout, fg, final, tm):
    m = x.shape[0]
    resident = lambda shape: _resident(shape, lambda i: (0,) * len(shape))
    return pl.pallas_call(
        functools.partial(_merge_kernel, final=final),
        grid=(m // tm,),
        in_specs=[
            pl.BlockSpec((tm, D_MODEL), lambda i: (i, 0)),
            resident((1, D_MODEL)),
            pl.BlockSpec((tm, N_BRANCH * BRANCH_W), lambda i: (i, 0)),
            resident((D_MODEL, N_GATE)),
            resident((N_BRANCH, BRANCH_W, D_MODEL)),
            resident((D_MODEL, D_MODEL)),
            resident((1, D_MODEL)),
        ],
        out_specs=pl.BlockSpec((tm, D_MODEL), lambda i: (i, 0)),
        out_shape=jax.ShapeDtypeStruct((m, D_MODEL), F32),
        scratch_shapes=[pltpu.VMEM((tm, D_MODEL), BF16)],
        compiler_params=pltpu.CompilerParams(
            dimension_semantics=("arbitrary",), vmem_limit_bytes=VMEM_LIMIT),
        name="merge",
    )(x, g, o, wg, wbr, wout, fg)


def _tile_masks(sample):
    row = lax.broadcasted_iota(jnp.int32, (ROWS, ROWS), 0)
    col = lax.broadcasted_iota(jnp.int32, (ROWS, ROWS), 1)
    if not sample:
        return row >= col, None
    same = (row & (SAMPLE_SEQS - 1)) == (col & (SAMPLE_SEQS - 1))
    return jnp.logical_and(same, row >= col), same


def _ml_prelims(pi, pf, bi, bfo, cmask, smask):
    ai = pi + bi
    lf = _log_sigmoid(pf + bfo)
    tri = jnp.where(cmask, 1.0, 0.0).astype(BF16)
    if smask is None:
        bones = jnp.ones((ROWS, ROWS), BF16)
    else:
        bones = jnp.where(smask, 1.0, 0.0).astype(BF16)
    fcum = _exact_mm(tri, lf)
    ftot = _exact_mm(bones, lf)
    g = (ftot - fcum) + ai
    return dict(ai_t=ai.T, fcum=fcum, fcum_t=fcum.T, ftot=ftot, g=g, g_t=g.T, bones=bones)


def _ml_gates(pre, col, m_prev_c, cmask, smask):
    f_c = pre["fcum"][:, col:col + 1]
    f_r = pre["fcum_t"][col:col + 1, :]
    ig_r = pre["ai_t"][col:col + 1, :]
    g_c = pre["g"][:, col:col + 1]
    g_r = pre["g_t"][col:col + 1, :]
    ftot_c = pre["ftot"][:, col:col + 1]
    logw = jnp.where(cmask, (f_c - f_r) + ig_r, -jnp.inf)
    b = f_c + m_prev_c
    m_t = jnp.maximum(jnp.max(logw, axis=1, keepdims=True), b)
    wts = jnp.exp(logw - m_t)
    inter = jnp.exp(b - m_t)
    if smask is None:
        gmax = jnp.max(g_r, axis=1, keepdims=True)
    else:
        gmax = jnp.max(jnp.where(smask, g_r, -jnp.inf), axis=1, keepdims=True)
    m_new = jnp.maximum(ftot_c + m_prev_c, gmax)
    ws = jnp.exp(g_c - m_new)
    dec = jnp.exp((ftot_c + m_prev_c) - m_new)
    return dict(m_t=m_t, wts=wts, inter=inter, m_new=m_new, ws=ws, dec=dec)


def _ml_output(gt, qb, kb, vb, inter_term, qn):
    s = _bdot_nt(qb, kb) * gt["wts"]
    num = _bdot(s.astype(BF16), vb) + inter_term * gt["inter"]
    den = jnp.sum(s, axis=1, keepdims=True) + qn * gt["inter"]
    return num / jnp.maximum(jnp.abs(den), jnp.exp(-gt["m_t"]))


def _branch_c_post(conv, c_z, cb, lng, lnb):
    y = conv + cb
    mu = jnp.mean(y, axis=-1, keepdims=True)
    yc = y - mu
    var = jnp.mean(yc * yc, axis=-1, keepdims=True)
    ln = (yc * lax.rsqrt(var + EPS)) * lng + lnb
    return _silu(c_z) * _silu(ln)


_GROUPS = ((0, 2048), (2048, 4096), (4096, 5632), (5632, N_MAIN))


def _mix_chunk(hb_ref, w_ref, wif_ref, or0, o_ref, tab, prm, scr, cdec):
    c2_ref, s2_ref, decay_ref, qdec_ref, kdec_ref = tab
    wa_ref, wc_ref, cb_ref, lng_ref, lnb_ref, rgn_ref, mgn_ref, bi_ref, bf_ref = prm
    exta, extc, shc, s_scr, c_scr, n_scr, m_scr = scr
    orows = slice(or0, or0 + ROWS)
    projected = {}

    def pcol(name, lo=0, w=BRANCH_W):
        c0 = _COL[name]
        g0, g1 = next(g for g in _GROUPS if g[0] <= c0 < g[1])
        if g0 not in projected:
            projected[g0] = _bdot(hb_ref[...], w_ref[:, g0:g1])
        return projected[g0][:, c0 - g0 + lo:c0 - g0 + lo + w]

    ua = pcol("a_c") * pcol("a_u")
    exta[SUBLANES:SUBLANES + ROWS, :] = ua
    conv_a = (wa_ref[0:1, :] * exta[SUBLANES - 2:SUBLANES - 2 + ROWS, :]
              + wa_ref[1:2, :] * exta[SUBLANES - 1:SUBLANES - 1 + ROWS, :] + wa_ref[2:3, :] * ua)
    o_a = _silu(pcol("a_z")) * (pcol("a_b") * conv_a)
    o_ref[orows, 0:512] = o_a.astype(BF16)
    exta[0:SUBLANES, :] = exta[ROWS:ROWS + SUBLANES, :]

    extc[32:32 + ROWS, :] = pcol("glu_v") * jax.nn.sigmoid(pcol("glu_g"))
    span = ROWS + 32 - SUBLANES
    conv_blocks = []
    for lb in range(BRANCH_W // 128):
        lanes = slice(lb * 128, (lb + 1) * 128)
        sh = shc.at[lb % 2]
        for ph in range(1, SUBLANES):
            sh[ph - 1] = extc[ph:ph + span, lanes]
        acc = None
        for ph in range(SUBLANES):
            for a in range(span // SUBLANES - ROWS // SUBLANES + 1):
                j = a * SUBLANES + ph - 2
                if 0 <= j < CONV_C_W:
                    arows = slice(a * SUBLANES, a * SUBLANES + ROWS)
                    shifted = extc[arows, lanes] if ph == 0 else sh[ph - 1, arows, :]
                    term = wc_ref[j:j + 1, lanes] * shifted
                    acc = term if acc is None else acc + term
        conv_blocks.append(acc + wc_ref[CONV_C_W - 1:CONV_C_W, lanes] * extc[32:32 + ROWS, lanes])
    conv_c = jnp.concatenate(conv_blocks, axis=1)
    o_c = _branch_c_post(conv_c, pcol("c_z"), cb_ref[...], lng_ref[...], lnb_ref[...])
    o_ref[orows, 1024:1536] = o_c.astype(BF16)
    extc[0:32, :] = extc[ROWS:ROWS + 32, :]

    c2 = c2_ref[orows, :]
    s2 = s2_ref[orows, :]
    for h in range(HEADS):
        lo = h * HEAD_D
        hs = slice(lo, lo + HEAD_D)
        q = _rotary(pcol("r_q", lo, HEAD_D), c2, s2)
        k = _rotary(pcol("r_k", lo, HEAD_D), c2, s2) * (HEAD_D ** -0.5)
        qb = q.astype(BF16)
        kb = k.astype(BF16)
        vb = pcol("r_v", lo, HEAD_D).astype(BF16)
        s_old = s_scr[h]
        sc = _bdot_nt(qb, kb) * decay_ref[h]
        ret = _bdot(sc.astype(BF16), vb) + _bdot(qb, s_old.astype(BF16)) * qdec_ref[:, hs]
        s_scr[h] = s_old * cdec[h] + _bdot((k * kdec_ref[:, hs]).T.astype(BF16), vb)
        o_r = _silu(pcol("r_z", lo, HEAD_D)) * (_head_norm(ret) * rgn_ref[:, hs])
        o_ref[orows, 512 + lo:512 + lo + HEAD_D] = o_r.astype(BF16)

    cmask, smask = _tile_masks(False)
    pif = _bdot(hb_ref[...], wif_ref[...])
    pre = _ml_prelims(pif[:, 0:128], pif[:, 128:256], bi_ref[...], bf_ref[...], cmask, smask)
    lane = lax.broadcasted_iota(jnp.int32, (ROWS, ROWS), 1)
    m_rows = m_scr[...]
    m_rows_new = jnp.zeros((ROWS, ROWS), F32)
    for h in range(HEADS):
        lo = h * HEAD_D
        hs = slice(lo, lo + HEAD_D)
        gt = _ml_gates(pre, h, m_rows[:, h:h + 1], cmask, smask)
        q = pcol("m_q", lo, HEAD_D)
        k = pcol("m_k", lo, HEAD_D) * (HEAD_D ** -0.5)
        qb = q.astype(BF16)
        kb = k.astype(BF16)
        vb = pcol("m_v", lo, HEAD_D).astype(BF16)
        c_old = c_scr[h]
        n_old = n_scr[:, hs]
        qn = jnp.sum(q * n_old, axis=1, keepdims=True)
        hout = _ml_output(gt, qb, kb, vb, _bdot(qb, c_old.astype(BF16)), qn)
        kw = k * gt["ws"]
        dec_full = jnp.broadcast_to(gt["dec"], (ROWS, ROWS))
        c_scr[h] = c_old * dec_full + _bdot(kw.T.astype(BF16), vb)
        n_scr[:, hs] = n_old * dec_full + _bdot(pre["bones"], kw.astype(BF16))
        m_rows_new = jnp.where(lane == h, jnp.broadcast_to(gt["m_new"], (ROWS, ROWS)), m_rows_new)
        hm = jax.nn.sigmoid(pcol("m_o", lo, HEAD_D)) * hout
        o_m = _silu(pcol("m_z", lo, HEAD_D)) * (_head_norm(hm) * mgn_ref[:, hs])
        o_ref[orows, 1536 + lo:1536 + lo + HEAD_D] = o_m.astype(BF16)
    m_scr[...] = m_rows_new


def _xmix_kernel(x_ref, g_ref, w_ref, wif_ref,
                 c2_ref, s2_ref, decay_ref, qdec_ref, kdec_ref,
                 wa_ref, wc_ref, cb_ref, lng_ref, lnb_ref, rgn_ref, mgn_ref, bi_ref, bf_ref,
                 o_ref, bufa_ref, sret_ref, bufc_ref, cml_ref, nml_ref, mml_ref,
                 hb_scr, exta, extc, shc, s_scr, c_scr, n_scr, m_scr,
                 *, cdec, steps_per_seq):
    s = pl.program_id(0)
    seq_step = s % steps_per_seq
    tab = (c2_ref, s2_ref, decay_ref, qdec_ref, kdec_ref)
    prm = (wa_ref, wc_ref, cb_ref, lng_ref, lnb_ref, rgn_ref, mgn_ref, bi_ref, bf_ref)
    scr = (exta, extc, shc, s_scr, c_scr, n_scr, m_scr)
    nchunk = STEP_ROWS // ROWS

    @pl.when(seq_step == 0)
    def _():
        exta[0:SUBLANES, :] = jnp.zeros((SUBLANES, BRANCH_W), F32)
        extc[0:32, :] = jnp.zeros((32, BRANCH_W), F32)
        s_scr[...] = jnp.zeros_like(s_scr)
        c_scr[...] = jnp.zeros_like(c_scr)
        n_scr[...] = jnp.zeros_like(n_scr)
        m_scr[...] = jnp.zeros_like(m_scr)

    for ci in range(nchunk):
        hb = hb_scr.at[ci % 2]
        hb[...] = _rmsnorm(x_ref[ci * ROWS:(ci + 1) * ROWS, :], g_ref[...]).astype(BF16)
        _mix_chunk(hb, w_ref, wif_ref, ci * ROWS, o_ref, tab, prm, scr, cdec)

    @pl.when(seq_step == steps_per_seq - 1)
    def _():
        bufa_ref[0] = exta[SUBLANES - (CONV_A_W - 1):SUBLANES, :]
        bufc_ref[0] = extc[32 - (CONV_C_W - 1):32, :]
        for h in range(HEADS):
            sret_ref[0, h] = s_scr[h]
            cml_ref[0, h] = c_scr[h]
            nml_ref[0, h:h + 1, :] = n_scr[0:1, h * HEAD_D:(h + 1) * HEAD_D]
        mml_ref[0] = m_scr[0:1, :]


def _xmix_prompt(x, gnorm, w_all, wif, layer, consts, prm, batch, seq):
    steps_per_seq = seq // STEP_ROWS
    nstep = batch * steps_per_seq
    c1 = lambda shape: _resident(shape, lambda s: (0,) * len(shape))
    rope = pl.BlockSpec((STEP_ROWS, HEAD_D), lambda s: (s % steps_per_seq, 0))
    in_specs = [
        pl.BlockSpec((STEP_ROWS, D_MODEL), lambda s: (s, 0)),
        c1((1, D_MODEL)),
        _resident((None, D_MODEL, W_BLOCK), lambda s: (layer, 0, 0)),
        c1((D_MODEL, N_IF)),
        rope, rope,
        c1((HEADS, ROWS, ROWS)),
        c1((ROWS, BRANCH_W)),
        c1((ROWS, BRANCH_W)),
        c1((CONV_A_W, BRANCH_W)),
        c1((CONV_C_W, BRANCH_W)),
    ] + [c1((1, BRANCH_W))] * 5 + [c1((1, 128))] * 2
    per_b3 = lambda s: (s // steps_per_seq, 0, 0)
    per_b4 = lambda s: (s // steps_per_seq, 0, 0, 0)
    out_specs = [
        pl.BlockSpec((STEP_ROWS, N_BRANCH * BRANCH_W), lambda s: (s, 0)),
        pl.BlockSpec((1, CONV_A_W - 1, BRANCH_W), per_b3),
        pl.BlockSpec((1, HEADS, HEAD_D, HEAD_D), per_b4),
        pl.BlockSpec((1, CONV_C_W - 1, BRANCH_W), per_b3),
        pl.BlockSpec((1, HEADS, HEAD_D, HEAD_D), per_b4),
        pl.BlockSpec((1, HEADS, HEAD_D), per_b3),
        pl.BlockSpec((1, 1, 128), per_b3),
    ]
    out_shape = [
        jax.ShapeDtypeStruct((batch * seq, N_BRANCH * BRANCH_W), BF16),
        jax.ShapeDtypeStruct((batch, CONV_A_W - 1, BRANCH_W), F32),
        jax.ShapeDtypeStruct((batch, HEADS, HEAD_D, HEAD_D), F32),
        jax.ShapeDtypeStruct((batch, CONV_C_W - 1, BRANCH_W), F32),
        jax.ShapeDtypeStruct((batch, HEADS, HEAD_D, HEAD_D), F32),
        jax.ShapeDtypeStruct((batch, HEADS, HEAD_D), F32),
        jax.ShapeDtypeStruct((batch, 1, 128), F32),
    ]
    scratch = [
        pltpu.VMEM((2, ROWS, D_MODEL), BF16),
        pltpu.VMEM((ROWS + SUBLANES, BRANCH_W), F32),
        pltpu.VMEM((ROWS + 32, BRANCH_W), F32),
        pltpu.VMEM((2, SUBLANES - 1, ROWS + 32 - SUBLANES, 128), F32),
        pltpu.VMEM((HEADS, HEAD_D, HEAD_D), F32),
        pltpu.VMEM((HEADS, HEAD_D, HEAD_D), F32),
        pltpu.VMEM((ROWS, BRANCH_W), F32),
        pltpu.VMEM((ROWS, ROWS), F32),
    ]
    return pl.pallas_call(
        functools.partial(_xmix_kernel, cdec=consts["cdec"], steps_per_seq=steps_per_seq),
        grid=(nstep,),
        in_specs=in_specs, out_specs=out_specs, out_shape=out_shape, scratch_shapes=scratch,
        compiler_params=pltpu.CompilerParams(
            dimension_semantics=("arbitrary",), vmem_limit_bytes=VMEM_LIMIT),
        name="xmix_prompt",
    )(x, gnorm, w_all, wif, consts["c2"], consts["s2"], consts["decay"], consts["qdec"],
      consts["kdec"], prm["wa"], prm["wc"], prm["cb"], prm["lng"], prm["lnb"], prm["rgn"],
      prm["mgn"], prm["bi"], prm["bf"])


def _mix_sample_kernel(pa_ref, pc_ref, rq_ref, rk_ref, rv_ref, rz_ref, mq_ref, mk_ref, mv_ref,
                       mo_ref, mz_ref, pif_ref, c2_ref, s2_ref, decay_ref, qdec_ref, kdec_ref,
                       cdec_ref, wa_ref, wc_ref, cb_ref, lng_ref, lnb_ref, rgn_ref, mgn_ref,
                       bi_ref, bf_ref, bufa_ref, sret_ref, bufc_ref, cml_ref, nrows_ref, mrows_ref,
                       o_ref, bufa_o, sret_o, bufc_o, cml_o, nrows_o, mrows_o, dec_scr):
    h = pl.program_id(1)
    sq = SAMPLE_SEQS
    nt = ROWS // sq

    @pl.when(h == 0)
    def _():
        ua = pa_ref[:, 512:1024] * pa_ref[:, 1024:1536]
        ea = [bufa_ref[:, 0, :], bufa_ref[:, 1, :]] + [ua[t * sq:(t + 1) * sq, :] for t in range(nt)]
        conv_a = jnp.concatenate(
            [wa_ref[0:1, :] * ea[t] + wa_ref[1:2, :] * ea[t + 1] + wa_ref[2:3, :] * ea[t + 2]
             for t in range(nt)], axis=0)
        o_ref[:, 0:512] = (_silu(pa_ref[:, 1536:2048]) * (pa_ref[:, 0:512] * conv_a)).astype(BF16)
        bufa_o[:, 0, :] = ea[nt]
        bufa_o[:, 1, :] = ea[nt + 1]
        uc = pc_ref[:, 0:512] * jax.nn.sigmoid(pc_ref[:, 512:1024])
        nb = CONV_C_W - 1
        ec = [bufc_ref[:, i, :] for i in range(nb)] + [uc[t * sq:(t + 1) * sq, :] for t in range(nt)]
        planes = []
        for t in range(nt):
            acc = wc_ref[0:1, :] * ec[t]
            for j in range(1, CONV_C_W):
                acc = acc + wc_ref[j:j + 1, :] * ec[t + j]
            planes.append(acc)
        conv_c = jnp.concatenate(planes, axis=0)
        o_ref[:, 1024:1536] = _branch_c_post(conv_c, pc_ref[:, 1024:1536], cb_ref[...], lng_ref[...],
                                             lnb_ref[...]).astype(BF16)
        for i in range(nb):
            bufc_o[:, i, :] = ec[i + nt]

    cmask, smask = _tile_masks(True)
    rowseq = lax.broadcasted_iota(jnp.int32, (ROWS, ROWS), 0) & (sq - 1)
    laneseq = lax.broadcasted_iota(jnp.int32, (ROWS, ROWS), 1) & (sq - 1)

    c2 = c2_ref[...]
    s2 = s2_ref[...]
    rq = _rotary(rq_ref[...], c2, s2)
    rk = _rotary(rk_ref[...], c2, s2) * (HEAD_D ** -0.5)
    rqb = rq.astype(BF16)
    rkb = rk.astype(BF16)
    rvb = rv_ref[...].astype(BF16)
    rkd_t = (rk * kdec_ref[...]).T
    cdec = cdec_ref[0, 0:1, :]

    pre = _ml_prelims(pif_ref[:, 0:128], pif_ref[:, 128:256], bi_ref[...], bf_ref[...], cmask, smask)
    m_prev = mrows_ref[...]
    gt = _ml_gates(pre, 0, m_prev[:, 0:1], cmask, smask)
    mq = mq_ref[...]
    mk = mk_ref[...] * (HEAD_D ** -0.5)
    mqb = mq.astype(BF16)
    mkb = mk.astype(BF16)
    mvb = mv_ref[...].astype(BF16)
    kw = mk * gt["ws"]
    kw_t = kw.T
    dec_full = jnp.broadcast_to(gt["dec"], (ROWS, ROWS))
    dec_scr[...] = dec_full

    def per_seq(b, carry):
        inter_r, inter_m = carry
        sb = sret_ref[b, 0]
        inter_r = jnp.where(rowseq == b, _bdot(rqb, sb.astype(BF16)), inter_r)
        k_sel = jnp.where(laneseq == b, rkd_t, 0.0).astype(BF16)
        sret_o[b, 0] = sb * cdec + _bdot(k_sel, rvb)
        cb_ = cml_ref[b, 0]
        inter_m = jnp.where(rowseq == b, _bdot(mqb, cb_.astype(BF16)), inter_m)
        kw_sel = jnp.where(laneseq == b, kw_t, 0.0).astype(BF16)
        cml_o[b, 0] = cb_ * dec_scr[pl.ds(b, 1), :] + _bdot(kw_sel, mvb)
        return inter_r, inter_m

    zero = jnp.zeros((ROWS, HEAD_D), F32)
    inter_r, inter_m = lax.fori_loop(0, sq, per_seq, (zero, zero), unroll=8)

    sc = _bdot_nt(rqb, rkb) * decay_ref[0]
    ret = _bdot(sc.astype(BF16), rvb) + inter_r * qdec_ref[...]
    o_r = (_silu(rz_ref[...]) * (_head_norm(ret) * rgn_ref[...])).astype(BF16)

    n_old = nrows_ref[...]
    qn = jnp.sum(mq * n_old, axis=1, keepdims=True)
    hout = _ml_output(gt, mqb, mkb, mvb, inter_m, qn)
    nrows_o[...] = n_old * dec_full + _bdot(pre["bones"], kw.astype(BF16))
    mrows_o[...] = jnp.broadcast_to(gt["m_new"], (ROWS, ROWS))
    hm = jax.nn.sigmoid(mo_ref[...]) * hout
    o_m = (_silu(mz_ref[...]) * (_head_norm(hm) * mgn_ref[...])).astype(BF16)

    for hh in range(HEADS):
        @pl.when(h == hh)
        def _(hh=hh):
            o_ref[:, 512 + hh * HEAD_D:512 + (hh + 1) * HEAD_D] = o_r
            o_ref[:, 1536 + hh * HEAD_D:1536 + (hh + 1) * HEAD_D] = o_m


def _mix_sample(p, pif, consts, prm, st, nseq, layer, depth, stacked):
    ntile = nseq // SAMPLE_SEQS
    sq = SAMPLE_SEQS
    head_cols = lambda name: pl.BlockSpec(
        (ROWS, HEAD_D), lambda i, h, _o=_COL[name] // HEAD_D: (i, _o + h))
    const2 = lambda i, h: (0, 0)
    per_head2 = lambda i, h: (0, h)
    conv_a_state = pl.BlockSpec((None, sq, CONV_A_W - 1, BRANCH_W), lambda i, h: (layer, i, 0, 0))
    conv_c_state = pl.BlockSpec((None, sq, CONV_C_W - 1, BRANCH_W), lambda i, h: (layer, i, 0, 0))
    matrix_state = pl.BlockSpec((None, sq, 1, HEAD_D, HEAD_D), lambda i, h: (layer, i, h, 0, 0))
    in_specs = [
        pl.BlockSpec((ROWS, 2048), lambda i, h: (i, 0)),
        pl.BlockSpec((ROWS, 2048), lambda i, h: (i, 2)),
    ] + [head_cols(n) for n in ("r_q", "r_k", "r_v", "r_z", "m_q", "m_k", "m_v", "m_o", "m_z")] + [
        pl.BlockSpec((ROWS, 256), lambda i, h: (i, h)),
        pl.BlockSpec((ROWS, HEAD_D), const2),
        pl.BlockSpec((ROWS, HEAD_D), const2),
        pl.BlockSpec((1, ROWS, ROWS), lambda i, h: (h, 0, 0)),
        pl.BlockSpec((ROWS, HEAD_D), per_head2),
        pl.BlockSpec((ROWS, HEAD_D), per_head2),
        pl.BlockSpec((1, 8, 128), lambda i, h: (h, 0, 0)),
        pl.BlockSpec((CONV_A_W, BRANCH_W), const2),
        pl.BlockSpec((CONV_C_W, BRANCH_W), const2),
        pl.BlockSpec((1, BRANCH_W), const2),
        pl.BlockSpec((1, BRANCH_W), const2),
        pl.BlockSpec((1, BRANCH_W), const2),
        pl.BlockSpec((1, HEAD_D), per_head2),
        pl.BlockSpec((1, HEAD_D), per_head2),
        pl.BlockSpec((1, 128), per_head2),
        pl.BlockSpec((1, 128), per_head2),
        conv_a_state, matrix_state, conv_c_state, matrix_state,
        pl.BlockSpec((ROWS, HEAD_D), lambda i, h: (i, h)),
        pl.BlockSpec((ROWS, HEAD_D), lambda i, h: (i, h)),
    ]
    out_specs = [
        pl.BlockSpec((ROWS, N_BRANCH * BRANCH_W), lambda i, h: (i, 0)),
        conv_a_state, matrix_state, conv_c_state, matrix_state,
        pl.BlockSpec((ROWS, HEAD_D), lambda i, h: (i, h)),
        pl.BlockSpec((ROWS, HEAD_D), lambda i, h: (i, h)),
    ]
    nrow = ntile * ROWS
    out_shape = [
        jax.ShapeDtypeStruct((nrow, N_BRANCH * BRANCH_W), BF16),
        jax.ShapeDtypeStruct((depth, nseq, CONV_A_W - 1, BRANCH_W), F32),
        jax.ShapeDtypeStruct((depth, nseq, HEADS, HEAD_D, HEAD_D), F32),
        jax.ShapeDtypeStruct((depth, nseq, CONV_C_W - 1, BRANCH_W), F32),
        jax.ShapeDtypeStruct((depth, nseq, HEADS, HEAD_D, HEAD_D), F32),
        jax.ShapeDtypeStruct((nrow, HEADS * HEAD_D), F32),
        jax.ShapeDtypeStruct((nrow, HEADS * HEAD_D), F32),
    ]
    args = [p] * 11 + [
        pif, consts["c2"], consts["s2"], consts["decay"], consts["qdec"], consts["kdec"],
        consts["cdec"], prm["wa"], prm["wc"], prm["cb"], prm["lng"], prm["lnb"], prm["rgn"],
        prm["mgn"], prm["bi_h"], prm["bf_h"],
        st["bufa"], st["sret"], st["bufc"], st["cml"], st["nrows"], st["mrows"]]
    assert len(args) == len(in_specs)
    n_in = len(args)
    aliases = {}
    if stacked is not None:
        aliases = {n_in + k: 1 + k for k in range(4)}
        in_specs = in_specs + [pl.BlockSpec(memory_space=pl.ANY)] * 4
        args = args + list(stacked)

    def entry(*refs):
        _mix_sample_kernel(*refs[:n_in], *refs[n_in + len(aliases):])

    return pl.pallas_call(
        entry,
        grid=(ntile, HEADS),
        in_specs=in_specs, out_specs=out_specs, out_shape=out_shape,
        scratch_shapes=[pltpu.VMEM((ROWS, ROWS), F32)],
        input_output_aliases=aliases,
        compiler_params=pltpu.CompilerParams(
            dimension_semantics=("arbitrary", "arbitrary"), vmem_limit_bytes=VMEM_LIMIT),
        name="mix_sample",
    )(*args)


def _rope_tables(pos):
    inv = ROPE_BASE ** (-jnp.arange(0, HEAD_D, 2, dtype=F32) / HEAD_D)
    ang = pos.astype(F32)[:, None] * inv[None, :]
    cos = jnp.cos(ang)
    sin = jnp.sin(ang)
    return jnp.concatenate([cos, cos], axis=1), jnp.concatenate([-sin, sin], axis=1)


def _decay_tables(t_of_row, seq_of_row, chunk_len):
    log_g = np.log1p(-(2.0 ** (-5.0 - np.arange(HEADS, dtype=np.float64))))
    dt = t_of_row[:, None] - t_of_row[None, :]
    ok = (seq_of_row[:, None] == seq_of_row[None, :]) & (dt >= 0)
    decay = np.where(ok[None], np.exp(np.maximum(dt, 0)[None] * log_g[:, None, None]), 0.0)
    qdec = np.exp((t_of_row[:, None] + 1.0) * log_g[None, :])
    kdec = np.exp((chunk_len - 1.0 - t_of_row)[:, None] * log_g[None, :])
    cdec = np.exp(chunk_len * log_g)
    rep = lambda a: jnp.asarray(np.repeat(a, HEAD_D, axis=1), F32)
    return jnp.asarray(decay, F32), rep(qdec), rep(kdec), cdec


def kernel(x_prompt, x_sample, state_conv_a, state_ret, state_conv_c, state_mlstm_C,
           state_mlstm_n, state_mlstm_m, norm_g, w_in, conv_a_w, conv_c_w, conv_c_b,
           ln_c_g, ln_c_b, ret_gn_g, ml_gn_g, ml_gate_b, w_br, w_out, final_g):
    bp, tp, _ = x_prompt.shape
    bs, ts, _ = x_sample.shape
    depth = w_in.shape[0]
    past_len = 16384
    sq = SAMPLE_SEQS
    ntile = bs // sq
    assert tp % STEP_ROWS == 0 and ts * sq == ROWS and bs % sq == 0

    r = np.arange(ROWS)
    c2p, s2p = _rope_tables(jnp.arange(tp, dtype=jnp.int32))
    decay_p, qdec_p, kdec_p, cdec_p = _decay_tables(r.astype(np.float64), np.zeros(ROWS), float(ROWS))
    consts_p = dict(c2=c2p, s2=s2p, decay=decay_p, qdec=qdec_p, kdec=kdec_p,
                    cdec=tuple(float(v) for v in cdec_p))
    t_s = (r // sq).astype(np.float64)
    c2s, s2s = _rope_tables(past_len + jnp.asarray(r // sq, jnp.int32))
    decay_s, qdec_s, kdec_s, cdec_s = _decay_tables(t_s, r % sq, float(ts))
    cdec_s_arr = jnp.asarray(np.broadcast_to(cdec_s[:, None, None], (HEADS, 8, 128)), F32)
    consts_s = dict(c2=c2s, s2=s2s, decay=decay_s, qdec=qdec_s, kdec=kdec_s, cdec=cdec_s_arr)

    xp = x_prompt.reshape(bp * tp, D_MODEL)
    xs = x_sample.reshape(ntile, sq, ts, D_MODEL).transpose(0, 2, 1, 3).reshape(bs * ts, D_MODEL)

    fg = final_g.reshape(1, D_MODEL)
    outs_p = [[] for _ in range(6)]
    n_s, m_s = [], []
    stacked = None
    sel_p = np.zeros((2 * HEADS, N_IF), np.float32)
    sel_s = np.zeros((2 * HEADS, HEADS * N_IF), np.float32)
    for h in range(HEADS):
        sel_p[h, h] = 1.0
        sel_p[HEADS + h, 128 + h] = 1.0
        sel_s[h, h * N_IF] = 1.0
        sel_s[HEADS + h, h * N_IF + 128] = 1.0
    w_bf = w_in.astype(BF16)
    pad_row = lambda v: jnp.pad(v, (0, 128 - v.shape[0])).reshape(1, 128)
    for l in range(depth):
        w_if = w_in[l, :, N_MAIN:N_MAIN + 2 * HEADS]
        spread = lambda sel: jnp.dot(w_if, jnp.asarray(sel), precision=lax.Precision.HIGHEST).astype(BF16)
        w_gate = w_bf[l, :, N_MAIN + 2 * HEADS:]
        gb = ml_gate_b[l]
        prm = dict(
            wa=conv_a_w[l], wc=conv_c_w[l], cb=conv_c_b[l].reshape(1, -1),
            lng=ln_c_g[l].reshape(1, -1), lnb=ln_c_b[l].reshape(1, -1),
            rgn=ret_gn_g[l].reshape(1, -1), mgn=ml_gn_g[l].reshape(1, -1),
            bi=pad_row(gb[:HEADS]), bf=pad_row(gb[HEADS:]),
            bi_h=jnp.concatenate([pad_row(gb[h:h + 1]) for h in range(HEADS)], axis=1),
            bf_h=jnp.concatenate([pad_row(gb[HEADS + h:HEADS + h + 1]) for h in range(HEADS)], axis=1),
        )
        gnorm = norm_g[l].reshape(1, D_MODEL)
        wbr = w_br[l].astype(BF16)
        wout = w_out[l].astype(BF16)
        final = l == depth - 1

        res = _xmix_prompt(xp, gnorm, w_bf, spread(sel_p), l, consts_p, prm, bp, tp)
        for k in range(5):
            outs_p[k].append(res[1 + k])
        outs_p[5].append(res[6][:, 0, :HEADS])
        xp = _merge(xp, gnorm, res[0], w_gate, wbr, wout, fg, final, tm=512)

        ps, pifs = _inproj(xs, gnorm, w_bf, spread(sel_s), l, tm=bs * ts, tn=2048)
        n_rows = jnp.broadcast_to(
            state_mlstm_n[l].reshape(ntile, 1, sq, HEADS * HEAD_D),
            (ntile, ts, sq, HEADS * HEAD_D)).reshape(bs * ts, HEADS * HEAD_D)
        m_rows = jnp.broadcast_to(
            state_mlstm_m[l].reshape(ntile, 1, sq, HEADS, 1),
            (ntile, ts, sq, HEADS, HEAD_D)).reshape(bs * ts, HEADS * HEAD_D)
        st = dict(bufa=state_conv_a, sret=state_ret, bufc=state_conv_c, cml=state_mlstm_C,
                  nrows=n_rows, mrows=m_rows)
        res = _mix_sample(ps, pifs, consts_s, prm, st, bs, l, depth, stacked)
        stacked = tuple(res[1:5])
        n_s.append(res[5].reshape(ntile, ts, sq, HEADS, HEAD_D)[:, 0].reshape(bs, HEADS, HEAD_D))
        m_s.append(res[6].reshape(ntile, ts, sq, HEADS, HEAD_D)[:, 0, :, :, 0].reshape(bs, HEADS))
        xs = _merge(xs, gnorm, res[0], w_gate, wbr, wout, fg, final, tm=bs * ts)

    y_prompt = xp.reshape(bp, tp, D_MODEL)
    y_sample = xs.reshape(ntile, ts, sq, D_MODEL).transpose(0, 2, 1, 3).reshape(bs, ts, D_MODEL)
    sp = [jnp.stack(a, axis=0) for a in outs_p]
    return (y_prompt, y_sample, sp[0], stacked[0], sp[1], stacked[1], sp[2], stacked[2],
            sp[3], stacked[3], sp[4], jnp.stack(n_s, axis=0), sp[5], jnp.stack(m_s, axis=0))
```

```python
import functools

import numpy as np
import jax
import jax.numpy as jnp
from jax import lax
from jax.experimental import pallas as pl
from jax.experimental.pallas import tpu as pltpu

F32 = jnp.float32
BF16 = jnp.bfloat16

D_MODEL = 1024
BRANCH_W = 512
N_BRANCH = 4
HEADS = 4
HEAD_D = 128
CONV_A_W = 3
CONV_C_W = 31
ROPE_BASE = 10000.0
EPS = 1e-6
N_MAIN = 16 * BRANCH_W
N_IF = 256
N_GATE = N_BRANCH * D_MODEL
ROWS = 128
STEP_ROWS = 4 * ROWS
W_BLOCK = N_MAIN + 512
SAMPLE_SEQS = 32
SUBLANES = 8
LANE_BLOCKS = BRANCH_W // 128
VMEM_LIMIT = 56 * 1024 * 1024

_COL = dict(a_b=0, a_c=512, a_u=1024, a_z=1536, r_q=2048, r_k=2560, r_v=3072, r_z=3584,
            glu_v=4096, glu_g=4608, c_z=5120, m_q=5632, m_k=6144, m_v=6656, m_o=7168, m_z=7680)


def _bdot(a, b):
    return jnp.dot(a, b, preferred_element_type=F32)


def _bdot_nt(a, b):
    return lax.dot_general(a, b, (((1,), (1,)), ((), ())), preferred_element_type=F32)


def _exact_mm(mat_b16, x):
    hi = x.astype(BF16)
    r1 = x - hi.astype(F32)
    mid = r1.astype(BF16)
    lo = (r1 - mid.astype(F32)).astype(BF16)
    return _bdot(mat_b16, hi) + _bdot(mat_b16, mid) + _bdot(mat_b16, lo)


def _silu(x):
    return x * jax.nn.sigmoid(x)


def _log_sigmoid(x):
    return jnp.minimum(x, 0.0) - jnp.log1p(jnp.exp(-jnp.abs(x)))


def _head_norm(y):
    mu = jnp.mean(y, axis=-1, keepdims=True)
    yc = y - mu
    var = jnp.mean(yc * yc, axis=-1, keepdims=True)
    return yc * lax.rsqrt(var + EPS)


def _rotary(x, c2, s2):
    return x * c2 + pltpu.roll(x, HEAD_D // 2, 1) * s2


def _rmsnorm(x, g):
    ms = jnp.mean(x * x, axis=-1, keepdims=True)
    return (x * lax.rsqrt(ms + EPS)) * g


def _resident(shape, imap):
    return pl.BlockSpec(shape, imap, pipeline_mode=pl.Buffered(1))


def _inproj_kernel(x_ref, g_ref, w_ref, wif_ref, p_ref, pif_ref, hb_ref):
    @pl.when(pl.program_id(1) == 0)
    def _():
        hb = _rmsnorm(x_ref[...], g_ref[...]).astype(BF16)
        hb_ref[...] = hb
        pif_ref[...] = _bdot(hb, wif_ref[...])

    p_ref[...] = _bdot(hb_ref[...], w_ref[...])


def _inproj(x, g, w_all, wif, layer, tm, tn):
    m = x.shape[0]
    nif = wif.shape[1]
    return pl.pallas_call(
        _inproj_kernel,
        grid=(m // tm, N_MAIN // tn),
        in_specs=[
            pl.BlockSpec((tm, D_MODEL), lambda i, j: (i, 0)),
            pl.BlockSpec((1, D_MODEL), lambda i, j: (0, 0)),
            pl.BlockSpec((None, D_MODEL, tn), lambda i, j: (layer, 0, j)),
            pl.BlockSpec((D_MODEL, nif), lambda i, j: (0, 0)),
        ],
        out_specs=[
            pl.BlockSpec((tm, tn), lambda i, j: (i, j)),
            pl.BlockSpec((tm, nif), lambda i, j: (i, 0)),
        ],
        out_shape=[jax.ShapeDtypeStruct((m, N_MAIN), F32), jax.ShapeDtypeStruct((m, nif), F32)],
        scratch_shapes=[pltpu.VMEM((tm, D_MODEL), BF16)],
        compiler_params=pltpu.CompilerParams(
            dimension_semantics=("arbitrary", "arbitrary"), vmem_limit_bytes=VMEM_LIMIT),
        name="inproj",
    )(x, g, w_all, wif)


def _merge_kernel(x_ref, g_ref, o_ref, wg_ref, wbr_ref, wout_ref, fg_ref, y_ref, hb_ref, *, final):
    x = x_ref[...]
    hb_ref[...] = _rmsnorm(x, g_ref[...]).astype(BF16)
    acc = None
    for n in range(N_BRANCH):
        proj = _bdot(o_ref[:, n * BRANCH_W:(n + 1) * BRANCH_W], wbr_ref[n])
        gate = jax.nn.sigmoid(_bdot(hb_ref[...], wg_ref[:, n * D_MODEL:(n + 1) * D_MODEL]))
        term = gate * proj
        acc = term if acc is None else acc + term
    y = x + _bdot(acc.astype(BF16), wout_ref[...])
    if final:
        y = _rmsnorm(y, fg_ref[...])
    y_ref[...] = y


def _merge(x, g, o, wg, wbr, wout, fg, final, tm):
    m = x.shape[0]
    resident = lambda shape: _resident(shape, lambda i: (0,) * len(shape))
    return pl.pallas_call(
        functools.partial(_merge_kernel, final=final),
        grid=(m // tm,),
        in_specs=[
            pl.BlockSpec((tm, D_MODEL), lambda i: (i, 0)),
            resident((1, D_MODEL)),
            pl.BlockSpec((tm, N_BRANCH * BRANCH_W), lambda i: (i, 0)),
            resident((D_MODEL, N_GATE)),
            resident((N_BRANCH, BRANCH_W, D_MODEL)),
            resident((D_MODEL, D_MODEL)),
            resident((1, D_MODEL)),
        ],
        out_specs=pl.BlockSpec((tm, D_MODEL), lambda i: (i, 0)),
        out_shape=jax.ShapeDtypeStruct((m, D_MODEL), F32),
        scratch_shapes=[pltpu.VMEM((tm, D_MODEL), BF16)],
        compiler_params=pltpu.CompilerParams(
            dimension_semantics=("arbitrary",), vmem_limit_bytes=VMEM_LIMIT),
        name="merge",
    )(x, g, o, wg, wbr, wout, fg)


def _tile_masks(sample):
    row = lax.broadcasted_iota(jnp.int32, (ROWS, ROWS), 0)
    col = lax.broadcasted_iota(jnp.int32, (ROWS, ROWS), 1)
    if not sample:
        return row >= col, None
    same = (row & (SAMPLE_SEQS - 1)) == (col & (SAMPLE_SEQS - 1))
    return jnp.logical_and(same, row >= col), same


def _ml_prelims(pi, pf, bi, bfo, cmask, smask):
    ai = pi + bi
    lf = _log_sigmoid(pf + bfo)
    tri = jnp.where(cmask, 1.0, 0.0).astype(BF16)
    if smask is None:
        bones = jnp.ones((ROWS, ROWS), BF16)
    else:
        bones = jnp.where(smask, 1.0, 0.0).astype(BF16)
    fcum = _exact_mm(tri, lf)
    ftot = _exact_mm(bones, lf)
    g = (ftot - fcum) + ai
    return dict(ai_t=ai.T, fcum=fcum, fcum_t=fcum.T, ftot=ftot, g=g, g_t=g.T, bones=bones)


def _ml_gates(pre, col, m_prev_c, cmask, smask):
    f_c = pre["fcum"][:, col:col + 1]
    f_r = pre["fcum_t"][col:col + 1, :]
    ig_r = pre["ai_t"][col:col + 1, :]
    g_c = pre["g"][:, col:col + 1]
    g_r = pre["g_t"][col:col + 1, :]
    ftot_c = pre["ftot"][:, col:col + 1]
    logw = jnp.where(cmask, (f_c - f_r) + ig_r, -jnp.inf)
    b = f_c + m_prev_c
    m_t = jnp.maximum(jnp.max(logw, axis=1, keepdims=True), b)
    wts = jnp.exp(logw - m_t)
    inter = jnp.exp(b - m_t)
    if smask is None:
        gmax = jnp.max(g_r, axis=1, keepdims=True)
    else:
        gmax = jnp.max(jnp.where(smask, g_r, -jnp.inf), axis=1, keepdims=True)
    m_new = jnp.maximum(ftot_c + m_prev_c, gmax)
    ws = jnp.exp(g_c - m_new)
    dec = jnp.exp((ftot_c + m_prev_c) - m_new)
    return dict(m_t=m_t, wts=wts, inter=inter, m_new=m_new, ws=ws, dec=dec)


def _ml_output(gt, qb, kb, vb, inter_term, qn):
    s = _bdot_nt(qb, kb) * gt["wts"]
    num = _bdot(s.astype(BF16), vb) + inter_term * gt["inter"]
    den = jnp.sum(s, axis=1, keepdims=True) + qn * gt["inter"]
    return num / jnp.maximum(jnp.abs(den), jnp.exp(-gt["m_t"]))


def _branch_c_post(conv, c_z, cb, lng, lnb):
    y = conv + cb
    mu = jnp.mean(y, axis=-1, keepdims=True)
    yc = y - mu
    var = jnp.mean(yc * yc, axis=-1, keepdims=True)
    ln = (yc * lax.rsqrt(var + EPS)) * lng + lnb
    return _silu(c_z) * _silu(ln)


_GROUPS = ((0, 2048), (2048, 4096), (4096, 5632), (5632, N_MAIN))


def _mix_chunk(hb_ref, w_ref, wif_ref, or0, o_ref, tab, prm, scr, cdec):
    c2_ref, s2_ref, decay_ref, qdec_ref, kdec_ref = tab
    wa_ref, wc_ref, cb_ref, lng_ref, lnb_ref, rgn_ref, mgn_ref, bi_ref, bf_ref = prm
    exta, extc, shc, convc, s_scr, c_scr, n_scr, m_scr = scr
    orows = slice(or0, or0 + ROWS)
    projected = {}

    def pcol(name, lo=0, w=BRANCH_W):
        c0 = _COL[name]
        g0, g1 = next(g for g in _GROUPS if g[0] <= c0 < g[1])
        if g0 not in projected:
            projected[g0] = _bdot(hb_ref[...], w_ref[:, g0:g1])
        return projected[g0][:, c0 - g0 + lo:c0 - g0 + lo + w]

    ua = pcol("a_c") * pcol("a_u")
    exta[SUBLANES:SUBLANES + ROWS, :] = ua
    conv_a = (wa_ref[0:1, :] * exta[SUBLANES - 2:SUBLANES - 2 + ROWS, :]
              + wa_ref[1:2, :] * exta[SUBLANES - 1:SUBLANES - 1 + ROWS, :] + wa_ref[2:3, :] * ua)
    o_a = _silu(pcol("a_z")) * (pcol("a_b") * conv_a)
    o_ref[orows, 0:512] = o_a.astype(BF16)
    exta[0:SUBLANES, :] = exta[ROWS:ROWS + SUBLANES, :]

    uc = pcol("glu_v") * jax.nn.sigmoid(pcol("glu_g"))
    for lb in range(LANE_BLOCKS):
        extc[lb, 32:32 + ROWS, :] = uc[:, lb * 128:(lb + 1) * 128]
    span = ROWS + 32 - SUBLANES

    def conv_block(lb, carry):
        e = extc.at[lb]
        w = wc_ref.at[lb]
        for ph in range(1, SUBLANES):
            shc[ph - 1] = e[ph:ph + span, :]
        acc = None
        for ph in range(SUBLANES):
            for a in range(span // SUBLANES - ROWS // SUBLANES + 1):
                j = a * SUBLANES + ph - 2
                if 0 <= j < CONV_C_W:
                    arows = slice(a * SUBLANES, a * SUBLANES + ROWS)
                    shifted = e[arows, :] if ph == 0 else shc[ph - 1, arows, :]
                    term = w[j:j + 1, :] * shifted
                    acc = term if acc is None else acc + term
        convc[lb] = acc + w[CONV_C_W - 1:CONV_C_W, :] * e[32:32 + ROWS, :]
        e[0:32, :] = e[ROWS:ROWS + 32, :]
        return carry

    lax.fori_loop(0, LANE_BLOCKS, conv_block, 0)
    conv_c = jnp.concatenate([convc[lb] for lb in range(LANE_BLOCKS)], axis=1)
    o_c = _branch_c_post(conv_c, pcol("c_z"), cb_ref[...], lng_ref[...], lnb_ref[...])
    o_ref[orows, 1024:1536] = o_c.astype(BF16)

    c2 = c2_ref[orows, :]
    s2 = s2_ref[orows, :]
    for h in range(HEADS):
        lo = h * HEAD_D
        hs = slice(lo, lo + HEAD_D)
        q = _rotary(pcol("r_q", lo, HEAD_D), c2, s2)
        k = _rotary(pcol("r_k", lo, HEAD_D), c2, s2) * (HEAD_D ** -0.5)
        qb = q.astype(BF16)
        kb = k.astype(BF16)
        vb = pcol("r_v", lo, HEAD_D).astype(BF16)
        s_old = s_scr[h]
        sc = _bdot_nt(qb, kb) * decay_ref[h]
        ret = _bdot(sc.astype(BF16), vb) + _bdot(qb, s_old.astype(BF16)) * qdec_ref[:, hs]
        s_scr[h] = s_old * cdec[h] + _bdot((k * kdec_ref[:, hs]).T.astype(BF16), vb)
        o_r = _silu(pcol("r_z", lo, HEAD_D)) * (_head_norm(ret) * rgn_ref[:, hs])
        o_ref[orows, 512 + lo:512 + lo + HEAD_D] = o_r.astype(BF16)

    cmask, smask = _tile_masks(False)
    pif = _bdot(hb_ref[...], wif_ref[...])
    pre = _ml_prelims(pif[:, 0:128], pif[:, 128:256], bi_ref[...], bf_ref[...], cmask, smask)
    lane = lax.broadcasted_iota(jnp.int32, (ROWS, ROWS), 1)
    m_rows = m_scr[...]
    m_rows_new = jnp.zeros((ROWS, ROWS), F32)
    for h in range(HEADS):
        lo = h * HEAD_D
        hs = slice(lo, lo + HEAD_D)
        gt = _ml_gates(pre, h, m_rows[:, h:h + 1], cmask, smask)
        q = pcol("m_q", lo, HEAD_D)
        k = pcol("m_k", lo, HEAD_D) * (HEAD_D ** -0.5)
        qb = q.astype(BF16)
        kb = k.astype(BF16)
        vb = pcol("m_v", lo, HEAD_D).astype(BF16)
        c_old = c_scr[h]
        n_old = n_scr[:, hs]
        qn = jnp.sum(q * n_old, axis=1, keepdims=True)
        hout = _ml_output(gt, qb, kb, vb, _bdot(qb, c_old.astype(BF16)), qn)
        kw = k * gt["ws"]
        dec_full = jnp.broadcast_to(gt["dec"], (ROWS, ROWS))
        c_scr[h] = c_old * dec_full + _bdot(kw.T.astype(BF16), vb)
        n_scr[:, hs] = n_old * dec_full + _bdot(pre["bones"], kw.astype(BF16))
        m_rows_new = jnp.where(lane == h, jnp.broadcast_to(gt["m_new"], (ROWS, ROWS)), m_rows_new)
        hm = jax.nn.sigmoid(pcol("m_o", lo, HEAD_D)) * hout
        o_m = _silu(pcol("m_z", lo, HEAD_D)) * (_head_norm(hm) * mgn_ref[:, hs])
        o_ref[orows, 1536 + lo:1536 + lo + HEAD_D] = o_m.astype(BF16)
    m_scr[...] = m_rows_new


def _xmix_kernel(x_ref, g_ref, w_ref, wif_ref,
                 c2_ref, s2_ref, decay_ref, qdec_ref, kdec_ref,
                 wa_ref, wc_ref, cb_ref, lng_ref, lnb_ref, rgn_ref, mgn_ref, bi_ref, bf_ref,
                 o_ref, bufa_ref, sret_ref, bufc_ref, cml_ref, nml_ref, mml_ref,
                 hb_scr, exta, extc, shc, convc, s_scr, c_scr, n_scr, m_scr,
                 *, cdec, steps_per_seq):
    s = pl.program_id(0)
    seq_step = s % steps_per_seq
    tab = (c2_ref, s2_ref, decay_ref, qdec_ref, kdec_ref)
    prm = (wa_ref, wc_ref, cb_ref, lng_ref, lnb_ref, rgn_ref, mgn_ref, bi_ref, bf_ref)
    scr = (exta, extc, shc, convc, s_scr, c_scr, n_scr, m_scr)
    nchunk = STEP_ROWS // ROWS

    @pl.when(seq_step == 0)
    def _():
        exta[0:SUBLANES, :] = jnp.zeros((SUBLANES, BRANCH_W), F32)
        extc[:, 0:32, :] = jnp.zeros((LANE_BLOCKS, 32, 128), F32)
        s_scr[...] = jnp.zeros_like(s_scr)
        c_scr[...] = jnp.zeros_like(c_scr)
        n_scr[...] = jnp.zeros_like(n_scr)
        m_scr[...] = jnp.zeros_like(m_scr)

    for ci in range(nchunk):
        hb = hb_scr.at[ci % 2]
        hb[...] = _rmsnorm(x_ref[ci * ROWS:(ci + 1) * ROWS, :], g_ref[...]).astype(BF16)
        _mix_chunk(hb, w_ref, wif_ref, ci * ROWS, o_ref, tab, prm, scr, cdec)

    @pl.when(seq_step == steps_per_seq - 1)
    def _():
        bufa_ref[0] = exta[SUBLANES - (CONV_A_W - 1):SUBLANES, :]
        for lb in range(LANE_BLOCKS):
            bufc_ref[0, :, lb * 128:(lb + 1) * 128] = extc[lb, 32 - (CONV_C_W - 1):32, :]
        for h in range(HEADS):
            sret_ref[0, h] = s_scr[h]
            cml_ref[0, h] = c_scr[h]
            nml_ref[0, h:h + 1, :] = n_scr[0:1, h * HEAD_D:(h + 1) * HEAD_D]
        mml_ref[0] = m_scr[0:1, :]


def _xmix_prompt(x, gnorm, w_all, wif, layer, consts, prm, batch, seq):
    steps_per_seq = seq // STEP_ROWS
    nstep = batch * steps_per_seq
    c1 = lambda shape: _resident(shape, lambda s: (0,) * len(shape))
    rope = pl.BlockSpec((STEP_ROWS, HEAD_D), lambda s: (s % steps_per_seq, 0))
    in_specs = [
        pl.BlockSpec((STEP_ROWS, D_MODEL), lambda s: (s, 0)),
        c1((1, D_MODEL)),
        _resident((None, D_MODEL, W_BLOCK), lambda s: (layer, 0, 0)),
        c1((D_MODEL, N_IF)),
        rope, rope,
        c1((HEADS, ROWS, ROWS)),
        c1((ROWS, BRANCH_W)),
        c1((ROWS, BRANCH_W)),
        c1((CONV_A_W, BRANCH_W)),
        c1((LANE_BLOCKS, CONV_C_W, 128)),
    ] + [c1((1, BRANCH_W))] * 5 + [c1((1, 128))] * 2
    per_b3 = lambda s: (s // steps_per_seq, 0, 0)
    per_b4 = lambda s: (s // steps_per_seq, 0, 0, 0)
    out_specs = [
        pl.BlockSpec((STEP_ROWS, N_BRANCH * BRANCH_W), lambda s: (s, 0)),
        pl.BlockSpec((1, CONV_A_W - 1, BRANCH_W), per_b3),
        pl.BlockSpec((1, HEADS, HEAD_D, HEAD_D), per_b4),
        pl.BlockSpec((1, CONV_C_W - 1, BRANCH_W), per_b3),
        pl.BlockSpec((1, HEADS, HEAD_D, HEAD_D), per_b4),
        pl.BlockSpec((1, HEADS, HEAD_D), per_b3),
        pl.BlockSpec((1, 1, 128), per_b3),
    ]
    out_shape = [
        jax.ShapeDtypeStruct((batch * seq, N_BRANCH * BRANCH_W), BF16),
        jax.ShapeDtypeStruct((batch, CONV_A_W - 1, BRANCH_W), F32),
        jax.ShapeDtypeStruct((batch, HEADS, HEAD_D, HEAD_D), F32),
        jax.ShapeDtypeStruct((batch, CONV_C_W - 1, BRANCH_W), F32),
        jax.ShapeDtypeStruct((batch, HEADS, HEAD_D, HEAD_D), F32),
        jax.ShapeDtypeStruct((batch, HEADS, HEAD_D), F32),
        jax.ShapeDtypeStruct((batch, 1, 128), F32),
    ]
    scratch = [
        pltpu.VMEM((2, ROWS, D_MODEL), BF16),
        pltpu.VMEM((ROWS + SUBLANES, BRANCH_W), F32),
        pltpu.VMEM((LANE_BLOCKS, ROWS + 32, 128), F32),
        pltpu.VMEM((SUBLANES - 1, ROWS + 32 - SUBLANES, 128), F32),
        pltpu.VMEM((LANE_BLOCKS, ROWS, 128), F32),
        pltpu.VMEM((HEADS, HEAD_D, HEAD_D), F32),
        pltpu.VMEM((HEADS, HEAD_D, HEAD_D), F32),
        pltpu.VMEM((ROWS, BRANCH_W), F32),
        pltpu.VMEM((ROWS, ROWS), F32),
    ]
    return pl.pallas_call(
        functools.partial(_xmix_kernel, cdec=consts["cdec"], steps_per_seq=steps_per_seq),
        grid=(nstep,),
        in_specs=in_specs, out_specs=out_specs, out_shape=out_shape, scratch_shapes=scratch,
        compiler_params=pltpu.CompilerParams(
            dimension_semantics=("arbitrary",), vmem_limit_bytes=VMEM_LIMIT),
        name="xmix_prompt",
    )(x, gnorm, w_all, wif, consts["c2"], consts["s2"], consts["decay"], consts["qdec"],
      consts["kdec"], prm["wa"], prm["wc_blocks"], prm["cb"], prm["lng"], prm["lnb"], prm["rgn"],
      prm["mgn"], prm["bi"], prm["bf"])


def _mix_sample_kernel(pa_ref, pc_ref, rq_ref, rk_ref, rv_ref, rz_ref, mq_ref, mk_ref, mv_ref,
                       mo_ref, mz_ref, pif_ref, c2_ref, s2_ref, decay_ref, qdec_ref, kdec_ref,
                       cdec_ref, wa_ref, wc_ref, cb_ref, lng_ref, lnb_ref, rgn_ref, mgn_ref,
                       bi_ref, bf_ref, bufa_ref, sret_ref, bufc_ref, cml_ref, nrows_ref, mrows_ref,
                       o_ref, bufa_o, sret_o, bufc_o, cml_o, nrows_o, mrows_o, dec_scr):
    h = pl.program_id(1)
    sq = SAMPLE_SEQS
    nt = ROWS // sq

    @pl.when(h == 0)
    def _():
        ua = pa_ref[:, 512:1024] * pa_ref[:, 1024:1536]
        ea = [bufa_ref[:, 0, :], bufa_ref[:, 1, :]] + [ua[t * sq:(t + 1) * sq, :] for t in range(nt)]
        conv_a = jnp.concatenate(
            [wa_ref[0:1, :] * ea[t] + wa_ref[1:2, :] * ea[t + 1] + wa_ref[2:3, :] * ea[t + 2]
             for t in range(nt)], axis=0)
        o_ref[:, 0:512] = (_silu(pa_ref[:, 1536:2048]) * (pa_ref[:, 0:512] * conv_a)).astype(BF16)
        bufa_o[:, 0, :] = ea[nt]
        bufa_o[:, 1, :] = ea[nt + 1]
        uc = pc_ref[:, 0:512] * jax.nn.sigmoid(pc_ref[:, 512:1024])
        nb = CONV_C_W - 1
        ec = [bufc_ref[:, i, :] for i in range(nb)] + [uc[t * sq:(t + 1) * sq, :] for t in range(nt)]
        planes = []
        for t in range(nt):
            acc = wc_ref[0:1, :] * ec[t]
            for j in range(1, CONV_C_W):
                acc = acc + wc_ref[j:j + 1, :] * ec[t + j]
            planes.append(acc)
        conv_c = jnp.concatenate(planes, axis=0)
        o_ref[:, 1024:1536] = _branch_c_post(conv_c, pc_ref[:, 1024:1536], cb_ref[...], lng_ref[...],
                                             lnb_ref[...]).astype(BF16)
        for i in range(nb):
            bufc_o[:, i, :] = ec[i + nt]

    cmask, smask = _tile_masks(True)
    rowseq = lax.broadcasted_iota(jnp.int32, (ROWS, ROWS), 0) & (sq - 1)
    laneseq = lax.broadcasted_iota(jnp.int32, (ROWS, ROWS), 1) & (sq - 1)

    c2 = c2_ref[...]
    s2 = s2_ref[...]
    rq = _rotary(rq_ref[...], c2, s2)
    rk = _rotary(rk_ref[...], c2, s2) * (HEAD_D ** -0.5)
    rqb = rq.astype(BF16)
    rkb = rk.astype(BF16)
    rvb = rv_ref[...].astype(BF16)
    rkd_t = (rk * kdec_ref[...]).T
    cdec = cdec_ref[0, 0:1, :]

    pre = _ml_prelims(pif_ref[:, 0:128], pif_ref[:, 128:256], bi_ref[...], bf_ref[...], cmask, smask)
    m_prev = mrows_ref[...]
    gt = _ml_gates(pre, 0, m_prev[:, 0:1], cmask, smask)
    mq = mq_ref[...]
    mk = mk_ref[...] * (HEAD_D ** -0.5)
    mqb = mq.astype(BF16)
    mkb = mk.astype(BF16)
    mvb = mv_ref[...].astype(BF16)
    kw = mk * gt["ws"]
    kw_t = kw.T
    dec_full = jnp.broadcast_to(gt["dec"], (ROWS, ROWS))
    dec_scr[...] = dec_full

    def per_seq(b, carry):
        inter_r, inter_m = carry
        sb = sret_ref[b, 0]
        inter_r = jnp.where(rowseq == b, _bdot(rqb, sb.astype(BF16)), inter_r)
        k_sel = jnp.where(laneseq == b, rkd_t, 0.0).astype(BF16)
        sret_o[b, 0] = sb * cdec + _bdot(k_sel, rvb)
        cb_ = cml_ref[b, 0]
        inter_m = jnp.where(rowseq == b, _bdot(mqb, cb_.astype(BF16)), inter_m)
        kw_sel = jnp.where(laneseq == b, kw_t, 0.0).astype(BF16)
        cml_o[b, 0] = cb_ * dec_scr[pl.ds(b, 1), :] + _bdot(kw_sel, mvb)
        return inter_r, inter_m

    zero = jnp.zeros((ROWS, HEAD_D), F32)
    inter_r, inter_m = lax.fori_loop(0, sq, per_seq, (zero, zero), unroll=8)

    sc = _bdot_nt(rqb, rkb) * decay_ref[0]
    ret = _bdot(sc.astype(BF16), rvb) + inter_r * qdec_ref[...]
    o_r = (_silu(rz_ref[...]) * (_head_norm(ret) * rgn_ref[...])).astype(BF16)

    n_old = nrows_ref[...]
    qn = jnp.sum(mq * n_old, axis=1, keepdims=True)
    hout = _ml_output(gt, mqb, mkb, mvb, inter_m, qn)
    nrows_o[...] = n_old * dec_full + _bdot(pre["bones"], kw.astype(BF16))
    mrows_o[...] = jnp.broadcast_to(gt["m_new"], (ROWS, ROWS))
    hm = jax.nn.sigmoid(mo_ref[...]) * hout
    o_m = (_silu(mz_ref[...]) * (_head_norm(hm) * mgn_ref[...])).astype(BF16)

    for hh in range(HEADS):
        @pl.when(h == hh)
        def _(hh=hh):
            o_ref[:, 512 + hh * HEAD_D:512 + (hh + 1) * HEAD_D] = o_r
            o_ref[:, 1536 + hh * HEAD_D:1536 + (hh + 1) * HEAD_D] = o_m


def _mix_sample(p, pif, consts, prm, st, nseq, layer, depth, stacked):
    ntile = nseq // SAMPLE_SEQS
    sq = SAMPLE_SEQS
    head_cols = lambda name: pl.BlockSpec(
        (ROWS, HEAD_D), lambda i, h, _o=_COL[name] // HEAD_D: (i, _o + h))
    const2 = lambda i, h: (0, 0)
    per_head2 = lambda i, h: (0, h)
    conv_a_state = pl.BlockSpec((None, sq, CONV_A_W - 1, BRANCH_W), lambda i, h: (layer, i, 0, 0))
    conv_c_state = pl.BlockSpec((None, sq, CONV_C_W - 1, BRANCH_W), lambda i, h: (layer, i, 0, 0))
    matrix_state = pl.BlockSpec((None, sq, 1, HEAD_D, HEAD_D), lambda i, h: (layer, i, h, 0, 0))
    in_specs = [
        pl.BlockSpec((ROWS, 2048), lambda i, h: (i, 0)),
        pl.BlockSpec((ROWS, 2048), lambda i, h: (i, 2)),
    ] + [head_cols(n) for n in ("r_q", "r_k", "r_v", "r_z", "m_q", "m_k", "m_v", "m_o", "m_z")] + [
        pl.BlockSpec((ROWS, 256), lambda i, h: (i, h)),
        pl.BlockSpec((ROWS, HEAD_D), const2),
        pl.BlockSpec((ROWS, HEAD_D), const2),
        pl.BlockSpec((1, ROWS, ROWS), lambda i, h: (h, 0, 0)),
        pl.BlockSpec((ROWS, HEAD_D), per_head2),
        pl.BlockSpec((ROWS, HEAD_D), per_head2),
        pl.BlockSpec((1, 8, 128), lambda i, h: (h, 0, 0)),
        pl.BlockSpec((CONV_A_W, BRANCH_W), const2),
        pl.BlockSpec((CONV_C_W, BRANCH_W), const2),
        pl.BlockSpec((1, BRANCH_W), const2),
        pl.BlockSpec((1, BRANCH_W), const2),
        pl.BlockSpec((1, BRANCH_W), const2),
        pl.BlockSpec((1, HEAD_D), per_head2),
        pl.BlockSpec((1, HEAD_D), per_head2),
        pl.BlockSpec((1, 128), per_head2),
        pl.BlockSpec((1, 128), per_head2),
        conv_a_state, matrix_state, conv_c_state, matrix_state,
        pl.BlockSpec((ROWS, HEAD_D), lambda i, h: (i, h)),
        pl.BlockSpec((ROWS, HEAD_D), lambda i, h: (i, h)),
    ]
    out_specs = [
        pl.BlockSpec((ROWS, N_BRANCH * BRANCH_W), lambda i, h: (i, 0)),
        conv_a_state, matrix_state, conv_c_state, matrix_state,
        pl.BlockSpec((ROWS, HEAD_D), lambda i, h: (i, h)),
        pl.BlockSpec((ROWS, HEAD_D), lambda i, h: (i, h)),
    ]
    nrow = ntile * ROWS
    out_shape = [
        jax.ShapeDtypeStruct((nrow, N_BRANCH * BRANCH_W), BF16),
        jax.ShapeDtypeStruct((depth, nseq, CONV_A_W - 1, BRANCH_W), F32),
        jax.ShapeDtypeStruct((depth, nseq, HEADS, HEAD_D, HEAD_D), F32),
        jax.ShapeDtypeStruct((depth, nseq, CONV_C_W - 1, BRANCH_W), F32),
        jax.ShapeDtypeStruct((depth, nseq, HEADS, HEAD_D, HEAD_D), F32),
        jax.ShapeDtypeStruct((nrow, HEADS * HEAD_D), F32),
        jax.ShapeDtypeStruct((nrow, HEADS * HEAD_D), F32),
    ]
    args = [p] * 11 + [
        pif, consts["c2"], consts["s2"], consts["decay"], consts["qdec"], consts["kdec"],
        consts["cdec"], prm["wa"], prm["wc"], prm["cb"], prm["lng"], prm["lnb"], prm["rgn"],
        prm["mgn"], prm["bi_h"], prm["bf_h"],
        st["bufa"], st["sret"], st["bufc"], st["cml"], st["nrows"], st["mrows"]]
    assert len(args) == len(in_specs)
    n_in = len(args)
    aliases = {}
    if stacked is not None:
        aliases = {n_in + k: 1 + k for k in range(4)}
        in_specs = in_specs + [pl.BlockSpec(memory_space=pl.ANY)] * 4
        args = args + list(stacked)

    def entry(*refs):
        _mix_sample_kernel(*refs[:n_in], *refs[n_in + len(aliases):])

    return pl.pallas_call(
        entry,
        grid=(ntile, HEADS),
        in_specs=in_specs, out_specs=out_specs, out_shape=out_shape,
        scratch_shapes=[pltpu.VMEM((ROWS, ROWS), F32)],
        input_output_aliases=aliases,
        compiler_params=pltpu.CompilerParams(
            dimension_semantics=("arbitrary", "arbitrary"), vmem_limit_bytes=VMEM_LIMIT),
        name="mix_sample",
    )(*args)


def _rope_tables(pos):
    inv = ROPE_BASE ** (-jnp.arange(0, HEAD_D, 2, dtype=F32) / HEAD_D)
    ang = pos.astype(F32)[:, None] * inv[None, :]
    cos = jnp.cos(ang)
    sin = jnp.sin(ang)
    return jnp.concatenate([cos, cos], axis=1), jnp.concatenate([-sin, sin], axis=1)


def _decay_tables(t_of_row, seq_of_row, chunk_len):
    log_g = np.log1p(-(2.0 ** (-5.0 - np.arange(HEADS, dtype=np.float64))))
    dt = t_of_row[:, None] - t_of_row[None, :]
    ok = (seq_of_row[:, None] == seq_of_row[None, :]) & (dt >= 0)
    decay = np.where(ok[None], np.exp(np.maximum(dt, 0)[None] * log_g[:, None, None]), 0.0)
    qdec = np.exp((t_of_row[:, None] + 1.0) * log_g[None, :])
    kdec = np.exp((chunk_len - 1.0 - t_of_row)[:, None] * log_g[None, :])
    cdec = np.exp(chunk_len * log_g)
    rep = lambda a: jnp.asarray(np.repeat(a, HEAD_D, axis=1), F32)
    return jnp.asarray(decay, F32), rep(qdec), rep(kdec), cdec


def kernel(x_prompt, x_sample, state_conv_a, state_ret, state_conv_c, state_mlstm_C,
           state_mlstm_n, state_mlstm_m, norm_g, w_in, conv_a_w, conv_c_w, conv_c_b,
           ln_c_g, ln_c_b, ret_gn_g, ml_gn_g, ml_gate_b, w_br, w_out, final_g):
    bp, tp, _ = x_prompt.shape
    bs, ts, _ = x_sample.shape
    depth = w_in.shape[0]
    past_len = 16384
    sq = SAMPLE_SEQS
    ntile = bs // sq
    assert tp % STEP_ROWS == 0 and ts * sq == ROWS and bs % sq == 0

    r = np.arange(ROWS)
    c2p, s2p = _rope_tables(jnp.arange(tp, dtype=jnp.int32))
    decay_p, qdec_p, kdec_p, cdec_p = _decay_tables(r.astype(np.float64), np.zeros(ROWS), float(ROWS))
    consts_p = dict(c2=c2p, s2=s2p, decay=decay_p, qdec=qdec_p, kdec=kdec_p,
                    cdec=tuple(float(v) for v in cdec_p))
    t_s = (r // sq).astype(np.float64)
    c2s, s2s = _rope_tables(past_len + jnp.asarray(r // sq, jnp.int32))
    decay_s, qdec_s, kdec_s, cdec_s = _decay_tables(t_s, r % sq, float(ts))
    cdec_s_arr = jnp.asarray(np.broadcast_to(cdec_s[:, None, None], (HEADS, 8, 128)), F32)
    consts_s = dict(c2=c2s, s2=s2s, decay=decay_s, qdec=qdec_s, kdec=kdec_s, cdec=cdec_s_arr)

    xp = x_prompt.reshape(bp * tp, D_MODEL)
    xs = x_sample.reshape(ntile, sq, ts, D_MODEL).transpose(0, 2, 1, 3).reshape(bs * ts, D_MODEL)

    fg = final_g.reshape(1, D_MODEL)
    outs_p = [[] for _ in range(6)]
    n_s, m_s = [], []
    stacked = None
    sel_p = np.zeros((2 * HEADS, N_IF), np.float32)
    sel_s = np.zeros((2 * HEADS, HEADS * N_IF), np.float32)
    for h in range(HEADS):
        sel_p[h, h] = 1.0
        sel_p[HEADS + h, 128 + h] = 1.0
        sel_s[h, h * N_IF] = 1.0
        sel_s[HEADS + h, h * N_IF + 128] = 1.0
    w_bf = w_in.astype(BF16)
    pad_row = lambda v: jnp.pad(v, (0, 128 - v.shape[0])).reshape(1, 128)
    for l in range(depth):
        w_if = w_in[l, :, N_MAIN:N_MAIN + 2 * HEADS]
        spread = lambda sel: jnp.dot(w_if, jnp.asarray(sel), precision=lax.Precision.HIGHEST).astype(BF16)
        w_gate = w_bf[l, :, N_MAIN + 2 * HEADS:]
        gb = ml_gate_b[l]
        prm = dict(
            wa=conv_a_w[l], wc=conv_c_w[l], cb=conv_c_b[l].reshape(1, -1),
            wc_blocks=conv_c_w[l].reshape(CONV_C_W, LANE_BLOCKS, 128).transpose(1, 0, 2),
            lng=ln_c_g[l].reshape(1, -1), lnb=ln_c_b[l].reshape(1, -1),
            rgn=ret_gn_g[l].reshape(1, -1), mgn=ml_gn_g[l].reshape(1, -1),
            bi=pad_row(gb[:HEADS]), bf=pad_row(gb[HEADS:]),
            bi_h=jnp.concatenate([pad_row(gb[h:h + 1]) for h in range(HEADS)], axis=1),
            bf_h=jnp.concatenate([pad_row(gb[HEADS + h:HEADS + h + 1]) for h in range(HEADS)], axis=1),
        )
        gnorm = norm_g[l].reshape(1, D_MODEL)
        wbr = w_br[l].astype(BF16)
        wout = w_out[l].astype(BF16)
        final = l == depth - 1

        res = _xmix_prompt(xp, gnorm, w_bf, spread(sel_p), l, consts_p, prm, bp, tp)
        for k in range(5):
            outs_p[k].append(res[1 + k])
        outs_p[5].append(res[6][:, 0, :HEADS])
        xp = _merge(xp, gnorm, res[0], w_gate, wbr, wout, fg, final, tm=512)

        ps, pifs = _inproj(xs, gnorm, w_bf, spread(sel_s), l, tm=bs * ts, tn=2048)
        n_rows = jnp.broadcast_to(
            state_mlstm_n[l].reshape(ntile, 1, sq, HEADS * HEAD_D),
            (ntile, ts, sq, HEADS * HEAD_D)).reshape(bs * ts, HEADS * HEAD_D)
        m_rows = jnp.broadcast_to(
            state_mlstm_m[l].reshape(ntile, 1, sq, HEADS, 1),
            (ntile, ts, sq, HEADS, HEAD_D)).reshape(bs * ts, HEADS * HEAD_D)
        st = dict(bufa=state_conv_a, sret=state_ret, bufc=state_conv_c, cml=state_mlstm_C,
                  nrows=n_rows, mrows=m_rows)
        res = _mix_sample(ps, pifs, consts_s, prm, st, bs, l, depth, stacked)
        stacked = tuple(res[1:5])
        n_s.append(res[5].reshape(ntile, ts, sq, HEADS, HEAD_D)[:, 0].reshape(bs, HEADS, HEAD_D))
        m_s.append(res[6].reshape(ntile, ts, sq, HEADS, HEAD_D)[:, 0, :, :, 0].reshape(bs, HEADS))
        xs = _merge(xs, gnorm, res[0], w_gate, wbr, wout, fg, final, tm=bs * ts)

    y_prompt = xp.reshape(bp, tp, D_MODEL)
    y_sample = xs.reshape(ntile, ts, sq, D_MODEL).transpose(0, 2, 1, 3).reshape(bs, ts, D_MODEL)
    sp = [jnp.stack(a, axis=0) for a in outs_p]
    return (y_prompt, y_sample, sp[0], stacked[0], sp[1], stacked[1], sp[2], stacked[2],
            sp[3], stacked[3], sp[4], jnp.stack(n_s, axis=0), sp[5], jnp.stack(m_s, axis=0))
```

```python
import functools

import numpy as np
import jax
import jax.numpy as jnp
from jax import lax
from jax.experimental import pallas as pl
from jax.experimental.pallas import tpu as pltpu

F32 = jnp.float32
BF16 = jnp.bfloat16

D_MODEL = 1024
BRANCH_W = 512
N_BRANCH = 4
HEADS = 4
HEAD_D = 128
CONV_A_W = 3
CONV_C_W = 31
ROPE_BASE = 10000.0
EPS = 1e-6
N_MAIN = 16 * BRANCH_W
N_IF = 256
N_GATE = N_BRANCH * D_MODEL
ROWS = 128
STEP_ROWS = 4 * ROWS
W_BLOCK = N_MAIN + 512
SAMPLE_SEQS = 32
SUBLANES = 8
VMEM_LIMIT = 56 * 1024 * 1024

_COL = dict(a_b=0, a_c=512, a_u=1024, a_z=1536, r_q=2048, r_k=2560, r_v=3072, r_z=3584,
            glu_v=4096, glu_g=4608, c_z=5120, m_q=5632, m_k=6144, m_v=6656, m_o=7168, m_z=7680)


def _bdot(a, b):
    return jnp.dot(a, b, preferred_element_type=F32)


def _bdot_nt(a, b):
    return lax.dot_general(a, b, (((1,), (1,)), ((), ())), preferred_element_type=F32)


def _exact_mm(mat_b16, x):
    hi = x.astype(BF16)
    r1 = x - hi.astype(F32)
    mid = r1.astype(BF16)
    lo = (r1 - mid.astype(F32)).astype(BF16)
    return _bdot(mat_b16, hi) + _bdot(mat_b16, mid) + _bdot(mat_b16, lo)


def _silu(x):
    return x * jax.nn.sigmoid(x)


def _log_sigmoid(x):
    return jnp.minimum(x, 0.0) - jnp.log1p(jnp.exp(-jnp.abs(x)))


def _head_norm(y):
    mu = jnp.mean(y, axis=-1, keepdims=True)
    yc = y - mu
    var = jnp.mean(yc * yc, axis=-1, keepdims=True)
    return yc * lax.rsqrt(var + EPS)


def _rotary(x, c2, s2):
    return x * c2 + pltpu.roll(x, HEAD_D // 2, 1) * s2


def _rmsnorm(x, g):
    ms = jnp.mean(x * x, axis=-1, keepdims=True)
    return (x * lax.rsqrt(ms + EPS)) * g


def _resident(shape, imap):
    return pl.BlockSpec(shape, imap, pipeline_mode=pl.Buffered(1))


def _inproj_kernel(x_ref, g_ref, w_ref, wif_ref, p_ref, pif_ref, hb_ref):
    @pl.when(pl.program_id(1) == 0)
    def _():
        hb = _rmsnorm(x_ref[...], g_ref[...]).astype(BF16)
        hb_ref[...] = hb
        pif_ref[...] = _bdot(hb, wif_ref[...])

    p_ref[...] = _bdot(hb_ref[...], w_ref[...])


def _inproj(x, g, w_all, wif, layer, tm, tn):
    m = x.shape[0]
    nif = wif.shape[1]
    return pl.pallas_call(
        _inproj_kernel,
        grid=(m // tm, N_MAIN // tn),
        in_specs=[
            pl.BlockSpec((tm, D_MODEL), lambda i, j: (i, 0)),
            pl.BlockSpec((1, D_MODEL), lambda i, j: (0, 0)),
            pl.BlockSpec((None, D_MODEL, tn), lambda i, j: (layer, 0, j)),
            pl.BlockSpec((D_MODEL, nif), lambda i, j: (0, 0)),
        ],
        out_specs=[
            pl.BlockSpec((tm, tn), lambda i, j: (i, j)),
            pl.BlockSpec((tm, nif), lambda i, j: (i, 0)),
        ],
        out_shape=[jax.ShapeDtypeStruct((m, N_MAIN), F32), jax.ShapeDtypeStruct((m, nif), F32)],
        scratch_shapes=[pltpu.VMEM((tm, D_MODEL), BF16)],
        compiler_params=pltpu.CompilerParams(
            dimension_semantics=("arbitrary", "arbitrary"), vmem_limit_bytes=VMEM_LIMIT),
        name="inproj",
    )(x, g, w_all, wif)


def _merge_kernel(x_ref, g_ref, o_ref, wg_ref, wbr_ref, wout_ref, fg_ref, y_ref, hb_ref, *, final):
    x = x_ref[...]
    hb_ref[...] = _rmsnorm(x, g_ref[...]).astype(BF16)
    acc = None
    for n in range(N_BRANCH):
        proj = _bdot(o_ref[:, n * BRANCH_W:(n + 1) * BRANCH_W], wbr_ref[n])
        gate = jax.nn.sigmoid(_bdot(hb_ref[...], wg_ref[:, n * D_MODEL:(n + 1) * D_MODEL]))
        term = gate * proj
        acc = term if acc is None else acc + term
    y = x + _bdot(acc.astype(BF16), wout_ref[...])
    if final:
        y = _rmsnorm(y, fg_ref[...])
    y_ref[...] = y


def _merge(x, g, o, wg, wbr, wout, fg, final, tm):
    m = x.shape[0]
    resident = lambda shape: _resident(shape, lambda i: (0,) * len(shape))
    return pl.pallas_call(
        functools.partial(_merge_kernel, final=final),
        grid=(m // tm,),
        in_specs=[
            pl.BlockSpec((tm, D_MODEL), lambda i: (i, 0)),
            resident((1, D_MODEL)),
            pl.BlockSpec((tm, N_BRANCH * BRANCH_W), lambda i: (i, 0)),
            resident((D_MODEL, N_GATE)),
            resident((N_BRANCH, BRANCH_W, D_MODEL)),
            resident((D_MODEL, D_MODEL)),
            resident((1, D_MODEL)),
        ],
        out_specs=pl.BlockSpec((tm, D_MODEL), lambda i: (i, 0)),
        out_shape=jax.ShapeDtypeStruct((m, D_MODEL), F32),
        scratch_shapes=[pltpu.VMEM((tm, D_MODEL), BF16)],
        compiler_params=pltpu.CompilerParams(
            dimension_semantics=("arbitrary",), vmem_limit_bytes=VMEM_LIMIT),
        name="merge",
    )(x, g, o, wg, wbr, wout, fg)


def _tile_masks(sample):
    row = lax.broadcasted_iota(jnp.int32, (ROWS, ROWS), 0)
    col = lax.broadcasted_iota(jnp.int32, (ROWS, ROWS), 1)
    if not sample:
        return row >= col, None
    same = (row & (SAMPLE_SEQS - 1)) == (col & (SAMPLE_SEQS - 1))
    return jnp.logical_and(same, row >= col), same


def _ml_prelims(pi, pf, bi, bfo, cmask, smask):
    ai = pi + bi
    lf = _log_sigmoid(pf + bfo)
    tri = jnp.where(cmask, 1.0, 0.0).astype(BF16)
    if smask is None:
        bones = jnp.ones((ROWS, ROWS), BF16)
    else:
        bones = jnp.where(smask, 1.0, 0.0).astype(BF16)
    fcum = _exact_mm(tri, lf)
    ftot = _exact_mm(bones, lf)
    g = (ftot - fcum) + ai
    return dict(ai_t=ai.T, fcum=fcum, fcum_t=fcum.T, ftot=ftot, g=g, g_t=g.T, bones=bones)


def _ml_gates(pre, col, m_prev_c, cmask, smask):
    f_c = pre["fcum"][:, col:col + 1]
    f_r = pre["fcum_t"][col:col + 1, :]
    ig_r = pre["ai_t"][col:col + 1, :]
    g_c = pre["g"][:, col:col + 1]
    g_r = pre["g_t"][col:col + 1, :]
    ftot_c = pre["ftot"][:, col:col + 1]
    logw = jnp.where(cmask, (f_c - f_r) + ig_r, -jnp.inf)
    b = f_c + m_prev_c
    m_t = jnp.maximum(jnp.max(logw, axis=1, keepdims=True), b)
    wts = jnp.exp(logw - m_t)
    inter = jnp.exp(b - m_t)
    if smask is None:
        gmax = jnp.max(g_r, axis=1, keepdims=True)
    else:
        gmax = jnp.max(jnp.where(smask, g_r, -jnp.inf), axis=1, keepdims=True)
    m_new = jnp.maximum(ftot_c + m_prev_c, gmax)
    ws = jnp.exp(g_c - m_new)
    dec = jnp.exp((ftot_c + m_prev_c) - m_new)
    return dict(m_t=m_t, wts=wts, inter=inter, m_new=m_new, ws=ws, dec=dec)


def _ml_output(gt, qb, kb, vb, inter_term, qn):
    s = _bdot_nt(qb, kb) * gt["wts"]
    num = _bdot(s.astype(BF16), vb) + inter_term * gt["inter"]
    den = jnp.sum(s, axis=1, keepdims=True) + qn * gt["inter"]
    return num / jnp.maximum(jnp.abs(den), jnp.exp(-gt["m_t"]))


def _branch_c_post(conv, c_z, cb, lng, lnb):
    y = conv + cb
    mu = jnp.mean(y, axis=-1, keepdims=True)
    yc = y - mu
    var = jnp.mean(yc * yc, axis=-1, keepdims=True)
    ln = (yc * lax.rsqrt(var + EPS)) * lng + lnb
    return _silu(c_z) * _silu(ln)


def _mix_chunk(hb_ref, w_ref, wif_ref, or0, o_ref, tab, prm, scr, cdec):
    c2_ref, s2_ref, decay_ref, qdec_ref, kdec_ref = tab
    wa_ref, wc_ref, cb_ref, lng_ref, lnb_ref, rgn_ref, mgn_ref, bi_ref, bf_ref = prm
    exta, extc, shc, s_scr, c_scr, n_scr, m_scr = scr
    orows = slice(or0, or0 + ROWS)
    projected = {}

    def pcol(name, lo=0, w=BRANCH_W):
        c0 = _COL[name]
        if name not in projected:
            projected[name] = _bdot(hb_ref[...], w_ref[:, c0:c0 + BRANCH_W])
        return projected[name][:, lo:lo + w]

    ua = pcol("a_c") * pcol("a_u")
    exta[SUBLANES:SUBLANES + ROWS, :] = ua
    conv_a = (wa_ref[0:1, :] * exta[SUBLANES - 2:SUBLANES - 2 + ROWS, :]
              + wa_ref[1:2, :] * exta[SUBLANES - 1:SUBLANES - 1 + ROWS, :] + wa_ref[2:3, :] * ua)
    o_a = _silu(pcol("a_z")) * (pcol("a_b") * conv_a)
    o_ref[orows, 0:512] = o_a.astype(BF16)
    exta[0:SUBLANES, :] = exta[ROWS:ROWS + SUBLANES, :]

    extc[32:32 + ROWS, :] = pcol("glu_v") * jax.nn.sigmoid(pcol("glu_g"))
    span = ROWS + 32 - SUBLANES
    conv_blocks = []
    for lb in range(BRANCH_W // 128):
        lanes = slice(lb * 128, (lb + 1) * 128)
        sh = shc.at[lb % 2]
        for ph in range(1, SUBLANES):
            sh[ph - 1] = extc[ph:ph + span, lanes]
        acc = None
        for ph in range(SUBLANES):
            for a in range(span // SUBLANES - ROWS // SUBLANES + 1):
                j = a * SUBLANES + ph - 2
                if 0 <= j < CONV_C_W:
                    arows = slice(a * SUBLANES, a * SUBLANES + ROWS)
                    shifted = extc[arows, lanes] if ph == 0 else sh[ph - 1, arows, :]
                    term = wc_ref[j:j + 1, lanes] * shifted
                    acc = term if acc is None else acc + term
        conv_blocks.append(acc + wc_ref[CONV_C_W - 1:CONV_C_W, lanes] * extc[32:32 + ROWS, lanes])
    conv_c = jnp.concatenate(conv_blocks, axis=1)
    o_c = _branch_c_post(conv_c, pcol("c_z"), cb_ref[...], lng_ref[...], lnb_ref[...])
    o_ref[orows, 1024:1536] = o_c.astype(BF16)
    extc[0:32, :] = extc[ROWS:ROWS + 32, :]

    c2 = c2_ref[orows, :]
    s2 = s2_ref[orows, :]
    for h in range(HEADS):
        lo = h * HEAD_D
        hs = slice(lo, lo + HEAD_D)
        q = _rotary(pcol("r_q", lo, HEAD_D), c2, s2)
        k = _rotary(pcol("r_k", lo, HEAD_D), c2, s2) * (HEAD_D ** -0.5)
        qb = q.astype(BF16)
        kb = k.astype(BF16)
        vb = pcol("r_v", lo, HEAD_D).astype(BF16)
        s_old = s_scr[h]
        sc = _bdot_nt(qb, kb) * decay_ref[h]
        ret = _bdot(sc.astype(BF16), vb) + _bdot(qb, s_old.astype(BF16)) * qdec_ref[:, hs]
        s_scr[h] = s_old * cdec[h] + _bdot((k * kdec_ref[:, hs]).T.astype(BF16), vb)
        o_r = _silu(pcol("r_z", lo, HEAD_D)) * (_head_norm(ret) * rgn_ref[:, hs])
        o_ref[orows, 512 + lo:512 + lo + HEAD_D] = o_r.astype(BF16)

    cmask, smask = _tile_masks(False)
    pif = _bdot(hb_ref[...], wif_ref[...])
    pre = _ml_prelims(pif[:, 0:128], pif[:, 128:256], bi_ref[...], bf_ref[...], cmask, smask)
    lane = lax.broadcasted_iota(jnp.int32, (ROWS, ROWS), 1)
    m_rows = m_scr[...]
    m_rows_new = jnp.zeros((ROWS, ROWS), F32)
    for h in range(HEADS):
        lo = h * HEAD_D
        hs = slice(lo, lo + HEAD_D)
        gt = _ml_gates(pre, h, m_rows[:, h:h + 1], cmask, smask)
        q = pcol("m_q", lo, HEAD_D)
        k = pcol("m_k", lo, HEAD_D) * (HEAD_D ** -0.5)
        qb = q.astype(BF16)
        kb = k.astype(BF16)
        vb = pcol("m_v", lo, HEAD_D).astype(BF16)
        c_old = c_scr[h]
        n_old = n_scr[:, hs]
        qn = jnp.sum(q * n_old, axis=1, keepdims=True)
        hout = _ml_output(gt, qb, kb, vb, _bdot(qb, c_old.astype(BF16)), qn)
        kw = k * gt["ws"]
        dec_full = jnp.broadcast_to(gt["dec"], (ROWS, ROWS))
        c_scr[h] = c_old * dec_full + _bdot(kw.T.astype(BF16), vb)
        n_scr[:, hs] = n_old * dec_full + _bdot(pre["bones"], kw.astype(BF16))
        m_rows_new = jnp.where(lane == h, jnp.broadcast_to(gt["m_new"], (ROWS, ROWS)), m_rows_new)
        hm = jax.nn.sigmoid(pcol("m_o", lo, HEAD_D)) * hout
        o_m = _silu(pcol("m_z", lo, HEAD_D)) * (_head_norm(hm) * mgn_ref[:, hs])
        o_ref[orows, 1536 + lo:1536 + lo + HEAD_D] = o_m.astype(BF16)
    m_scr[...] = m_rows_new


def _xmix_kernel(x_ref, g_ref, w_ref, wif_ref,
                 c2_ref, s2_ref, decay_ref, qdec_ref, kdec_ref,
                 wa_ref, wc_ref, cb_ref, lng_ref, lnb_ref, rgn_ref, mgn_ref, bi_ref, bf_ref,
                 o_ref, bufa_ref, sret_ref, bufc_ref, cml_ref, nml_ref, mml_ref,
                 hb_scr, exta, extc, shc, s_scr, c_scr, n_scr, m_scr,
                 *, cdec, steps_per_seq):
    s = pl.program_id(0)
    seq_step = s % steps_per_seq
    tab = (c2_ref, s2_ref, decay_ref, qdec_ref, kdec_ref)
    prm = (wa_ref, wc_ref, cb_ref, lng_ref, lnb_ref, rgn_ref, mgn_ref, bi_ref, bf_ref)
    scr = (exta, extc, shc, s_scr, c_scr, n_scr, m_scr)
    nchunk = STEP_ROWS // ROWS

    @pl.when(seq_step == 0)
    def _():
        exta[0:SUBLANES, :] = jnp.zeros((SUBLANES, BRANCH_W), F32)
        extc[0:32, :] = jnp.zeros((32, BRANCH_W), F32)
        s_scr[...] = jnp.zeros_like(s_scr)
        c_scr[...] = jnp.zeros_like(c_scr)
        n_scr[...] = jnp.zeros_like(n_scr)
        m_scr[...] = jnp.zeros_like(m_scr)

    for ci in range(nchunk):
        hb = hb_scr.at[ci % 2]
        hb[...] = _rmsnorm(x_ref[ci * ROWS:(ci + 1) * ROWS, :], g_ref[...]).astype(BF16)
        _mix_chunk(hb, w_ref, wif_ref, ci * ROWS, o_ref, tab, prm, scr, cdec)

    @pl.when(seq_step == steps_per_seq - 1)
    def _():
        bufa_ref[0] = exta[SUBLANES - (CONV_A_W - 1):SUBLANES, :]
        bufc_ref[0] = extc[32 - (CONV_C_W - 1):32, :]
        for h in range(HEADS):
            sret_ref[0, h] = s_scr[h]
            cml_ref[0, h] = c_scr[h]
            nml_ref[0, h:h + 1, :] = n_scr[0:1, h * HEAD_D:(h + 1) * HEAD_D]
        mml_ref[0] = m_scr[0:1, :]


def _xmix_prompt(x, gnorm, w_all, wif, layer, consts, prm, batch, seq):
    steps_per_seq = seq // STEP_ROWS
    nstep = batch * steps_per_seq
    c1 = lambda shape: _resident(shape, lambda s: (0,) * len(shape))
    rope = pl.BlockSpec((STEP_ROWS, HEAD_D), lambda s: (s % steps_per_seq, 0))
    in_specs = [
        pl.BlockSpec((STEP_ROWS, D_MODEL), lambda s: (s, 0)),
        c1((1, D_MODEL)),
        _resident((None, D_MODEL, W_BLOCK), lambda s: (layer, 0, 0)),
        c1((D_MODEL, N_IF)),
        rope, rope,
        c1((HEADS, ROWS, ROWS)),
        c1((ROWS, BRANCH_W)),
        c1((ROWS, BRANCH_W)),
        c1((CONV_A_W, BRANCH_W)),
        c1((CONV_C_W, BRANCH_W)),
    ] + [c1((1, BRANCH_W))] * 5 + [c1((1, 128))] * 2
    per_b3 = lambda s: (s // steps_per_seq, 0, 0)
    per_b4 = lambda s: (s // steps_per_seq, 0, 0, 0)
    out_specs = [
        pl.BlockSpec((STEP_ROWS, N_BRANCH * BRANCH_W), lambda s: (s, 0)),
        pl.BlockSpec((1, CONV_A_W - 1, BRANCH_W), per_b3),
        pl.BlockSpec((1, HEADS, HEAD_D, HEAD_D), per_b4),
        pl.BlockSpec((1, CONV_C_W - 1, BRANCH_W), per_b3),
        pl.BlockSpec((1, HEADS, HEAD_D, HEAD_D), per_b4),
        pl.BlockSpec((1, HEADS, HEAD_D), per_b3),
        pl.BlockSpec((1, 1, 128), per_b3),
    ]
    out_shape = [
        jax.ShapeDtypeStruct((batch * seq, N_BRANCH * BRANCH_W), BF16),
        jax.ShapeDtypeStruct((batch, CONV_A_W - 1, BRANCH_W), F32),
        jax.ShapeDtypeStruct((batch, HEADS, HEAD_D, HEAD_D), F32),
        jax.ShapeDtypeStruct((batch, CONV_C_W - 1, BRANCH_W), F32),
        jax.ShapeDtypeStruct((batch, HEADS, HEAD_D, HEAD_D), F32),
        jax.ShapeDtypeStruct((batch, HEADS, HEAD_D), F32),
        jax.ShapeDtypeStruct((batch, 1, 128), F32),
    ]
    scratch = [
        pltpu.VMEM((2, ROWS, D_MODEL), BF16),
        pltpu.VMEM((ROWS + SUBLANES, BRANCH_W), F32),
        pltpu.VMEM((ROWS + 32, BRANCH_W), F32),
        pltpu.VMEM((2, SUBLANES - 1, ROWS + 32 - SUBLANES, 128), F32),
        pltpu.VMEM((HEADS, HEAD_D, HEAD_D), F32),
        pltpu.VMEM((HEADS, HEAD_D, HEAD_D), F32),
        pltpu.VMEM((ROWS, BRANCH_W), F32),
        pltpu.VMEM((ROWS, ROWS), F32),
    ]
    return pl.pallas_call(
        functools.partial(_xmix_kernel, cdec=consts["cdec"], steps_per_seq=steps_per_seq),
        grid=(nstep,),
        in_specs=in_specs, out_specs=out_specs, out_shape=out_shape, scratch_shapes=scratch,
        compiler_params=pltpu.CompilerParams(
            dimension_semantics=("arbitrary",), vmem_limit_bytes=VMEM_LIMIT),
        name="xmix_prompt",
    )(x, gnorm, w_all, wif, consts["c2"], consts["s2"], consts["decay"], consts["qdec"],
      consts["kdec"], prm["wa"], prm["wc"], prm["cb"], prm["lng"], prm["lnb"], prm["rgn"],
      prm["mgn"], prm["bi"], prm["bf"])


def _mix_sample_kernel(pa_ref, pc_ref, rq_ref, rk_ref, rv_ref, rz_ref, mq_ref, mk_ref, mv_ref,
                       mo_ref, mz_ref, pif_ref, c2_ref, s2_ref, decay_ref, qdec_ref, kdec_ref,
                       cdec_ref, wa_ref, wc_ref, cb_ref, lng_ref, lnb_ref, rgn_ref, mgn_ref,
                       bi_ref, bf_ref, bufa_ref, sret_ref, bufc_ref, cml_ref, nrows_ref, mrows_ref,
                       o_ref, bufa_o, sret_o, bufc_o, cml_o, nrows_o, mrows_o, dec_scr):
    h = pl.program_id(1)
    sq = SAMPLE_SEQS
    nt = ROWS // sq

    @pl.when(h == 0)
    def _():
        ua = pa_ref[:, 512:1024] * pa_ref[:, 1024:1536]
        ea = [bufa_ref[:, 0, :], bufa_ref[:, 1, :]] + [ua[t * sq:(t + 1) * sq, :] for t in range(nt)]
        conv_a = jnp.concatenate(
            [wa_ref[0:1, :] * ea[t] + wa_ref[1:2, :] * ea[t + 1] + wa_ref[2:3, :] * ea[t + 2]
             for t in range(nt)], axis=0)
        o_ref[:, 0:512] = (_silu(pa_ref[:, 1536:2048]) * (pa_ref[:, 0:512] * conv_a)).astype(BF16)
        bufa_o[:, 0, :] = ea[nt]
        bufa_o[:, 1, :] = ea[nt + 1]
        uc = pc_ref[:, 0:512] * jax.nn.sigmoid(pc_ref[:, 512:1024])
        nb = CONV_C_W - 1
        ec = [bufc_ref[:, i, :] for i in range(nb)] + [uc[t * sq:(t + 1) * sq, :] for t in range(nt)]
        planes = []
        for t in range(nt):
            acc = wc_ref[0:1, :] * ec[t]
            for j in range(1, CONV_C_W):
                acc = acc + wc_ref[j:j + 1, :] * ec[t + j]
            planes.append(acc)
        conv_c = jnp.concatenate(planes, axis=0)
        o_ref[:, 1024:1536] = _branch_c_post(conv_c, pc_ref[:, 1024:1536], cb_ref[...], lng_ref[...],
                                             lnb_ref[...]).astype(BF16)
        for i in range(nb):
            bufc_o[:, i, :] = ec[i + nt]

    cmask, smask = _tile_masks(True)
    rowseq = lax.broadcasted_iota(jnp.int32, (ROWS, ROWS), 0) & (sq - 1)
    laneseq = lax.broadcasted_iota(jnp.int32, (ROWS, ROWS), 1) & (sq - 1)

    c2 = c2_ref[...]
    s2 = s2_ref[...]
    rq = _rotary(rq_ref[...], c2, s2)
    rk = _rotary(rk_ref[...], c2, s2) * (HEAD_D ** -0.5)
    rqb = rq.astype(BF16)
    rkb = rk.astype(BF16)
    rvb = rv_ref[...].astype(BF16)
    rkd_t = (rk * kdec_ref[...]).T
    cdec = cdec_ref[0, 0:1, :]

    pre = _ml_prelims(pif_ref[:, 0:128], pif_ref[:, 128:256], bi_ref[...], bf_ref[...], cmask, smask)
    m_prev = mrows_ref[...]
    gt = _ml_gates(pre, 0, m_prev[:, 0:1], cmask, smask)
    mq = mq_ref[...]
    mk = mk_ref[...] * (HEAD_D ** -0.5)
    mqb = mq.astype(BF16)
    mkb = mk.astype(BF16)
    mvb = mv_ref[...].astype(BF16)
    kw = mk * gt["ws"]
    kw_t = kw.T
    dec_full = jnp.broadcast_to(gt["dec"], (ROWS, ROWS))
    dec_scr[...] = dec_full

    def per_seq(b, carry):
        inter_r, inter_m = carry
        sb = sret_ref[b, 0]
        inter_r = jnp.where(rowseq == b, _bdot(rqb, sb.astype(BF16)), inter_r)
        k_sel = jnp.where(laneseq == b, rkd_t, 0.0).astype(BF16)
        sret_o[b, 0] = sb * cdec + _bdot(k_sel, rvb)
        cb_ = cml_ref[b, 0]
        inter_m = jnp.where(rowseq == b, _bdot(mqb, cb_.astype(BF16)), inter_m)
        kw_sel = jnp.where(laneseq == b, kw_t, 0.0).astype(BF16)
        cml_o[b, 0] = cb_ * dec_scr[pl.ds(b, 1), :] + _bdot(kw_sel, mvb)
        return inter_r, inter_m

    zero = jnp.zeros((ROWS, HEAD_D), F32)
    inter_r, inter_m = lax.fori_loop(0, sq, per_seq, (zero, zero), unroll=8)

    sc = _bdot_nt(rqb, rkb) * decay_ref[0]
    ret = _bdot(sc.astype(BF16), rvb) + inter_r * qdec_ref[...]
    o_r = (_silu(rz_ref[...]) * (_head_norm(ret) * rgn_ref[...])).astype(BF16)

    n_old = nrows_ref[...]
    qn = jnp.sum(mq * n_old, axis=1, keepdims=True)
    hout = _ml_output(gt, mqb, mkb, mvb, inter_m, qn)
    nrows_o[...] = n_old * dec_full + _bdot(pre["bones"], kw.astype(BF16))
    mrows_o[...] = jnp.broadcast_to(gt["m_new"], (ROWS, ROWS))
    hm = jax.nn.sigmoid(mo_ref[...]) * hout
    o_m = (_silu(mz_ref[...]) * (_head_norm(hm) * mgn_ref[...])).astype(BF16)

    for hh in range(HEADS):
        @pl.when(h == hh)
        def _(hh=hh):
            o_ref[:, 512 + hh * HEAD_D:512 + (hh + 1) * HEAD_D] = o_r
            o_ref[:, 1536 + hh * HEAD_D:1536 + (hh + 1) * HEAD_D] = o_m


def _mix_sample(p, pif, consts, prm, st, nseq, layer, depth, stacked):
    ntile = nseq // SAMPLE_SEQS
    sq = SAMPLE_SEQS
    head_cols = lambda name: pl.BlockSpec(
        (ROWS, HEAD_D), lambda i, h, _o=_COL[name] // HEAD_D: (i, _o + h))
    const2 = lambda i, h: (0, 0)
    per_head2 = lambda i, h: (0, h)
    conv_a_state = pl.BlockSpec((None, sq, CONV_A_W - 1, BRANCH_W), lambda i, h: (layer, i, 0, 0))
    conv_c_state = pl.BlockSpec((None, sq, CONV_C_W - 1, BRANCH_W), lambda i, h: (layer, i, 0, 0))
    matrix_state = pl.BlockSpec((None, sq, 1, HEAD_D, HEAD_D), lambda i, h: (layer, i, h, 0, 0))
    in_specs = [
        pl.BlockSpec((ROWS, 2048), lambda i, h: (i, 0)),
        pl.BlockSpec((ROWS, 2048), lambda i, h: (i, 2)),
    ] + [head_cols(n) for n in ("r_q", "r_k", "r_v", "r_z", "m_q", "m_k", "m_v", "m_o", "m_z")] + [
        pl.BlockSpec((ROWS, 256), lambda i, h: (i, h)),
        pl.BlockSpec((ROWS, HEAD_D), const2),
        pl.BlockSpec((ROWS, HEAD_D), const2),
        pl.BlockSpec((1, ROWS, ROWS), lambda i, h: (h, 0, 0)),
        pl.BlockSpec((ROWS, HEAD_D), per_head2),
        pl.BlockSpec((ROWS, HEAD_D), per_head2),
        pl.BlockSpec((1, 8, 128), lambda i, h: (h, 0, 0)),
        pl.BlockSpec((CONV_A_W, BRANCH_W), const2),
        pl.BlockSpec((CONV_C_W, BRANCH_W), const2),
        pl.BlockSpec((1, BRANCH_W), const2),
        pl.BlockSpec((1, BRANCH_W), const2),
        pl.BlockSpec((1, BRANCH_W), const2),
        pl.BlockSpec((1, HEAD_D), per_head2),
        pl.BlockSpec((1, HEAD_D), per_head2),
        pl.BlockSpec((1, 128), per_head2),
        pl.BlockSpec((1, 128), per_head2),
        conv_a_state, matrix_state, conv_c_state, matrix_state,
        pl.BlockSpec((ROWS, HEAD_D), lambda i, h: (i, h)),
        pl.BlockSpec((ROWS, HEAD_D), lambda i, h: (i, h)),
    ]
    out_specs = [
        pl.BlockSpec((ROWS, N_BRANCH * BRANCH_W), lambda i, h: (i, 0)),
        conv_a_state, matrix_state, conv_c_state, matrix_state,
        pl.BlockSpec((ROWS, HEAD_D), lambda i, h: (i, h)),
        pl.BlockSpec((ROWS, HEAD_D), lambda i, h: (i, h)),
    ]
    nrow = ntile * ROWS
    out_shape = [
        jax.ShapeDtypeStruct((nrow, N_BRANCH * BRANCH_W), BF16),
        jax.ShapeDtypeStruct((depth, nseq, CONV_A_W - 1, BRANCH_W), F32),
        jax.ShapeDtypeStruct((depth, nseq, HEADS, HEAD_D, HEAD_D), F32),
        jax.ShapeDtypeStruct((depth, nseq, CONV_C_W - 1, BRANCH_W), F32),
        jax.ShapeDtypeStruct((depth, nseq, HEADS, HEAD_D, HEAD_D), F32),
        jax.ShapeDtypeStruct((nrow, HEADS * HEAD_D), F32),
        jax.ShapeDtypeStruct((nrow, HEADS * HEAD_D), F32),
    ]
    args = [p] * 11 + [
        pif, consts["c2"], consts["s2"], consts["decay"], consts["qdec"], consts["kdec"],
        consts["cdec"], prm["wa"], prm["wc"], prm["cb"], prm["lng"], prm["lnb"], prm["rgn"],
        prm["mgn"], prm["bi_h"], prm["bf_h"],
        st["bufa"], st["sret"], st["bufc"], st["cml"], st["nrows"], st["mrows"]]
    assert len(args) == len(in_specs)
    n_in = len(args)
    aliases = {}
    if stacked is not None:
        aliases = {n_in + k: 1 + k for k in range(4)}
        in_specs = in_specs + [pl.BlockSpec(memory_space=pl.ANY)] * 4
        args = args + list(stacked)

    def entry(*refs):
        _mix_sample_kernel(*refs[:n_in], *refs[n_in + len(aliases):])

    return pl.pallas_call(
        entry,
        grid=(ntile, HEADS),
        in_specs=in_specs, out_specs=out_specs, out_shape=out_shape,
        scratch_shapes=[pltpu.VMEM((ROWS, ROWS), F32)],
        input_output_aliases=aliases,
        compiler_params=pltpu.CompilerParams(
            dimension_semantics=("arbitrary", "arbitrary"), vmem_limit_bytes=VMEM_LIMIT),
        name="mix_sample",
    )(*args)


def _rope_tables(pos):
    inv = ROPE_BASE ** (-jnp.arange(0, HEAD_D, 2, dtype=F32) / HEAD_D)
    ang = pos.astype(F32)[:, None] * inv[None, :]
    cos = jnp.cos(ang)
    sin = jnp.sin(ang)
    return jnp.concatenate([cos, cos], axis=1), jnp.concatenate([-sin, sin], axis=1)


def _decay_tables(t_of_row, seq_of_row, chunk_len):
    log_g = np.log1p(-(2.0 ** (-5.0 - np.arange(HEADS, dtype=np.float64))))
    dt = t_of_row[:, None] - t_of_row[None, :]
    ok = (seq_of_row[:, None] == seq_of_row[None, :]) & (dt >= 0)
    decay = np.where(ok[None], np.exp(np.maximum(dt, 0)[None] * log_g[:, None, None]), 0.0)
    qdec = np.exp((t_of_row[:, None] + 1.0) * log_g[None, :])
    kdec = np.exp((chunk_len - 1.0 - t_of_row)[:, None] * log_g[None, :])
    cdec = np.exp(chunk_len * log_g)
    rep = lambda a: jnp.asarray(np.repeat(a, HEAD_D, axis=1), F32)
    return jnp.asarray(decay, F32), rep(qdec), rep(kdec), cdec


def kernel(x_prompt, x_sample, state_conv_a, state_ret, state_conv_c, state_mlstm_C,
           state_mlstm_n, state_mlstm_m, norm_g, w_in, conv_a_w, conv_c_w, conv_c_b,
           ln_c_g, ln_c_b, ret_gn_g, ml_gn_g, ml_gate_b, w_br, w_out, final_g):
    bp, tp, _ = x_prompt.shape
    bs, ts, _ = x_sample.shape
    depth = w_in.shape[0]
    past_len = 16384
    sq = SAMPLE_SEQS
    ntile = bs // sq
    assert tp % STEP_ROWS == 0 and ts * sq == ROWS and bs % sq == 0

    r = np.arange(ROWS)
    c2p, s2p = _rope_tables(jnp.arange(tp, dtype=jnp.int32))
    decay_p, qdec_p, kdec_p, cdec_p = _decay_tables(r.astype(np.float64), np.zeros(ROWS), float(ROWS))
    consts_p = dict(c2=c2p, s2=s2p, decay=decay_p, qdec=qdec_p, kdec=kdec_p,
                    cdec=tuple(float(v) for v in cdec_p))
    t_s = (r // sq).astype(np.float64)
    c2s, s2s = _rope_tables(past_len + jnp.asarray(r // sq, jnp.int32))
    decay_s, qdec_s, kdec_s, cdec_s = _decay_tables(t_s, r % sq, float(ts))
    cdec_s_arr = jnp.asarray(np.broadcast_to(cdec_s[:, None, None], (HEADS, 8, 128)), F32)
    consts_s = dict(c2=c2s, s2=s2s, decay=decay_s, qdec=qdec_s, kdec=kdec_s, cdec=cdec_s_arr)

    xp = x_prompt.reshape(bp * tp, D_MODEL)
    xs = x_sample.reshape(ntile, sq, ts, D_MODEL).transpose(0, 2, 1, 3).reshape(bs * ts, D_MODEL)

    fg = final_g.reshape(1, D_MODEL)
    outs_p = [[] for _ in range(6)]
    n_s, m_s = [], []
    stacked = None
    sel_p = np.zeros((2 * HEADS, N_IF), np.float32)
    sel_s = np.zeros((2 * HEADS, HEADS * N_IF), np.float32)
    for h in range(HEADS):
        sel_p[h, h] = 1.0
        sel_p[HEADS + h, 128 + h] = 1.0
        sel_s[h, h * N_IF] = 1.0
        sel_s[HEADS + h, h * N_IF + 128] = 1.0
    w_bf = w_in.astype(BF16)
    pad_row = lambda v: jnp.pad(v, (0, 128 - v.shape[0])).reshape(1, 128)
    for l in range(depth):
        w_if = w_in[l, :, N_MAIN:N_MAIN + 2 * HEADS]
        spread = lambda sel: jnp.dot(w_if, jnp.asarray(sel), precision=lax.Precision.HIGHEST).astype(BF16)
        w_gate = w_bf[l, :, N_MAIN + 2 * HEADS:]
        gb = ml_gate_b[l]
        prm = dict(
            wa=conv_a_w[l], wc=conv_c_w[l], cb=conv_c_b[l].reshape(1, -1),
            lng=ln_c_g[l].reshape(1, -1), lnb=ln_c_b[l].reshape(1, -1),
            rgn=ret_gn_g[l].reshape(1, -1), mgn=ml_gn_g[l].reshape(1, -1),
            bi=pad_row(gb[:HEADS]), bf=pad_row(gb[HEADS:]),
            bi_h=jnp.concatenate([pad_row(gb[h:h + 1]) for h in range(HEADS)], axis=1),
            bf_h=jnp.concatenate([pad_row(gb[HEADS + h:HEADS + h + 1]) for h in range(HEADS)], axis=1),
        )
        gnorm = norm_g[l].reshape(1, D_MODEL)
        wbr = w_br[l].astype(BF16)
        wout = w_out[l].astype(BF16)
        final = l == depth - 1

        res = _xmix_prompt(xp, gnorm, w_bf, spread(sel_p), l, consts_p, prm, bp, tp)
        for k in range(5):
            outs_p[k].append(res[1 + k])
        outs_p[5].append(res[6][:, 0, :HEADS])
        xp = _merge(xp, gnorm, res[0], w_gate, wbr, wout, fg, final, tm=512)

        ps, pifs = _inproj(xs, gnorm, w_bf, spread(sel_s), l, tm=bs * ts, tn=2048)
        n_rows = jnp.broadcast_to(
            state_mlstm_n[l].reshape(ntile, 1, sq, HEADS * HEAD_D),
            (ntile, ts, sq, HEADS * HEAD_D)).reshape(bs * ts, HEADS * HEAD_D)
        m_rows = jnp.broadcast_to(
            state_mlstm_m[l].reshape(ntile, 1, sq, HEADS, 1),
            (ntile, ts, sq, HEADS, HEAD_D)).reshape(bs * ts, HEADS * HEAD_D)
        st = dict(bufa=state_conv_a, sret=state_ret, bufc=state_conv_c, cml=state_mlstm_C,
                  nrows=n_rows, mrows=m_rows)
        res = _mix_sample(ps, pifs, consts_s, prm, st, bs, l, depth, stacked)
        stacked = tuple(res[1:5])
        n_s.append(res[5].reshape(ntile, ts, sq, HEADS, HEAD_D)[:, 0].reshape(bs, HEADS, HEAD_D))
        m_s.append(res[6].reshape(ntile, ts, sq, HEADS, HEAD_D)[:, 0, :, :, 0].reshape(bs, HEADS))
        xs = _merge(xs, gnorm, res[0], w_gate, wbr, wout, fg, final, tm=bs * ts)

    y_prompt = xp.reshape(bp, tp, D_MODEL)
    y_sample = xs.reshape(ntile, ts, sq, D_MODEL).transpose(0, 2, 1, 3).reshape(bs, ts, D_MODEL)
    sp = [jnp.stack(a, axis=0) for a in outs_p]
    return (y_prompt, y_sample, sp[0], stacked[0], sp[1], stacked[1], sp[2], stacked[2],
            sp[3], stacked[3], sp[4], jnp.stack(n_s, axis=0), sp[5], jnp.stack(m_s, axis=0))
```

```python
import functools

import numpy as np
import jax
import jax.numpy as jnp
from jax import lax
from jax.experimental import pallas as pl
from jax.experimental.pallas import tpu as pltpu

F32 = jnp.float32
BF16 = jnp.bfloat16

D_MODEL = 1024
BRANCH_W = 512
N_BRANCH = 4
HEADS = 4
HEAD_D = 128
CONV_A_W = 3
CONV_C_W = 31
ROPE_BASE = 10000.0
EPS = 1e-6
N_MAIN = 16 * BRANCH_W
N_IF = 256
N_GATE = N_BRANCH * D_MODEL
ROWS = 128
STEP_ROWS = 4 * ROWS
W_BLOCK = N_MAIN + 512
SAMPLE_SEQS = 32
SUBLANES = 8
VMEM_LIMIT = 56 * 1024 * 1024

_COL = dict(a_b=0, a_c=512, a_u=1024, a_z=1536, r_q=2048, r_k=2560, r_v=3072, r_z=3584,
            glu_v=4096, glu_g=4608, c_z=5120, m_q=5632, m_k=6144, m_v=6656, m_o=7168, m_z=7680)


def _bdot(a, b):
    return jnp.dot(a, b, preferred_element_type=F32)


def _bdot_nt(a, b):
    return lax.dot_general(a, b, (((1,), (1,)), ((), ())), preferred_element_type=F32)


def _exact_mm(mat_b16, x):
    hi = x.astype(BF16)
    r1 = x - hi.astype(F32)
    mid = r1.astype(BF16)
    lo = (r1 - mid.astype(F32)).astype(BF16)
    return _bdot(mat_b16, hi) + _bdot(mat_b16, mid) + _bdot(mat_b16, lo)


def _silu(x):
    return x * jax.nn.sigmoid(x)


def _log_sigmoid(x):
    return jnp.minimum(x, 0.0) - jnp.log1p(jnp.exp(-jnp.abs(x)))


def _head_norm(y):
    mu = jnp.mean(y, axis=-1, keepdims=True)
    yc = y - mu
    var = jnp.mean(yc * yc, axis=-1, keepdims=True)
    return yc * lax.rsqrt(var + EPS)


def _rotary(x, c2, s2):
    return x * c2 + pltpu.roll(x, HEAD_D // 2, 1) * s2


def _rmsnorm(x, g):
    ms = jnp.mean(x * x, axis=-1, keepdims=True)
    return (x * lax.rsqrt(ms + EPS)) * g


def _resident(shape, imap):
    return pl.BlockSpec(shape, imap, pipeline_mode=pl.Buffered(1))


def _inproj_kernel(x_ref, g_ref, w_ref, wif_ref, p_ref, pif_ref, hb_ref):
    @pl.when(pl.program_id(1) == 0)
    def _():
        hb = _rmsnorm(x_ref[...], g_ref[...]).astype(BF16)
        hb_ref[...] = hb
        pif_ref[...] = _bdot(hb, wif_ref[...])

    p_ref[...] = _bdot(hb_ref[...], w_ref[...])


def _inproj(x, g, w_all, wif, layer, tm, tn):
    m = x.shape[0]
    nif = wif.shape[1]
    return pl.pallas_call(
        _inproj_kernel,
        grid=(m // tm, N_MAIN // tn),
        in_specs=[
            pl.BlockSpec((tm, D_MODEL), lambda i, j: (i, 0)),
            pl.BlockSpec((1, D_MODEL), lambda i, j: (0, 0)),
            pl.BlockSpec((None, D_MODEL, tn), lambda i, j: (layer, 0, j)),
            pl.BlockSpec((D_MODEL, nif), lambda i, j: (0, 0)),
        ],
        out_specs=[
            pl.BlockSpec((tm, tn), lambda i, j: (i, j)),
            pl.BlockSpec((tm, nif), lambda i, j: (i, 0)),
        ],
        out_shape=[jax.ShapeDtypeStruct((m, N_MAIN), F32), jax.ShapeDtypeStruct((m, nif), F32)],
        scratch_shapes=[pltpu.VMEM((tm, D_MODEL), BF16)],
        compiler_params=pltpu.CompilerParams(
            dimension_semantics=("arbitrary", "arbitrary"), vmem_limit_bytes=VMEM_LIMIT),
        name="inproj",
    )(x, g, w_all, wif)


def _merge_kernel(x_ref, g_ref, o_ref, wg_ref, wbr_ref, wout_ref, fg_ref, y_ref, hb_ref, *, final):
    x = x_ref[...]
    hb_ref[...] = _rmsnorm(x, g_ref[...]).astype(BF16)
    acc = None
    for n in range(N_BRANCH):
        proj = _bdot(o_ref[:, n * BRANCH_W:(n + 1) * BRANCH_W], wbr_ref[n])
        gate = jax.nn.sigmoid(_bdot(hb_ref[...], wg_ref[:, n * D_MODEL:(n + 1) * D_MODEL]))
        term = gate * proj
        acc = term if acc is None else acc + term
    y = x + _bdot(acc.astype(BF16), wout_ref[...])
    if final:
        y = _rmsnorm(y, fg_ref[...])
    y_ref[...] = y


def _merge(x, g, o, wg, wbr, wout, fg, final, tm):
    m = x.shape[0]
    resident = lambda shape: _resident(shape, lambda i: (0,) * len(shape))
    return pl.pallas_call(
        functools.partial(_merge_kernel, final=final),
        grid=(m // tm,),
        in_specs=[
            pl.BlockSpec((tm, D_MODEL), lambda i: (i, 0)),
            resident((1, D_MODEL)),
            pl.BlockSpec((tm, N_BRANCH * BRANCH_W), lambda i: (i, 0)),
            resident((D_MODEL, N_GATE)),
            resident((N_BRANCH, BRANCH_W, D_MODEL)),
            resident((D_MODEL, D_MODEL)),
            resident((1, D_MODEL)),
        ],
        out_specs=pl.BlockSpec((tm, D_MODEL), lambda i: (i, 0)),
        out_shape=jax.ShapeDtypeStruct((m, D_MODEL), F32),
        scratch_shapes=[pltpu.VMEM((tm, D_MODEL), BF16)],
        compiler_params=pltpu.CompilerParams(
            dimension_semantics=("arbitrary",), vmem_limit_bytes=VMEM_LIMIT),
        name="merge",
    )(x, g, o, wg, wbr, wout, fg)


def _tile_masks(sample):
    row = lax.broadcasted_iota(jnp.int32, (ROWS, ROWS), 0)
    col = lax.broadcasted_iota(jnp.int32, (ROWS, ROWS), 1)
    if not sample:
        return row >= col, None
    same = (row & (SAMPLE_SEQS - 1)) == (col & (SAMPLE_SEQS - 1))
    return jnp.logical_and(same, row >= col), same


def _ml_prelims(pi, pf, bi, bfo, cmask, smask):
    ai = pi + bi
    lf = _log_sigmoid(pf + bfo)
    tri = jnp.where(cmask, 1.0, 0.0).astype(BF16)
    if smask is None:
        bones = jnp.ones((ROWS, ROWS), BF16)
    else:
        bones = jnp.where(smask, 1.0, 0.0).astype(BF16)
    fcum = _exact_mm(tri, lf)
    ftot = _exact_mm(bones, lf)
    g = (ftot - fcum) + ai
    return dict(ai_t=ai.T, fcum=fcum, fcum_t=fcum.T, ftot=ftot, g=g, g_t=g.T, bones=bones)


def _ml_gates(pre, col, m_prev_c, cmask, smask):
    f_c = pre["fcum"][:, col:col + 1]
    f_r = pre["fcum_t"][col:col + 1, :]
    ig_r = pre["ai_t"][col:col + 1, :]
    g_c = pre["g"][:, col:col + 1]
    g_r = pre["g_t"][col:col + 1, :]
    ftot_c = pre["ftot"][:, col:col + 1]
    logw = jnp.where(cmask, (f_c - f_r) + ig_r, -jnp.inf)
    b = f_c + m_prev_c
    m_t = jnp.maximum(jnp.max(logw, axis=1, keepdims=True), b)
    wts = jnp.exp(logw - m_t)
    inter = jnp.exp(b - m_t)
    if smask is None:
        gmax = jnp.max(g_r, axis=1, keepdims=True)
    else:
        gmax = jnp.max(jnp.where(smask, g_r, -jnp.inf), axis=1, keepdims=True)
    m_new = jnp.maximum(ftot_c + m_prev_c, gmax)
    ws = jnp.exp(g_c - m_new)
    dec = jnp.exp((ftot_c + m_prev_c) - m_new)
    return dict(m_t=m_t, wts=wts, inter=inter, m_new=m_new, ws=ws, dec=dec)


def _ml_output(gt, qb, kb, vb, inter_term, qn):
    s = _bdot_nt(qb, kb) * gt["wts"]
    num = _bdot(s.astype(BF16), vb) + inter_term * gt["inter"]
    den = jnp.sum(s, axis=1, keepdims=True) + qn * gt["inter"]
    return num / jnp.maximum(jnp.abs(den), jnp.exp(-gt["m_t"]))


def _branch_c_post(conv, c_z, cb, lng, lnb):
    y = conv + cb
    mu = jnp.mean(y, axis=-1, keepdims=True)
    yc = y - mu
    var = jnp.mean(yc * yc, axis=-1, keepdims=True)
    ln = (yc * lax.rsqrt(var + EPS)) * lng + lnb
    return _silu(c_z) * _silu(ln)


_GROUPS = ((0, 4096), (4096, N_MAIN))


def _mix_chunk(hb_ref, w_ref, wif_ref, or0, o_ref, tab, prm, scr, cdec):
    c2_ref, s2_ref, decay_ref, qdec_ref, kdec_ref = tab
    wa_ref, wc_ref, cb_ref, lng_ref, lnb_ref, rgn_ref, mgn_ref, bi_ref, bf_ref = prm
    exta, extc, shc, s_scr, c_scr, n_scr, m_scr = scr
    orows = slice(or0, or0 + ROWS)
    projected = {}

    def pcol(name, lo=0, w=BRANCH_W):
        c0 = _COL[name]
        g0, g1 = next(g for g in _GROUPS if g[0] <= c0 < g[1])
        if g0 not in projected:
            projected[g0] = _bdot(hb_ref[...], w_ref[:, g0:g1])
        return projected[g0][:, c0 - g0 + lo:c0 - g0 + lo + w]

    ua = pcol("a_c") * pcol("a_u")
    exta[SUBLANES:SUBLANES + ROWS, :] = ua
    conv_a = (wa_ref[0:1, :] * exta[SUBLANES - 2:SUBLANES - 2 + ROWS, :]
              + wa_ref[1:2, :] * exta[SUBLANES - 1:SUBLANES - 1 + ROWS, :] + wa_ref[2:3, :] * ua)
    o_a = _silu(pcol("a_z")) * (pcol("a_b") * conv_a)
    o_ref[orows, 0:512] = o_a.astype(BF16)
    exta[0:SUBLANES, :] = exta[ROWS:ROWS + SUBLANES, :]

    extc[32:32 + ROWS, :] = pcol("glu_v") * jax.nn.sigmoid(pcol("glu_g"))
    span = ROWS + 32 - SUBLANES
    conv_blocks = []
    for lb in range(BRANCH_W // 128):
        lanes = slice(lb * 128, (lb + 1) * 128)
        sh = shc.at[lb % 2]
        for ph in range(1, SUBLANES):
            sh[ph - 1] = extc[ph:ph + span, lanes]
        acc = None
        for ph in range(SUBLANES):
            for a in range(span // SUBLANES - ROWS // SUBLANES + 1):
                j = a * SUBLANES + ph - 2
                if 0 <= j < CONV_C_W:
                    arows = slice(a * SUBLANES, a * SUBLANES + ROWS)
                    shifted = extc[arows, lanes] if ph == 0 else sh[ph - 1, arows, :]
                    term = wc_ref[j:j + 1, lanes] * shifted
                    acc = term if acc is None else acc + term
        conv_blocks.append(acc + wc_ref[CONV_C_W - 1:CONV_C_W, lanes] * extc[32:32 + ROWS, lanes])
    conv_c = jnp.concatenate(conv_blocks, axis=1)
    o_c = _branch_c_post(conv_c, pcol("c_z"), cb_ref[...], lng_ref[...], lnb_ref[...])
    o_ref[orows, 1024:1536] = o_c.astype(BF16)
    extc[0:32, :] = extc[ROWS:ROWS + 32, :]

    c2 = c2_ref[orows, :]
    s2 = s2_ref[orows, :]
    for h in range(HEADS):
        lo = h * HEAD_D
        hs = slice(lo, lo + HEAD_D)
        q = _rotary(pcol("r_q", lo, HEAD_D), c2, s2)
        k = _rotary(pcol("r_k", lo, HEAD_D), c2, s2) * (HEAD_D ** -0.5)
        qb = q.astype(BF16)
        kb = k.astype(BF16)
        vb = pcol("r_v", lo, HEAD_D).astype(BF16)
        s_old = s_scr[h]
        sc = _bdot_nt(qb, kb) * decay_ref[h]
        ret = _bdot(sc.astype(BF16), vb) + _bdot(qb, s_old.astype(BF16)) * qdec_ref[:, hs]
        s_scr[h] = s_old * cdec[h] + _bdot((k * kdec_ref[:, hs]).T.astype(BF16), vb)
        o_r = _silu(pcol("r_z", lo, HEAD_D)) * (_head_norm(ret) * rgn_ref[:, hs])
        o_ref[orows, 512 + lo:512 + lo + HEAD_D] = o_r.astype(BF16)

    cmask, smask = _tile_masks(False)
    pif = _bdot(hb_ref[...], wif_ref[...])
    pre = _ml_prelims(pif[:, 0:128], pif[:, 128:256], bi_ref[...], bf_ref[...], cmask, smask)
    lane = lax.broadcasted_iota(jnp.int32, (ROWS, ROWS), 1)
    m_rows = m_scr[...]
    m_rows_new = jnp.zeros((ROWS, ROWS), F32)
    for h in range(HEADS):
        lo = h * HEAD_D
        hs = slice(lo, lo + HEAD_D)
        gt = _ml_gates(pre, h, m_rows[:, h:h + 1], cmask, smask)
        q = pcol("m_q", lo, HEAD_D)
        k = pcol("m_k", lo, HEAD_D) * (HEAD_D ** -0.5)
        qb = q.astype(BF16)
        kb = k.astype(BF16)
        vb = pcol("m_v", lo, HEAD_D).astype(BF16)
        c_old = c_scr[h]
        n_old = n_scr[:, hs]
        qn = jnp.sum(q * n_old, axis=1, keepdims=True)
        hout = _ml_output(gt, qb, kb, vb, _bdot(qb, c_old.astype(BF16)), qn)
        kw = k * gt["ws"]
        dec_full = jnp.broadcast_to(gt["dec"], (ROWS, ROWS))
        c_scr[h] = c_old * dec_full + _bdot(kw.T.astype(BF16), vb)
        n_scr[:, hs] = n_old * dec_full + _bdot(pre["bones"], kw.astype(BF16))
        m_rows_new = jnp.where(lane == h, jnp.broadcast_to(gt["m_new"], (ROWS, ROWS)), m_rows_new)
        hm = jax.nn.sigmoid(pcol("m_o", lo, HEAD_D)) * hout
        o_m = _silu(pcol("m_z", lo, HEAD_D)) * (_head_norm(hm) * mgn_ref[:, hs])
        o_ref[orows, 1536 + lo:1536 + lo + HEAD_D] = o_m.astype(BF16)
    m_scr[...] = m_rows_new


def _xmix_kernel(x_ref, g_ref, w_ref, wif_ref,
                 c2_ref, s2_ref, decay_ref, qdec_ref, kdec_ref,
                 wa_ref, wc_ref, cb_ref, lng_ref, lnb_ref, rgn_ref, mgn_ref, bi_ref, bf_ref,
                 o_ref, bufa_ref, sret_ref, bufc_ref, cml_ref, nml_ref, mml_ref,
                 hb_scr, exta, extc, shc, s_scr, c_scr, n_scr, m_scr,
                 *, cdec, steps_per_seq):
    s = pl.program_id(0)
    seq_step = s % steps_per_seq
    tab = (c2_ref, s2_ref, decay_ref, qdec_ref, kdec_ref)
    prm = (wa_ref, wc_ref, cb_ref, lng_ref, lnb_ref, rgn_ref, mgn_ref, bi_ref, bf_ref)
    scr = (exta, extc, shc, s_scr, c_scr, n_scr, m_scr)
    nchunk = STEP_ROWS // ROWS

    @pl.when(seq_step == 0)
    def _():
        exta[0:SUBLANES, :] = jnp.zeros((SUBLANES, BRANCH_W), F32)
        extc[0:32, :] = jnp.zeros((32, BRANCH_W), F32)
        s_scr[...] = jnp.zeros_like(s_scr)
        c_scr[...] = jnp.zeros_like(c_scr)
        n_scr[...] = jnp.zeros_like(n_scr)
        m_scr[...] = jnp.zeros_like(m_scr)

    for ci in range(nchunk):
        hb = hb_scr.at[ci % 2]
        hb[...] = _rmsnorm(x_ref[ci * ROWS:(ci + 1) * ROWS, :], g_ref[...]).astype(BF16)
        _mix_chunk(hb, w_ref, wif_ref, ci * ROWS, o_ref, tab, prm, scr, cdec)

    @pl.when(seq_step == steps_per_seq - 1)
    def _():
        bufa_ref[0] = exta[SUBLANES - (CONV_A_W - 1):SUBLANES, :]
        bufc_ref[0] = extc[32 - (CONV_C_W - 1):32, :]
        for h in range(HEADS):
            sret_ref[0, h] = s_scr[h]
            cml_ref[0, h] = c_scr[h]
            nml_ref[0, h:h + 1, :] = n_scr[0:1, h * HEAD_D:(h + 1) * HEAD_D]
        mml_ref[0] = m_scr[0:1, :]


def _xmix_prompt(x, gnorm, w_all, wif, layer, consts, prm, batch, seq):
    steps_per_seq = seq // STEP_ROWS
    nstep = batch * steps_per_seq
    c1 = lambda shape: _resident(shape, lambda s: (0,) * len(shape))
    rope = pl.BlockSpec((STEP_ROWS, HEAD_D), lambda s: (s % steps_per_seq, 0))
    in_specs = [
        pl.BlockSpec((STEP_ROWS, D_MODEL), lambda s: (s, 0)),
        c1((1, D_MODEL)),
        _resident((None, D_MODEL, W_BLOCK), lambda s: (layer, 0, 0)),
        c1((D_MODEL, N_IF)),
        rope, rope,
        c1((HEADS, ROWS, ROWS)),
        c1((ROWS, BRANCH_W)),
        c1((ROWS, BRANCH_W)),
        c1((CONV_A_W, BRANCH_W)),
        c1((CONV_C_W, BRANCH_W)),
    ] + [c1((1, BRANCH_W))] * 5 + [c1((1, 128))] * 2
    per_b3 = lambda s: (s // steps_per_seq, 0, 0)
    per_b4 = lambda s: (s // steps_per_seq, 0, 0, 0)
    out_specs = [
        pl.BlockSpec((STEP_ROWS, N_BRANCH * BRANCH_W), lambda s: (s, 0)),
        pl.BlockSpec((1, CONV_A_W - 1, BRANCH_W), per_b3),
        pl.BlockSpec((1, HEADS, HEAD_D, HEAD_D), per_b4),
        pl.BlockSpec((1, CONV_C_W - 1, BRANCH_W), per_b3),
        pl.BlockSpec((1, HEADS, HEAD_D, HEAD_D), per_b4),
        pl.BlockSpec((1, HEADS, HEAD_D), per_b3),
        pl.BlockSpec((1, 1, 128), per_b3),
    ]
    out_shape = [
        jax.ShapeDtypeStruct((batch * seq, N_BRANCH * BRANCH_W), BF16),
        jax.ShapeDtypeStruct((batch, CONV_A_W - 1, BRANCH_W), F32),
        jax.ShapeDtypeStruct((batch, HEADS, HEAD_D, HEAD_D), F32),
        jax.ShapeDtypeStruct((batch, CONV_C_W - 1, BRANCH_W), F32),
        jax.ShapeDtypeStruct((batch, HEADS, HEAD_D, HEAD_D), F32),
        jax.ShapeDtypeStruct((batch, HEADS, HEAD_D), F32),
        jax.ShapeDtypeStruct((batch, 1, 128), F32),
    ]
    scratch = [
        pltpu.VMEM((2, ROWS, D_MODEL), BF16),
        pltpu.VMEM((ROWS + SUBLANES, BRANCH_W), F32),
        pltpu.VMEM((ROWS + 32, BRANCH_W), F32),
        pltpu.VMEM((2, SUBLANES - 1, ROWS + 32 - SUBLANES, 128), F32),
        pltpu.VMEM((HEADS, HEAD_D, HEAD_D), F32),
        pltpu.VMEM((HEADS, HEAD_D, HEAD_D), F32),
        pltpu.VMEM((ROWS, BRANCH_W), F32),
        pltpu.VMEM((ROWS, ROWS), F32),
    ]
    return pl.pallas_call(
        functools.partial(_xmix_kernel, cdec=consts["cdec"], steps_per_seq=steps_per_seq),
        grid=(nstep,),
        in_specs=in_specs, out_specs=out_specs, out_shape=out_shape, scratch_shapes=scratch,
        compiler_params=pltpu.CompilerParams(
            dimension_semantics=("arbitrary",), vmem_limit_bytes=VMEM_LIMIT),
        name="xmix_prompt",
    )(x, gnorm, w_all, wif, consts["c2"], consts["s2"], consts["decay"], consts["qdec"],
      consts["kdec"], prm["wa"], prm["wc"], prm["cb"], prm["lng"], prm["lnb"], prm["rgn"],
      prm["mgn"], prm["bi"], prm["bf"])


def _mix_sample_kernel(pa_ref, pc_ref, rq_ref, rk_ref, rv_ref, rz_ref, mq_ref, mk_ref, mv_ref,
                       mo_ref, mz_ref, pif_ref, c2_ref, s2_ref, decay_ref, qdec_ref, kdec_ref,
                       cdec_ref, wa_ref, wc_ref, cb_ref, lng_ref, lnb_ref, rgn_ref, mgn_ref,
                       bi_ref, bf_ref, bufa_ref, sret_ref, bufc_ref, cml_ref, nrows_ref, mrows_ref,
                       o_ref, bufa_o, sret_o, bufc_o, cml_o, nrows_o, mrows_o, dec_scr):
    h = pl.program_id(1)
    sq = SAMPLE_SEQS
    nt = ROWS // sq

    @pl.when(h == 0)
    def _():
        ua = pa_ref[:, 512:1024] * pa_ref[:, 1024:1536]
        ea = [bufa_ref[:, 0, :], bufa_ref[:, 1, :]] + [ua[t * sq:(t + 1) * sq, :] for t in range(nt)]
        conv_a = jnp.concatenate(
            [wa_ref[0:1, :] * ea[t] + wa_ref[1:2, :] * ea[t + 1] + wa_ref[2:3, :] * ea[t + 2]
             for t in range(nt)], axis=0)
        o_ref[:, 0:512] = (_silu(pa_ref[:, 1536:2048]) * (pa_ref[:, 0:512] * conv_a)).astype(BF16)
        bufa_o[:, 0, :] = ea[nt]
        bufa_o[:, 1, :] = ea[nt + 1]
        uc = pc_ref[:, 0:512] * jax.nn.sigmoid(pc_ref[:, 512:1024])
        nb = CONV_C_W - 1
        ec = [bufc_ref[:, i, :] for i in range(nb)] + [uc[t * sq:(t + 1) * sq, :] for t in range(nt)]
        planes = []
        for t in range(nt):
            acc = wc_ref[0:1, :] * ec[t]
            for j in range(1, CONV_C_W):
                acc = acc + wc_ref[j:j + 1, :] * ec[t + j]
            planes.append(acc)
        conv_c = jnp.concatenate(planes, axis=0)
        o_ref[:, 1024:1536] = _branch_c_post(conv_c, pc_ref[:, 1024:1536], cb_ref[...], lng_ref[...],
                                             lnb_ref[...]).astype(BF16)
        for i in range(nb):
            bufc_o[:, i, :] = ec[i + nt]

    cmask, smask = _tile_masks(True)
    rowseq = lax.broadcasted_iota(jnp.int32, (ROWS, ROWS), 0) & (sq - 1)
    laneseq = lax.broadcasted_iota(jnp.int32, (ROWS, ROWS), 1) & (sq - 1)

    c2 = c2_ref[...]
    s2 = s2_ref[...]
    rq = _rotary(rq_ref[...], c2, s2)
    rk = _rotary(rk_ref[...], c2, s2) * (HEAD_D ** -0.5)
    rqb = rq.astype(BF16)
    rkb = rk.astype(BF16)
    rvb = rv_ref[...].astype(BF16)
    rkd_t = (rk * kdec_ref[...]).T
    cdec = cdec_ref[0, 0:1, :]

    pre = _ml_prelims(pif_ref[:, 0:128], pif_ref[:, 128:256], bi_ref[...], bf_ref[...], cmask, smask)
    m_prev = mrows_ref[...]
    gt = _ml_gates(pre, 0, m_prev[:, 0:1], cmask, smask)
    mq = mq_ref[...]
    mk = mk_ref[...] * (HEAD_D ** -0.5)
    mqb = mq.astype(BF16)
    mkb = mk.astype(BF16)
    mvb = mv_ref[...].astype(BF16)
    kw = mk * gt["ws"]
    kw_t = kw.T
    dec_full = jnp.broadcast_to(gt["dec"], (ROWS, ROWS))
    dec_scr[...] = dec_full

    def per_seq(b, carry):
        inter_r, inter_m = carry
        sb = sret_ref[b, 0]
        inter_r = jnp.where(rowseq == b, _bdot(rqb, sb.astype(BF16)), inter_r)
        k_sel = jnp.where(laneseq == b, rkd_t, 0.0).astype(BF16)
        sret_o[b, 0] = sb * cdec + _bdot(k_sel, rvb)
        cb_ = cml_ref[b, 0]
        inter_m = jnp.where(rowseq == b, _bdot(mqb, cb_.astype(BF16)), inter_m)
        kw_sel = jnp.where(laneseq == b, kw_t, 0.0).astype(BF16)
        cml_o[b, 0] = cb_ * dec_scr[pl.ds(b, 1), :] + _bdot(kw_sel, mvb)
        return inter_r, inter_m

    zero = jnp.zeros((ROWS, HEAD_D), F32)
    inter_r, inter_m = lax.fori_loop(0, sq, per_seq, (zero, zero), unroll=8)

    sc = _bdot_nt(rqb, rkb) * decay_ref[0]
    ret = _bdot(sc.astype(BF16), rvb) + inter_r * qdec_ref[...]
    o_r = (_silu(rz_ref[...]) * (_head_norm(ret) * rgn_ref[...])).astype(BF16)

    n_old = nrows_ref[...]
    qn = jnp.sum(mq * n_old, axis=1, keepdims=True)
    hout = _ml_output(gt, mqb, mkb, mvb, inter_m, qn)
    nrows_o[...] = n_old * dec_full + _bdot(pre["bones"], kw.astype(BF16))
    mrows_o[...] = jnp.broadcast_to(gt["m_new"], (ROWS, ROWS))
    hm = jax.nn.sigmoid(mo_ref[...]) * hout
    o_m = (_silu(mz_ref[...]) * (_head_norm(hm) * mgn_ref[...])).astype(BF16)

    for hh in range(HEADS):
        @pl.when(h == hh)
        def _(hh=hh):
            o_ref[:, 512 + hh * HEAD_D:512 + (hh + 1) * HEAD_D] = o_r
            o_ref[:, 1536 + hh * HEAD_D:1536 + (hh + 1) * HEAD_D] = o_m


def _mix_sample(p, pif, consts, prm, st, nseq, layer, depth, stacked):
    ntile = nseq // SAMPLE_SEQS
    sq = SAMPLE_SEQS
    head_cols = lambda name: pl.BlockSpec(
        (ROWS, HEAD_D), lambda i, h, _o=_COL[name] // HEAD_D: (i, _o + h))
    const2 = lambda i, h: (0, 0)
    per_head2 = lambda i, h: (0, h)
    conv_a_state = pl.BlockSpec((None, sq, CONV_A_W - 1, BRANCH_W), lambda i, h: (layer, i, 0, 0))
    conv_c_state = pl.BlockSpec((None, sq, CONV_C_W - 1, BRANCH_W), lambda i, h: (layer, i, 0, 0))
    matrix_state = pl.BlockSpec((None, sq, 1, HEAD_D, HEAD_D), lambda i, h: (layer, i, h, 0, 0))
    in_specs = [
        pl.BlockSpec((ROWS, 2048), lambda i, h: (i, 0)),
        pl.BlockSpec((ROWS, 2048), lambda i, h: (i, 2)),
    ] + [head_cols(n) for n in ("r_q", "r_k", "r_v", "r_z", "m_q", "m_k", "m_v", "m_o", "m_z")] + [
        pl.BlockSpec((ROWS, 256), lambda i, h: (i, h)),
        pl.BlockSpec((ROWS, HEAD_D), const2),
        pl.BlockSpec((ROWS, HEAD_D), const2),
        pl.BlockSpec((1, ROWS, ROWS), lambda i, h: (h, 0, 0)),
        pl.BlockSpec((ROWS, HEAD_D), per_head2),
        pl.BlockSpec((ROWS, HEAD_D), per_head2),
        pl.BlockSpec((1, 8, 128), lambda i, h: (h, 0, 0)),
        pl.BlockSpec((CONV_A_W, BRANCH_W), const2),
        pl.BlockSpec((CONV_C_W, BRANCH_W), const2),
        pl.BlockSpec((1, BRANCH_W), const2),
        pl.BlockSpec((1, BRANCH_W), const2),
        pl.BlockSpec((1, BRANCH_W), const2),
        pl.BlockSpec((1, HEAD_D), per_head2),
        pl.BlockSpec((1, HEAD_D), per_head2),
        pl.BlockSpec((1, 128), per_head2),
        pl.BlockSpec((1, 128), per_head2),
        conv_a_state, matrix_state, conv_c_state, matrix_state,
        pl.BlockSpec((ROWS, HEAD_D), lambda i, h: (i, h)),
        pl.BlockSpec((ROWS, HEAD_D), lambda i, h: (i, h)),
    ]
    out_specs = [
        pl.BlockSpec((ROWS, N_BRANCH * BRANCH_W), lambda i, h: (i, 0)),
        conv_a_state, matrix_state, conv_c_state, matrix_state,
        pl.BlockSpec((ROWS, HEAD_D), lambda i, h: (i, h)),
        pl.BlockSpec((ROWS, HEAD_D), lambda i, h: (i, h)),
    ]
    nrow = ntile * ROWS
    out_shape = [
        jax.ShapeDtypeStruct((nrow, N_BRANCH * BRANCH_W), BF16),
        jax.ShapeDtypeStruct((depth, nseq, CONV_A_W - 1, BRANCH_W), F32),
        jax.ShapeDtypeStruct((depth, nseq, HEADS, HEAD_D, HEAD_D), F32),
        jax.ShapeDtypeStruct((depth, nseq, CONV_C_W - 1, BRANCH_W), F32),
        jax.ShapeDtypeStruct((depth, nseq, HEADS, HEAD_D, HEAD_D), F32),
        jax.ShapeDtypeStruct((nrow, HEADS * HEAD_D), F32),
        jax.ShapeDtypeStruct((nrow, HEADS * HEAD_D), F32),
    ]
    args = [p] * 11 + [
        pif, consts["c2"], consts["s2"], consts["decay"], consts["qdec"], consts["kdec"],
        consts["cdec"], prm["wa"], prm["wc"], prm["cb"], prm["lng"], prm["lnb"], prm["rgn"],
        prm["mgn"], prm["bi_h"], prm["bf_h"],
        st["bufa"], st["sret"], st["bufc"], st["cml"], st["nrows"], st["mrows"]]
    assert len(args) == len(in_specs)
    n_in = len(args)
    aliases = {}
    if stacked is not None:
        aliases = {n_in + k: 1 + k for k in range(4)}
        in_specs = in_specs + [pl.BlockSpec(memory_space=pl.ANY)] * 4
        args = args + list(stacked)

    def entry(*refs):
        _mix_sample_kernel(*refs[:n_in], *refs[n_in + len(aliases):])

    return pl.pallas_call(
        entry,
        grid=(ntile, HEADS),
        in_specs=in_specs, out_specs=out_specs, out_shape=out_shape,
        scratch_shapes=[pltpu.VMEM((ROWS, ROWS), F32)],
        input_output_aliases=aliases,
        compiler_params=pltpu.CompilerParams(
            dimension_semantics=("arbitrary", "arbitrary"), vmem_limit_bytes=VMEM_LIMIT),
        name="mix_sample",
    )(*args)


def _rope_tables(pos):
    inv = ROPE_BASE ** (-jnp.arange(0, HEAD_D, 2, dtype=F32) / HEAD_D)
    ang = pos.astype(F32)[:, None] * inv[None, :]
    cos = jnp.cos(ang)
    sin = jnp.sin(ang)
    return jnp.concatenate([cos, cos], axis=1), jnp.concatenate([-sin, sin], axis=1)


def _decay_tables(t_of_row, seq_of_row, chunk_len):
    log_g = np.log1p(-(2.0 ** (-5.0 - np.arange(HEADS, dtype=np.float64))))
    dt = t_of_row[:, None] - t_of_row[None, :]
    ok = (seq_of_row[:, None] == seq_of_row[None, :]) & (dt >= 0)
    decay = np.where(ok[None], np.exp(np.maximum(dt, 0)[None] * log_g[:, None, None]), 0.0)
    qdec = np.exp((t_of_row[:, None] + 1.0) * log_g[None, :])
    kdec = np.exp((chunk_len - 1.0 - t_of_row)[:, None] * log_g[None, :])
    cdec = np.exp(chunk_len * log_g)
    rep = lambda a: jnp.asarray(np.repeat(a, HEAD_D, axis=1), F32)
    return jnp.asarray(decay, F32), rep(qdec), rep(kdec), cdec


def kernel(x_prompt, x_sample, state_conv_a, state_ret, state_conv_c, state_mlstm_C,
           state_mlstm_n, state_mlstm_m, norm_g, w_in, conv_a_w, conv_c_w, conv_c_b,
           ln_c_g, ln_c_b, ret_gn_g, ml_gn_g, ml_gate_b, w_br, w_out, final_g):
    bp, tp, _ = x_prompt.shape
    bs, ts, _ = x_sample.shape
    depth = w_in.shape[0]
    past_len = 16384
    sq = SAMPLE_SEQS
    ntile = bs // sq
    assert tp % STEP_ROWS == 0 and ts * sq == ROWS and bs % sq == 0

    r = np.arange(ROWS)
    c2p, s2p = _rope_tables(jnp.arange(tp, dtype=jnp.int32))
    decay_p, qdec_p, kdec_p, cdec_p = _decay_tables(r.astype(np.float64), np.zeros(ROWS), float(ROWS))
    consts_p = dict(c2=c2p, s2=s2p, decay=decay_p, qdec=qdec_p, kdec=kdec_p,
                    cdec=tuple(float(v) for v in cdec_p))
    t_s = (r // sq).astype(np.float64)
    c2s, s2s = _rope_tables(past_len + jnp.asarray(r // sq, jnp.int32))
    decay_s, qdec_s, kdec_s, cdec_s = _decay_tables(t_s, r % sq, float(ts))
    cdec_s_arr = jnp.asarray(np.broadcast_to(cdec_s[:, None, None], (HEADS, 8, 128)), F32)
    consts_s = dict(c2=c2s, s2=s2s, decay=decay_s, qdec=qdec_s, kdec=kdec_s, cdec=cdec_s_arr)

    xp = x_prompt.reshape(bp * tp, D_MODEL)
    xs = x_sample.reshape(ntile, sq, ts, D_MODEL).transpose(0, 2, 1, 3).reshape(bs * ts, D_MODEL)

    fg = final_g.reshape(1, D_MODEL)
    outs_p = [[] for _ in range(6)]
    n_s, m_s = [], []
    stacked = None
    sel_p = np.zeros((2 * HEADS, N_IF), np.float32)
    sel_s = np.zeros((2 * HEADS, HEADS * N_IF), np.float32)
    for h in range(HEADS):
        sel_p[h, h] = 1.0
        sel_p[HEADS + h, 128 + h] = 1.0
        sel_s[h, h * N_IF] = 1.0
        sel_s[HEADS + h, h * N_IF + 128] = 1.0
    w_bf = w_in.astype(BF16)
    pad_row = lambda v: jnp.pad(v, (0, 128 - v.shape[0])).reshape(1, 128)
    for l in range(depth):
        w_if = w_in[l, :, N_MAIN:N_MAIN + 2 * HEADS]
        spread = lambda sel: jnp.dot(w_if, jnp.asarray(sel), precision=lax.Precision.HIGHEST).astype(BF16)
        w_gate = w_bf[l, :, N_MAIN + 2 * HEADS:]
        gb = ml_gate_b[l]
        prm = dict(
            wa=conv_a_w[l], wc=conv_c_w[l], cb=conv_c_b[l].reshape(1, -1),
            lng=ln_c_g[l].reshape(1, -1), lnb=ln_c_b[l].reshape(1, -1),
            rgn=ret_gn_g[l].reshape(1, -1), mgn=ml_gn_g[l].reshape(1, -1),
            bi=pad_row(gb[:HEADS]), bf=pad_row(gb[HEADS:]),
            bi_h=jnp.concatenate([pad_row(gb[h:h + 1]) for h in range(HEADS)], axis=1),
            bf_h=jnp.concatenate([pad_row(gb[HEADS + h:HEADS + h + 1]) for h in range(HEADS)], axis=1),
        )
        gnorm = norm_g[l].reshape(1, D_MODEL)
        wbr = w_br[l].astype(BF16)
        wout = w_out[l].astype(BF16)
        final = l == depth - 1

        res = _xmix_prompt(xp, gnorm, w_bf, spread(sel_p), l, consts_p, prm, bp, tp)
        for k in range(5):
            outs_p[k].append(res[1 + k])
        outs_p[5].append(res[6][:, 0, :HEADS])
        xp = _merge(xp, gnorm, res[0], w_gate, wbr, wout, fg, final, tm=512)

        ps, pifs = _inproj(xs, gnorm, w_bf, spread(sel_s), l, tm=bs * ts, tn=2048)
        n_rows = jnp.broadcast_to(
            state_mlstm_n[l].reshape(ntile, 1, sq, HEADS * HEAD_D),
            (ntile, ts, sq, HEADS * HEAD_D)).reshape(bs * ts, HEADS * HEAD_D)
        m_rows = jnp.broadcast_to(
            state_mlstm_m[l].reshape(ntile, 1, sq, HEADS, 1),
            (ntile, ts, sq, HEADS, HEAD_D)).reshape(bs * ts, HEADS * HEAD_D)
        st = dict(bufa=state_conv_a, sret=state_ret, bufc=state_conv_c, cml=state_mlstm_C,
                  nrows=n_rows, mrows=m_rows)
        res = _mix_sample(ps, pifs, consts_s, prm, st, bs, l, depth, stacked)
        stacked = tuple(res[1:5])
        n_s.append(res[5].reshape(ntile, ts, sq, HEADS, HEAD_D)[:, 0].reshape(bs, HEADS, HEAD_D))
        m_s.append(res[6].reshape(ntile, ts, sq, HEADS, HEAD_D)[:, 0, :, :, 0].reshape(bs, HEADS))
        xs = _merge(xs, gnorm, res[0], w_gate, wbr, wout, fg, final, tm=bs * ts)

    y_prompt = xp.reshape(bp, tp, D_MODEL)
    y_sample = xs.reshape(ntile, ts, sq, D_MODEL).transpose(0, 2, 1, 3).reshape(bs, ts, D_MODEL)
    sp = [jnp.stack(a, axis=0) for a in outs_p]
    return (y_prompt, y_sample, sp[0], stacked[0], sp[1], stacked[1], sp[2], stacked[2],
            sp[3], stacked[3], sp[4], jnp.stack(n_s, axis=0), sp[5], jnp.stack(m_s, axis=0))
```

```python
import functools

import numpy as np
import jax
import jax.numpy as jnp
from jax import lax
from jax.experimental import pallas as pl
from jax.experimental.pallas import tpu as pltpu

F32 = jnp.float32
BF16 = jnp.bfloat16

D_MODEL = 1024
BRANCH_W = 512
N_BRANCH = 4
HEADS = 4
HEAD_D = 128
CONV_A_W = 3
CONV_C_W = 31
ROPE_BASE = 10000.0
EPS = 1e-6
N_MAIN = 16 * BRANCH_W
N_IF = 256
N_GATE = N_BRANCH * D_MODEL
ROWS = 128
STEP_ROWS = 4 * ROWS
W_BLOCK = N_MAIN + 512
SAMPLE_SEQS = 32
SUBLANES = 8
VMEM_LIMIT = 56 * 1024 * 1024

_COL = dict(a_b=0, a_c=512, a_u=1024, a_z=1536, r_q=2048, r_k=2560, r_v=3072, r_z=3584,
            glu_v=4096, glu_g=4608, c_z=5120, m_q=5632, m_k=6144, m_v=6656, m_o=7168, m_z=7680)


def _bdot(a, b):
    return jnp.dot(a, b, preferred_element_type=F32)


def _bdot_nt(a, b):
    return lax.dot_general(a, b, (((1,), (1,)), ((), ())), preferred_element_type=F32)


def _exact_mm(mat_b16, x):
    hi = x.astype(BF16)
    r1 = x - hi.astype(F32)
    mid = r1.astype(BF16)
    lo = (r1 - mid.astype(F32)).astype(BF16)
    return _bdot(mat_b16, hi) + _bdot(mat_b16, mid) + _bdot(mat_b16, lo)


def _silu(x):
    return x * jax.nn.sigmoid(x)


def _log_sigmoid(x):
    return jnp.minimum(x, 0.0) - jnp.log1p(jnp.exp(-jnp.abs(x)))


def _head_norm(y):
    mu = jnp.mean(y, axis=-1, keepdims=True)
    yc = y - mu
    var = jnp.mean(yc * yc, axis=-1, keepdims=True)
    return yc * lax.rsqrt(var + EPS)


def _rotary(x, c2, s2):
    return x * c2 + pltpu.roll(x, HEAD_D // 2, 1) * s2


def _rmsnorm(x, g):
    ms = jnp.mean(x * x, axis=-1, keepdims=True)
    return (x * lax.rsqrt(ms + EPS)) * g


def _resident(shape, imap):
    return pl.BlockSpec(shape, imap, pipeline_mode=pl.Buffered(1))


def _inproj_kernel(x_ref, g_ref, w_ref, wif_ref, p_ref, pif_ref, hb_ref):
    @pl.when(pl.program_id(1) == 0)
    def _():
        hb = _rmsnorm(x_ref[...], g_ref[...]).astype(BF16)
        hb_ref[...] = hb
        pif_ref[...] = _bdot(hb, wif_ref[...])

    p_ref[...] = _bdot(hb_ref[...], w_ref[...])


def _inproj(x, g, w_all, wif, layer, tm, tn):
    m = x.shape[0]
    nif = wif.shape[1]
    return pl.pallas_call(
        _inproj_kernel,
        grid=(m // tm, N_MAIN // tn),
        in_specs=[
            pl.BlockSpec((tm, D_MODEL), lambda i, j: (i, 0)),
            pl.BlockSpec((1, D_MODEL), lambda i, j: (0, 0)),
            pl.BlockSpec((None, D_MODEL, tn), lambda i, j: (layer, 0, j)),
            pl.BlockSpec((D_MODEL, nif), lambda i, j: (0, 0)),
        ],
        out_specs=[
            pl.BlockSpec((tm, tn), lambda i, j: (i, j)),
            pl.BlockSpec((tm, nif), lambda i, j: (i, 0)),
        ],
        out_shape=[jax.ShapeDtypeStruct((m, N_MAIN), F32), jax.ShapeDtypeStruct((m, nif), F32)],
        scratch_shapes=[pltpu.VMEM((tm, D_MODEL), BF16)],
        compiler_params=pltpu.CompilerParams(
            dimension_semantics=("arbitrary", "arbitrary"), vmem_limit_bytes=VMEM_LIMIT),
        name="inproj",
    )(x, g, w_all, wif)


def _merge_kernel(x_ref, g_ref, o_ref, wg_ref, wbr_ref, wout_ref, fg_ref, y_ref, hb_ref, *, final):
    x = x_ref[...]
    hb_ref[...] = _rmsnorm(x, g_ref[...]).astype(BF16)
    acc = None
    for n in range(N_BRANCH):
        proj = _bdot(o_ref[:, n * BRANCH_W:(n + 1) * BRANCH_W], wbr_ref[n])
        gate = jax.nn.sigmoid(_bdot(hb_ref[...], wg_ref[:, n * D_MODEL:(n + 1) * D_MODEL]))
        term = gate * proj
        acc = term if acc is None else acc + term
    y = x + _bdot(acc.astype(BF16), wout_ref[...])
    if final:
        y = _rmsnorm(y, fg_ref[...])
    y_ref[...] = y


def _merge(x, g, o, wg, wbr, wout, fg, final, tm):
    m = x.shape[0]
    resident = lambda shape: _resident(shape, lambda i: (0,) * len(shape))
    return pl.pallas_call(
        functools.partial(_merge_kernel, final=final),
        grid=(m // tm,),
        in_specs=[
            pl.BlockSpec((tm, D_MODEL), lambda i: (i, 0)),
            resident((1, D_MODEL)),
            pl.BlockSpec((tm, N_BRANCH * BRANCH_W), lambda i: (i, 0)),
            resident((D_MODEL, N_GATE)),
            resident((N_BRANCH, BRANCH_W, D_MODEL)),
            resident((D_MODEL, D_MODEL)),
            resident((1, D_MODEL)),
        ],
        out_specs=pl.BlockSpec((tm, D_MODEL), lambda i: (i, 0)),
        out_shape=jax.ShapeDtypeStruct((m, D_MODEL), F32),
        scratch_shapes=[pltpu.VMEM((tm, D_MODEL), BF16)],
        compiler_params=pltpu.CompilerParams(
            dimension_semantics=("arbitrary",), vmem_limit_bytes=VMEM_LIMIT),
        name="merge",
    )(x, g, o, wg, wbr, wout, fg)


def _tile_masks(sample):
    row = lax.broadcasted_iota(jnp.int32, (ROWS, ROWS), 0)
    col = lax.broadcasted_iota(jnp.int32, (ROWS, ROWS), 1)
    if not sample:
        return row >= col, None
    same = (row & (SAMPLE_SEQS - 1)) == (col & (SAMPLE_SEQS - 1))
    return jnp.logical_and(same, row >= col), same


def _ml_prelims(pi, pf, bi, bfo, cmask, smask):
    ai = pi + bi
    lf = _log_sigmoid(pf + bfo)
    tri = jnp.where(cmask, 1.0, 0.0).astype(BF16)
    if smask is None:
        bones = jnp.ones((ROWS, ROWS), BF16)
    else:
        bones = jnp.where(smask, 1.0, 0.0).astype(BF16)
    fcum = _exact_mm(tri, lf)
    ftot = _exact_mm(bones, lf)
    g = (ftot - fcum) + ai
    return dict(ai_t=ai.T, fcum=fcum, fcum_t=fcum.T, ftot=ftot, g=g, g_t=g.T, bones=bones)


def _ml_gates(pre, col, m_prev_c, cmask, smask):
    f_c = pre["fcum"][:, col:col + 1]
    f_r = pre["fcum_t"][col:col + 1, :]
    ig_r = pre["ai_t"][col:col + 1, :]
    g_c = pre["g"][:, col:col + 1]
    g_r = pre["g_t"][col:col + 1, :]
    ftot_c = pre["ftot"][:, col:col + 1]
    logw = jnp.where(cmask, (f_c - f_r) + ig_r, -jnp.inf)
    b = f_c + m_prev_c
    m_t = jnp.maximum(jnp.max(logw, axis=1, keepdims=True), b)
    wts = jnp.exp(logw - m_t)
    inter = jnp.exp(b - m_t)
    if smask is None:
        gmax = jnp.max(g_r, axis=1, keepdims=True)
    else:
        gmax = jnp.max(jnp.where(smask, g_r, -jnp.inf), axis=1, keepdims=True)
    m_new = jnp.maximum(ftot_c + m_prev_c, gmax)
    ws = jnp.exp(g_c - m_new)
    dec = jnp.exp((ftot_c + m_prev_c) - m_new)
    return dict(m_t=m_t, wts=wts, inter=inter, m_new=m_new, ws=ws, dec=dec)


def _ml_output(gt, qb, kb, vb, inter_term, qn):
    s = _bdot_nt(qb, kb) * gt["wts"]
    num = _bdot(s.astype(BF16), vb) + inter_term * gt["inter"]
    den = jnp.sum(s, axis=1, keepdims=True) + qn * gt["inter"]
    return num / jnp.maximum(jnp.abs(den), jnp.exp(-gt["m_t"]))


def _branch_c_post(conv, c_z, cb, lng, lnb):
    y = conv + cb
    mu = jnp.mean(y, axis=-1, keepdims=True)
    yc = y - mu
    var = jnp.mean(yc * yc, axis=-1, keepdims=True)
    ln = (yc * lax.rsqrt(var + EPS)) * lng + lnb
    return _silu(c_z) * _silu(ln)


_GROUPS = ((0, 2048), (2048, 4096), (4096, 5632), (5632, N_MAIN))


def _mix_chunk(hb_ref, w_ref, wif_ref, or0, o_ref, tab, prm, scr, cdec):
    c2_ref, s2_ref, decay_ref, qdec_ref, kdec_ref = tab
    wa_ref, wc_ref, cb_ref, lng_ref, lnb_ref, rgn_ref, mgn_ref, bi_ref, bf_ref = prm
    exta, extc, shc, s_scr, c_scr, n_scr, m_scr = scr
    orows = slice(or0, or0 + ROWS)
    projected = {}

    def pcol(name, lo=0, w=BRANCH_W):
        c0 = _COL[name]
        g0, g1 = next(g for g in _GROUPS if g[0] <= c0 < g[1])
        if g0 not in projected:
            projected[g0] = _bdot(hb_ref[...], w_ref[:, g0:g1])
        return projected[g0][:, c0 - g0 + lo:c0 - g0 + lo + w]

    ua = pcol("a_c") * pcol("a_u")
    exta[SUBLANES:SUBLANES + ROWS, :] = ua
    conv_a = (wa_ref[0:1, :] * exta[SUBLANES - 2:SUBLANES - 2 + ROWS, :]
              + wa_ref[1:2, :] * exta[SUBLANES - 1:SUBLANES - 1 + ROWS, :] + wa_ref[2:3, :] * ua)
    o_a = _silu(pcol("a_z")) * (pcol("a_b") * conv_a)
    o_ref[orows, 0:512] = o_a.astype(BF16)
    exta[0:SUBLANES, :] = exta[ROWS:ROWS + SUBLANES, :]

    extc[32:32 + ROWS, :] = pcol("glu_v") * jax.nn.sigmoid(pcol("glu_g"))
    span = ROWS + 32 - SUBLANES
    conv_blocks = []
    for lb in range(BRANCH_W // 128):
        lanes = slice(lb * 128, (lb + 1) * 128)
        sh = shc.at[lb % 2]
        for ph in range(1, SUBLANES):
            sh[ph - 1] = extc[ph:ph + span, lanes]
        acc = None
        for ph in range(SUBLANES):
            for a in range(span // SUBLANES - ROWS // SUBLANES + 1):
                j = a * SUBLANES + ph - 2
                if 0 <= j < CONV_C_W:
                    arows = slice(a * SUBLANES, a * SUBLANES + ROWS)
                    shifted = extc[arows, lanes] if ph == 0 else sh[ph - 1, arows, :]
                    term = wc_ref[j:j + 1, lanes] * shifted
                    acc = term if acc is None else acc + term
        conv_blocks.append(acc + wc_ref[CONV_C_W - 1:CONV_C_W, lanes] * extc[32:32 + ROWS, lanes])
    conv_c = jnp.concatenate(conv_blocks, axis=1)
    o_c = _branch_c_post(conv_c, pcol("c_z"), cb_ref[...], lng_ref[...], lnb_ref[...])
    o_ref[orows, 1024:1536] = o_c.astype(BF16)
    extc[0:32, :] = extc[ROWS:ROWS + 32, :]

    c2 = c2_ref[orows, :]
    s2 = s2_ref[orows, :]
    for h in range(HEADS):
        lo = h * HEAD_D
        hs = slice(lo, lo + HEAD_D)
        q = _rotary(pcol("r_q", lo, HEAD_D), c2, s2)
        k = _rotary(pcol("r_k", lo, HEAD_D), c2, s2) * (HEAD_D ** -0.5)
        qb = q.astype(BF16)
        kb = k.astype(BF16)
        vb = pcol("r_v", lo, HEAD_D).astype(BF16)
        s_old = s_scr[h]
        sc = _bdot_nt(qb, kb) * decay_ref[h]
        ret = _bdot(sc.astype(BF16), vb) + _bdot(qb, s_old.astype(BF16)) * qdec_ref[:, hs]
        s_scr[h] = s_old * cdec[h] + _bdot((k * kdec_ref[:, hs]).T.astype(BF16), vb)
        o_r = _silu(pcol("r_z", lo, HEAD_D)) * (_head_norm(ret) * rgn_ref[:, hs])
        o_ref[orows, 512 + lo:512 + lo + HEAD_D] = o_r.astype(BF16)

    cmask, smask = _tile_masks(False)
    pif = _bdot(hb_ref[...], wif_ref[...])
    pre = _ml_prelims(pif[:, 0:128], pif[:, 128:256], bi_ref[...], bf_ref[...], cmask, smask)
    lane = lax.broadcasted_iota(jnp.int32, (ROWS, ROWS), 1)
    m_rows = m_scr[...]
    m_rows_new = jnp.zeros((ROWS, ROWS), F32)
    for h in range(HEADS):
        lo = h * HEAD_D
        hs = slice(lo, lo + HEAD_D)
        gt = _ml_gates(pre, h, m_rows[:, h:h + 1], cmask, smask)
        q = pcol("m_q", lo, HEAD_D)
        k = pcol("m_k", lo, HEAD_D) * (HEAD_D ** -0.5)
        qb = q.astype(BF16)
        kb = k.astype(BF16)
        vb = pcol("m_v", lo, HEAD_D).astype(BF16)
        c_old = c_scr[h]
        n_old = n_scr[:, hs]
        qn = jnp.sum(q * n_old, axis=1, keepdims=True)
        hout = _ml_output(gt, qb, kb, vb, _bdot(qb, c_old.astype(BF16)), qn)
        kw = k * gt["ws"]
        dec_full = jnp.broadcast_to(gt["dec"], (ROWS, ROWS))
        c_scr[h] = c_old * dec_full + _bdot(kw.T.astype(BF16), vb)
        n_scr[:, hs] = n_old * dec_full + _bdot(pre["bones"], kw.astype(BF16))
        m_rows_new = jnp.where(lane == h, jnp.broadcast_to(gt["m_new"], (ROWS, ROWS)), m_rows_new)
        hm = jax.nn.sigmoid(pcol("m_o", lo, HEAD_D)) * hout
        o_m = _silu(pcol("m_z", lo, HEAD_D)) * (_head_norm(hm) * mgn_ref[:, hs])
        o_ref[orows, 1536 + lo:1536 + lo + HEAD_D] = o_m.astype(BF16)
    m_scr[...] = m_rows_new


def _xmix_kernel(x_ref, g_ref, w_ref, wif_ref,
                 c2_ref, s2_ref, decay_ref, qdec_ref, kdec_ref,
                 wa_ref, wc_ref, cb_ref, lng_ref, lnb_ref, rgn_ref, mgn_ref, bi_ref, bf_ref,
                 o_ref, bufa_ref, sret_ref, bufc_ref, cml_ref, nml_ref, mml_ref,
                 hb_scr, exta, extc, shc, s_scr, c_scr, n_scr, m_scr,
                 *, cdec, steps_per_seq):
    s = pl.program_id(0)
    seq_step = s % steps_per_seq
    tab = (c2_ref, s2_ref, decay_ref, qdec_ref, kdec_ref)
    prm = (wa_ref, wc_ref, cb_ref, lng_ref, lnb_ref, rgn_ref, mgn_ref, bi_ref, bf_ref)
    scr = (exta, extc, shc, s_scr, c_scr, n_scr, m_scr)
    nchunk = STEP_ROWS // ROWS

    @pl.when(seq_step == 0)
    def _():
        exta[0:SUBLANES, :] = jnp.zeros((SUBLANES, BRANCH_W), F32)
        extc[0:32, :] = jnp.zeros((32, BRANCH_W), F32)
        s_scr[...] = jnp.zeros_like(s_scr)
        c_scr[...] = jnp.zeros_like(c_scr)
        n_scr[...] = jnp.zeros_like(n_scr)
        m_scr[...] = jnp.zeros_like(m_scr)

    for ci in range(nchunk):
        hb = hb_scr.at[ci % 2]
        hb[...] = _rmsnorm(x_ref[ci * ROWS:(ci + 1) * ROWS, :], g_ref[...]).astype(BF16)
        _mix_chunk(hb, w_ref, wif_ref, ci * ROWS, o_ref, tab, prm, scr, cdec)

    @pl.when(seq_step == steps_per_seq - 1)
    def _():
        bufa_ref[0] = exta[SUBLANES - (CONV_A_W - 1):SUBLANES, :]
        bufc_ref[0] = extc[32 - (CONV_C_W - 1):32, :]
        for h in range(HEADS):
            sret_ref[0, h] = s_scr[h]
            cml_ref[0, h] = c_scr[h]
            nml_ref[0, h:h + 1, :] = n_scr[0:1, h * HEAD_D:(h + 1) * HEAD_D]
        mml_ref[0] = m_scr[0:1, :]


def _xmix_prompt(x, gnorm, w_all, wif, layer, consts, prm, batch, seq):
    steps_per_seq = seq // STEP_ROWS
    nstep = batch * steps_per_seq
    c1 = lambda shape: _resident(shape, lambda s: (0,) * len(shape))
    rope = pl.BlockSpec((STEP_ROWS, HEAD_D), lambda s: (s % steps_per_seq, 0))
    in_specs = [
        pl.BlockSpec((STEP_ROWS, D_MODEL), lambda s: (s, 0)),
        c1((1, D_MODEL)),
        _resident((None, D_MODEL, W_BLOCK), lambda s: (layer, 0, 0)),
        c1((D_MODEL, N_IF)),
        rope, rope,
        c1((HEADS, ROWS, ROWS)),
        c1((ROWS, BRANCH_W)),
        c1((ROWS, BRANCH_W)),
        c1((CONV_A_W, BRANCH_W)),
        c1((CONV_C_W, BRANCH_W)),
    ] + [c1((1, BRANCH_W))] * 5 + [c1((1, 128))] * 2
    per_b3 = lambda s: (s // steps_per_seq, 0, 0)
    per_b4 = lambda s: (s // steps_per_seq, 0, 0, 0)
    out_specs = [
        pl.BlockSpec((STEP_ROWS, N_BRANCH * BRANCH_W), lambda s: (s, 0)),
        pl.BlockSpec((1, CONV_A_W - 1, BRANCH_W), per_b3),
        pl.BlockSpec((1, HEADS, HEAD_D, HEAD_D), per_b4),
        pl.BlockSpec((1, CONV_C_W - 1, BRANCH_W), per_b3),
        pl.BlockSpec((1, HEADS, HEAD_D, HEAD_D), per_b4),
        pl.BlockSpec((1, HEADS, HEAD_D), per_b3),
        pl.BlockSpec((1, 1, 128), per_b3),
    ]
    out_shape = [
        jax.ShapeDtypeStruct((batch * seq, N_BRANCH * BRANCH_W), BF16),
        jax.ShapeDtypeStruct((batch, CONV_A_W - 1, BRANCH_W), F32),
        jax.ShapeDtypeStruct((batch, HEADS, HEAD_D, HEAD_D), F32),
        jax.ShapeDtypeStruct((batch, CONV_C_W - 1, BRANCH_W), F32),
        jax.ShapeDtypeStruct((batch, HEADS, HEAD_D, HEAD_D), F32),
        jax.ShapeDtypeStruct((batch, HEADS, HEAD_D), F32),
        jax.ShapeDtypeStruct((batch, 1, 128), F32),
    ]
    scratch = [
        pltpu.VMEM((2, ROWS, D_MODEL), BF16),
        pltpu.VMEM((ROWS + SUBLANES, BRANCH_W), F32),
        pltpu.VMEM((ROWS + 32, BRANCH_W), F32),
        pltpu.VMEM((2, SUBLANES - 1, ROWS + 32 - SUBLANES, 128), F32),
        pltpu.VMEM((HEADS, HEAD_D, HEAD_D), F32),
        pltpu.VMEM((HEADS, HEAD_D, HEAD_D), F32),
        pltpu.VMEM((ROWS, BRANCH_W), F32),
        pltpu.VMEM((ROWS, ROWS), F32),
    ]
    return pl.pallas_call(
        functools.partial(_xmix_kernel, cdec=consts["cdec"], steps_per_seq=steps_per_seq),
        grid=(nstep,),
        in_specs=in_specs, out_specs=out_specs, out_shape=out_shape, scratch_shapes=scratch,
        compiler_params=pltpu.CompilerParams(
            dimension_semantics=("arbitrary",), vmem_limit_bytes=VMEM_LIMIT),
        name="xmix_prompt",
    )(x, gnorm, w_all, wif, consts["c2"], consts["s2"], consts["decay"], consts["qdec"],
      consts["kdec"], prm["wa"], prm["wc"], prm["cb"], prm["lng"], prm["lnb"], prm["rgn"],
      prm["mgn"], prm["bi"], prm["bf"])


def _mix_sample_kernel(pa_ref, pc_ref, rq_ref, rk_ref, rv_ref, rz_ref, mq_ref, mk_ref, mv_ref,
                       mo_ref, mz_ref, pif_ref, c2_ref, s2_ref, decay_ref, qdec_ref, kdec_ref,
                       cdec_ref, wa_ref, wc_ref, cb_ref, lng_ref, lnb_ref, rgn_ref, mgn_ref,
                       bi_ref, bf_ref, bufa_ref, sret_ref, bufc_ref, cml_ref, nrows_ref, mrows_ref,
                       o_ref, bufa_o, sret_o, bufc_o, cml_o, nrows_o, mrows_o, dec_scr):
    h = pl.program_id(1)
    sq = SAMPLE_SEQS
    nt = ROWS // sq

    @pl.when(h == 0)
    def _():
        ua = pa_ref[:, 512:1024] * pa_ref[:, 1024:1536]
        ea = [bufa_ref[:, 0, :], bufa_ref[:, 1, :]] + [ua[t * sq:(t + 1) * sq, :] for t in range(nt)]
        conv_a = jnp.concatenate(
            [wa_ref[0:1, :] * ea[t] + wa_ref[1:2, :] * ea[t + 1] + wa_ref[2:3, :] * ea[t + 2]
             for t in range(nt)], axis=0)
        o_ref[:, 0:512] = (_silu(pa_ref[:, 1536:2048]) * (pa_ref[:, 0:512] * conv_a)).astype(BF16)
        bufa_o[:, 0, :] = ea[nt]
        bufa_o[:, 1, :] = ea[nt + 1]
        uc = pc_ref[:, 0:512] * jax.nn.sigmoid(pc_ref[:, 512:1024])
        nb = CONV_C_W - 1
        ec = [bufc_ref[:, i, :] for i in range(nb)] + [uc[t * sq:(t + 1) * sq, :] for t in range(nt)]
        planes = []
        for t in range(nt):
            acc = wc_ref[0:1, :] * ec[t]
            for j in range(1, CONV_C_W):
                acc = acc + wc_ref[j:j + 1, :] * ec[t + j]
            planes.append(acc)
        conv_c = jnp.concatenate(planes, axis=0)
        o_ref[:, 1024:1536] = _branch_c_post(conv_c, pc_ref[:, 1024:1536], cb_ref[...], lng_ref[...],
                                             lnb_ref[...]).astype(BF16)
        for i in range(nb):
            bufc_o[:, i, :] = ec[i + nt]

    cmask, smask = _tile_masks(True)
    rowseq = lax.broadcasted_iota(jnp.int32, (ROWS, ROWS), 0) & (sq - 1)
    laneseq = lax.broadcasted_iota(jnp.int32, (ROWS, ROWS), 1) & (sq - 1)

    c2 = c2_ref[...]
    s2 = s2_ref[...]
    rq = _rotary(rq_ref[...], c2, s2)
    rk = _rotary(rk_ref[...], c2, s2) * (HEAD_D ** -0.5)
    rqb = rq.astype(BF16)
    rkb = rk.astype(BF16)
    rvb = rv_ref[...].astype(BF16)
    rkd_t = (rk * kdec_ref[...]).T
    cdec = cdec_ref[0, 0:1, :]

    pre = _ml_prelims(pif_ref[:, 0:128], pif_ref[:, 128:256], bi_ref[...], bf_ref[...], cmask, smask)
    m_prev = mrows_ref[...]
    gt = _ml_gates(pre, 0, m_prev[:, 0:1], cmask, smask)
    mq = mq_ref[...]
    mk = mk_ref[...] * (HEAD_D ** -0.5)
    mqb = mq.astype(BF16)
    mkb = mk.astype(BF16)
    mvb = mv_ref[...].astype(BF16)
    kw = mk * gt["ws"]
    kw_t = kw.T
    dec_full = jnp.broadcast_to(gt["dec"], (ROWS, ROWS))
    dec_scr[...] = dec_full

    def per_seq(b, carry):
        inter_r, inter_m = carry
        sb = sret_ref[b, 0]
        inter_r = jnp.where(rowseq == b, _bdot(rqb, sb.astype(BF16)), inter_r)
        k_sel = jnp.where(laneseq == b, rkd_t, 0.0).astype(BF16)
        sret_o[b, 0] = sb * cdec + _bdot(k_sel, rvb)
        cb_ = cml_ref[b, 0]
        inter_m = jnp.where(rowseq == b, _bdot(mqb, cb_.astype(BF16)), inter_m)
        kw_sel = jnp.where(laneseq == b, kw_t, 0.0).astype(BF16)
        cml_o[b, 0] = cb_ * dec_scr[pl.ds(b, 1), :] + _bdot(kw_sel, mvb)
        return inter_r, inter_m

    zero = jnp.zeros((ROWS, HEAD_D), F32)
    inter_r, inter_m = lax.fori_loop(0, sq, per_seq, (zero, zero), unroll=8)

    sc = _bdot_nt(rqb, rkb) * decay_ref[0]
    ret = _bdot(sc.astype(BF16), rvb) + inter_r * qdec_ref[...]
    o_r = (_silu(rz_ref[...]) * (_head_norm(ret) * rgn_ref[...])).astype(BF16)

    n_old = nrows_ref[...]
    qn = jnp.sum(mq * n_old, axis=1, keepdims=True)
    hout = _ml_output(gt, mqb, mkb, mvb, inter_m, qn)
    nrows_o[...] = n_old * dec_full + _bdot(pre["bones"], kw.astype(BF16))
    mrows_o[...] = jnp.broadcast_to(gt["m_new"], (ROWS, ROWS))
    hm = jax.nn.sigmoid(mo_ref[...]) * hout
    o_m = (_silu(mz_ref[...]) * (_head_norm(hm) * mgn_ref[...])).astype(BF16)

    for hh in range(HEADS):
        @pl.when(h == hh)
        def _(hh=hh):
            o_ref[:, 512 + hh * HEAD_D:512 + (hh + 1) * HEAD_D] = o_r
            o_ref[:, 1536 + hh * HEAD_D:1536 + (hh + 1) * HEAD_D] = o_m


def _mix_sample(p, pif, consts, prm, st, nseq, layer, depth, stacked):
    ntile = nseq // SAMPLE_SEQS
    sq = SAMPLE_SEQS
    head_cols = lambda name: pl.BlockSpec(
        (ROWS, HEAD_D), lambda i, h, _o=_COL[name] // HEAD_D: (i, _o + h))
    const2 = lambda i, h: (0, 0)
    per_head2 = lambda i, h: (0, h)
    conv_a_state = pl.BlockSpec((None, sq, CONV_A_W - 1, BRANCH_W), lambda i, h: (layer, i, 0, 0))
    conv_c_state = pl.BlockSpec((None, sq, CONV_C_W - 1, BRANCH_W), lambda i, h: (layer, i, 0, 0))
    matrix_state = pl.BlockSpec((None, sq, 1, HEAD_D, HEAD_D), lambda i, h: (layer, i, h, 0, 0))
    in_specs = [
        pl.BlockSpec((ROWS, 2048), lambda i, h: (i, 0)),
        pl.BlockSpec((ROWS, 2048), lambda i, h: (i, 2)),
    ] + [head_cols(n) for n in ("r_q", "r_k", "r_v", "r_z", "m_q", "m_k", "m_v", "m_o", "m_z")] + [
        pl.BlockSpec((ROWS, 256), lambda i, h: (i, h)),
        pl.BlockSpec((ROWS, HEAD_D), const2),
        pl.BlockSpec((ROWS, HEAD_D), const2),
        pl.BlockSpec((1, ROWS, ROWS), lambda i, h: (h, 0, 0)),
        pl.BlockSpec((ROWS, HEAD_D), per_head2),
        pl.BlockSpec((ROWS, HEAD_D), per_head2),
        pl.BlockSpec((1, 8, 128), lambda i, h: (h, 0, 0)),
        pl.BlockSpec((CONV_A_W, BRANCH_W), const2),
        pl.BlockSpec((CONV_C_W, BRANCH_W), const2),
        pl.BlockSpec((1, BRANCH_W), const2),
        pl.BlockSpec((1, BRANCH_W), const2),
        pl.BlockSpec((1, BRANCH_W), const2),
        pl.BlockSpec((1, HEAD_D), per_head2),
        pl.BlockSpec((1, HEAD_D), per_head2),
        pl.BlockSpec((1, 128), per_head2),
        pl.BlockSpec((1, 128), per_head2),
        conv_a_state, matrix_state, conv_c_state, matrix_state,
        pl.BlockSpec((ROWS, HEAD_D), lambda i, h: (i, h)),
        pl.BlockSpec((ROWS, HEAD_D), lambda i, h: (i, h)),
    ]
    out_specs = [
        pl.BlockSpec((ROWS, N_BRANCH * BRANCH_W), lambda i, h: (i, 0)),
        conv_a_state, matrix_state, conv_c_state, matrix_state,
        pl.BlockSpec((ROWS, HEAD_D), lambda i, h: (i, h)),
        pl.BlockSpec((ROWS, HEAD_D), lambda i, h: (i, h)),
    ]
    nrow = ntile * ROWS
    out_shape = [
        jax.ShapeDtypeStruct((nrow, N_BRANCH * BRANCH_W), BF16),
        jax.ShapeDtypeStruct((depth, nseq, CONV_A_W - 1, BRANCH_W), F32),
        jax.ShapeDtypeStruct((depth, nseq, HEADS, HEAD_D, HEAD_D), F32),
        jax.ShapeDtypeStruct((depth, nseq, CONV_C_W - 1, BRANCH_W), F32),
        jax.ShapeDtypeStruct((depth, nseq, HEADS, HEAD_D, HEAD_D), F32),
        jax.ShapeDtypeStruct((nrow, HEADS * HEAD_D), F32),
        jax.ShapeDtypeStruct((nrow, HEADS * HEAD_D), F32),
    ]
    args = [p] * 11 + [
        pif, consts["c2"], consts["s2"], consts["decay"], consts["qdec"], consts["kdec"],
        consts["cdec"], prm["wa"], prm["wc"], prm["cb"], prm["lng"], prm["lnb"], prm["rgn"],
        prm["mgn"], prm["bi_h"], prm["bf_h"],
        st["bufa"], st["sret"], st["bufc"], st["cml"], st["nrows"], st["mrows"]]
    assert len(args) == len(in_specs)
    n_in = len(args)
    aliases = {}
    if stacked is not None:
        aliases = {n_in + k: 1 + k for k in range(4)}
        in_specs = in_specs + [pl.BlockSpec(memory_space=pl.ANY)] * 4
        args = args + list(stacked)

    def entry(*refs):
        _mix_sample_kernel(*refs[:n_in], *refs[n_in + len(aliases):])

    return pl.pallas_call(
        entry,
        grid=(ntile, HEADS),
        in_specs=in_specs, out_specs=out_specs, out_shape=out_shape,
        scratch_shapes=[pltpu.VMEM((ROWS, ROWS), F32)],
        input_output_aliases=aliases,
        compiler_params=pltpu.CompilerParams(
            dimension_semantics=("arbitrary", "arbitrary"), vmem_limit_bytes=VMEM_LIMIT),
        name="mix_sample",
    )(*args)


def _rope_tables(pos):
    inv = ROPE_BASE ** (-jnp.arange(0, HEAD_D, 2, dtype=F32) / HEAD_D)
    ang = pos.astype(F32)[:, None] * inv[None, :]
    cos = jnp.cos(ang)
    sin = jnp.sin(ang)
    return jnp.concatenate([cos, cos], axis=1), jnp.concatenate([-sin, sin], axis=1)


def _decay_tables(t_of_row, seq_of_row, chunk_len):
    log_g = np.log1p(-(2.0 ** (-5.0 - np.arange(HEADS, dtype=np.float64))))
    dt = t_of_row[:, None] - t_of_row[None, :]
    ok = (seq_of_row[:, None] == seq_of_row[None, :]) & (dt >= 0)
    decay = np.where(ok[None], np.exp(np.maximum(dt, 0)[None] * log_g[:, None, None]), 0.0)
    qdec = np.exp((t_of_row[:, None] + 1.0) * log_g[None, :])
    kdec = np.exp((chunk_len - 1.0 - t_of_row)[:, None] * log_g[None, :])
    cdec = np.exp(chunk_len * log_g)
    rep = lambda a: jnp.asarray(np.repeat(a, HEAD_D, axis=1), F32)
    return jnp.asarray(decay, F32), rep(qdec), rep(kdec), cdec


def kernel(x_prompt, x_sample, state_conv_a, state_ret, state_conv_c, state_mlstm_C,
           state_mlstm_n, state_mlstm_m, norm_g, w_in, conv_a_w, conv_c_w, conv_c_b,
           ln_c_g, ln_c_b, ret_gn_g, ml_gn_g, ml_gate_b, w_br, w_out, final_g):
    bp, tp, _ = x_prompt.shape
    bs, ts, _ = x_sample.shape
    depth = w_in.shape[0]
    past_len = 16384
    sq = SAMPLE_SEQS
    ntile = bs // sq
    assert tp % STEP_ROWS == 0 and ts * sq == ROWS and bs % sq == 0

    r = np.arange(ROWS)
    c2p, s2p = _rope_tables(jnp.arange(tp, dtype=jnp.int32))
    decay_p, qdec_p, kdec_p, cdec_p = _decay_tables(r.astype(np.float64), np.zeros(ROWS), float(ROWS))
    consts_p = dict(c2=c2p, s2=s2p, decay=decay_p, qdec=qdec_p, kdec=kdec_p,
                    cdec=tuple(float(v) for v in cdec_p))
    t_s = (r // sq).astype(np.float64)
    c2s, s2s = _rope_tables(past_len + jnp.asarray(r // sq, jnp.int32))
    decay_s, qdec_s, kdec_s, cdec_s = _decay_tables(t_s, r % sq, float(ts))
    cdec_s_arr = jnp.asarray(np.broadcast_to(cdec_s[:, None, None], (HEADS, 8, 128)), F32)
    consts_s = dict(c2=c2s, s2=s2s, decay=decay_s, qdec=qdec_s, kdec=kdec_s, cdec=cdec_s_arr)

    xp = x_prompt.reshape(bp * tp, D_MODEL)
    xs = x_sample.reshape(ntile, sq, ts, D_MODEL).transpose(0, 2, 1, 3).reshape(bs * ts, D_MODEL)

    fg = final_g.reshape(1, D_MODEL)
    outs_p = [[] for _ in range(6)]
    n_s, m_s = [], []
    stacked = None
    sel_p = np.zeros((2 * HEADS, N_IF), np.float32)
    sel_s = np.zeros((2 * HEADS, HEADS * N_IF), np.float32)
    for h in range(HEADS):
        sel_p[h, h] = 1.0
        sel_p[HEADS + h, 128 + h] = 1.0
        sel_s[h, h * N_IF] = 1.0
        sel_s[HEADS + h, h * N_IF + 128] = 1.0
    w_bf = w_in[:, :, :W_BLOCK].astype(BF16)
    pad_row = lambda v: jnp.pad(v, (0, 128 - v.shape[0])).reshape(1, 128)
    for l in range(depth):
        w_if = w_in[l, :, N_MAIN:N_MAIN + 2 * HEADS]
        spread = lambda sel: jnp.dot(w_if, jnp.asarray(sel), precision=lax.Precision.HIGHEST).astype(BF16)
        w_gate = w_in[l, :, N_MAIN + 2 * HEADS:].astype(BF16)
        gb = ml_gate_b[l]
        prm = dict(
            wa=conv_a_w[l], wc=conv_c_w[l], cb=conv_c_b[l].reshape(1, -1),
            lng=ln_c_g[l].reshape(1, -1), lnb=ln_c_b[l].reshape(1, -1),
            rgn=ret_gn_g[l].reshape(1, -1), mgn=ml_gn_g[l].reshape(1, -1),
            bi=pad_row(gb[:HEADS]), bf=pad_row(gb[HEADS:]),
            bi_h=jnp.concatenate([pad_row(gb[h:h + 1]) for h in range(HEADS)], axis=1),
            bf_h=jnp.concatenate([pad_row(gb[HEADS + h:HEADS + h + 1]) for h in range(HEADS)], axis=1),
        )
        gnorm = norm_g[l].reshape(1, D_MODEL)
        wbr = w_br[l].astype(BF16)
        wout = w_out[l].astype(BF16)
        final = l == depth - 1

        res = _xmix_prompt(xp, gnorm, w_bf, spread(sel_p), l, consts_p, prm, bp, tp)
        for k in range(5):
            outs_p[k].append(res[1 + k])
        outs_p[5].append(res[6][:, 0, :HEADS])
        xp = _merge(xp, gnorm, res[0], w_gate, wbr, wout, fg, final, tm=512)

        ps, pifs = _inproj(xs, gnorm, w_bf, spread(sel_s), l, tm=bs * ts, tn=2048)
        n_rows = jnp.broadcast_to(
            state_mlstm_n[l].reshape(ntile, 1, sq, HEADS * HEAD_D),
            (ntile, ts, sq, HEADS * HEAD_D)).reshape(bs * ts, HEADS * HEAD_D)
        m_rows = jnp.broadcast_to(
            state_mlstm_m[l].reshape(ntile, 1, sq, HEADS, 1),
            (ntile, ts, sq, HEADS, HEAD_D)).reshape(bs * ts, HEADS * HEAD_D)
        st = dict(bufa=state_conv_a, sret=state_ret, bufc=state_conv_c, cml=state_mlstm_C,
                  nrows=n_rows, mrows=m_rows)
        res = _mix_sample(ps, pifs, consts_s, prm, st, bs, l, depth, stacked)
        stacked = tuple(res[1:5])
        n_s.append(res[5].reshape(ntile, ts, sq, HEADS, HEAD_D)[:, 0].reshape(bs, HEADS, HEAD_D))
        m_s.append(res[6].reshape(ntile, ts, sq, HEADS, HEAD_D)[:, 0, :, :, 0].reshape(bs, HEADS))
        xs = _merge(xs, gnorm, res[0], w_gate, wbr, wout, fg, final, tm=bs * ts)

    y_prompt = xp.reshape(bp, tp, D_MODEL)
    y_sample = xs.reshape(ntile, ts, sq, D_MODEL).transpose(0, 2, 1, 3).reshape(bs, ts, D_MODEL)
    sp = [jnp.stack(a, axis=0) for a in outs_p]
    return (y_prompt, y_sample, sp[0], stacked[0], sp[1], stacked[1], sp[2], stacked[2],
            sp[3], stacked[3], sp[4], jnp.stack(n_s, axis=0), sp[5], jnp.stack(m_s, axis=0))
```

```python
import functools

import numpy as np
import jax
import jax.numpy as jnp
from jax import lax
from jax.experimental import pallas as pl
from jax.experimental.pallas import tpu as pltpu

F32 = jnp.float32
BF16 = jnp.bfloat16

D_MODEL = 1024
BRANCH_W = 512
N_BRANCH = 4
HEADS = 4
HEAD_D = 128
CONV_A_W = 3
CONV_C_W = 31
ROPE_BASE = 10000.0
EPS = 1e-6
N_MAIN = 16 * BRANCH_W
N_IF = 256
N_GATE = N_BRANCH * D_MODEL
ROWS = 128
STEP_ROWS = 4 * ROWS
W_BLOCK = N_MAIN + 512
SAMPLE_SEQS = 32
SUBLANES = 8
VMEM_LIMIT = 56 * 1024 * 1024

_COL = dict(a_b=0, a_c=512, a_u=1024, a_z=1536, r_q=2048, r_k=2560, r_v=3072, r_z=3584,
            glu_v=4096, glu_g=4608, c_z=5120, m_q=5632, m_k=6144, m_v=6656, m_o=7168, m_z=7680)


def _bdot(a, b):
    return jnp.dot(a, b, preferred_element_type=F32)


def _bdot_nt(a, b):
    return lax.dot_general(a, b, (((1,), (1,)), ((), ())), preferred_element_type=F32)


def _exact_mm(mat_b16, x):
    hi = x.astype(BF16)
    r1 = x - hi.astype(F32)
    mid = r1.astype(BF16)
    lo = (r1 - mid.astype(F32)).astype(BF16)
    return _bdot(mat_b16, hi) + _bdot(mat_b16, mid) + _bdot(mat_b16, lo)


def _silu(x):
    return x * jax.nn.sigmoid(x)


def _log_sigmoid(x):
    return jnp.minimum(x, 0.0) - jnp.log1p(jnp.exp(-jnp.abs(x)))


def _head_norm(y):
    mu = jnp.mean(y, axis=-1, keepdims=True)
    yc = y - mu
    var = jnp.mean(yc * yc, axis=-1, keepdims=True)
    return yc * lax.rsqrt(var + EPS)


def _rotary(x, c2, s2):
    return x * c2 + pltpu.roll(x, HEAD_D // 2, 1) * s2


def _rmsnorm(x, g):
    ms = jnp.mean(x * x, axis=-1, keepdims=True)
    return (x * lax.rsqrt(ms + EPS)) * g


def _resident(shape, imap):
    return pl.BlockSpec(shape, imap, pipeline_mode=pl.Buffered(1))


def _inproj_kernel(x_ref, g_ref, w_ref, wif_ref, p_ref, pif_ref, hb_ref):
    @pl.when(pl.program_id(1) == 0)
    def _():
        hb = _rmsnorm(x_ref[...], g_ref[...]).astype(BF16)
        hb_ref[...] = hb
        pif_ref[...] = _bdot(hb, wif_ref[...])

    p_ref[...] = _bdot(hb_ref[...], w_ref[...])


def _inproj(x, g, w_all, wif, layer, tm, tn):
    m = x.shape[0]
    nif = wif.shape[1]
    return pl.pallas_call(
        _inproj_kernel,
        grid=(m // tm, N_MAIN // tn),
        in_specs=[
            pl.BlockSpec((tm, D_MODEL), lambda i, j: (i, 0)),
            pl.BlockSpec((1, D_MODEL), lambda i, j: (0, 0)),
            pl.BlockSpec((None, D_MODEL, tn), lambda i, j: (layer, 0, j)),
            pl.BlockSpec((D_MODEL, nif), lambda i, j: (0, 0)),
        ],
        out_specs=[
            pl.BlockSpec((tm, tn), lambda i, j: (i, j)),
            pl.BlockSpec((tm, nif), lambda i, j: (i, 0)),
        ],
        out_shape=[jax.ShapeDtypeStruct((m, N_MAIN), F32), jax.ShapeDtypeStruct((m, nif), F32)],
        scratch_shapes=[pltpu.VMEM((tm, D_MODEL), BF16)],
        compiler_params=pltpu.CompilerParams(
            dimension_semantics=("arbitrary", "arbitrary"), vmem_limit_bytes=VMEM_LIMIT),
        name="inproj",
    )(x, g, w_all, wif)


def _merge_kernel(x_ref, g_ref, o_ref, wg_ref, wbr_ref, wout_ref, fg_ref, y_ref, hb_ref, *, final):
    x = x_ref[...]
    hb_ref[...] = _rmsnorm(x, g_ref[...]).astype(BF16)
    acc = None
    for n in range(N_BRANCH):
        proj = _bdot(o_ref[:, n * BRANCH_W:(n + 1) * BRANCH_W], wbr_ref[n])
        gate = jax.nn.sigmoid(_bdot(hb_ref[...], wg_ref[:, n * D_MODEL:(n + 1) * D_MODEL]))
        term = gate * proj
        acc = term if acc is None else acc + term
    y = x + _bdot(acc.astype(BF16), wout_ref[...])
    if final:
        y = _rmsnorm(y, fg_ref[...])
    y_ref[...] = y


def _merge(x, g, o, wg, wbr, wout, fg, final, tm):
    m = x.shape[0]
    resident = lambda shape: _resident(shape, lambda i: (0,) * len(shape))
    return pl.pallas_call(
        functools.partial(_merge_kernel, final=final),
        grid=(m // tm,),
        in_specs=[
            pl.BlockSpec((tm, D_MODEL), lambda i: (i, 0)),
            resident((1, D_MODEL)),
            pl.BlockSpec((tm, N_BRANCH * BRANCH_W), lambda i: (i, 0)),
            resident((D_MODEL, N_GATE)),
            resident((N_BRANCH, BRANCH_W, D_MODEL)),
            resident((D_MODEL, D_MODEL)),
            resident((1, D_MODEL)),
        ],
        out_specs=pl.BlockSpec((tm, D_MODEL), lambda i: (i, 0)),
        out_shape=jax.ShapeDtypeStruct((m, D_MODEL), F32),
        scratch_shapes=[pltpu.VMEM((tm, D_MODEL), BF16)],
        compiler_params=pltpu.CompilerParams(
            dimension_semantics=("arbitrary",), vmem_limit_bytes=VMEM_LIMIT),
        name="merge",
    )(x, g, o, wg, wbr, wout, fg)


def _tile_masks(sample):
    row = lax.broadcasted_iota(jnp.int32, (ROWS, ROWS), 0)
    col = lax.broadcasted_iota(jnp.int32, (ROWS, ROWS), 1)
    if not sample:
        return row >= col, None
    same = (row & (SAMPLE_SEQS - 1)) == (col & (SAMPLE_SEQS - 1))
    return jnp.logical_and(same, row >= col), same


def _ml_prelims(pi, pf, bi, bfo, cmask, smask):
    ai = pi + bi
    lf = _log_sigmoid(pf + bfo)
    tri = jnp.where(cmask, 1.0, 0.0).astype(BF16)
    if smask is None:
        bones = jnp.ones((ROWS, ROWS), BF16)
    else:
        bones = jnp.where(smask, 1.0, 0.0).astype(BF16)
    fcum = _exact_mm(tri, lf)
    ftot = _exact_mm(bones, lf)
    g = (ftot - fcum) + ai
    return dict(ai_t=ai.T, fcum=fcum, fcum_t=fcum.T, ftot=ftot, g=g, g_t=g.T, bones=bones)


def _ml_gates(pre, col, m_prev_c, cmask, smask):
    f_c = pre["fcum"][:, col:col + 1]
    f_r = pre["fcum_t"][col:col + 1, :]
    ig_r = pre["ai_t"][col:col + 1, :]
    g_c = pre["g"][:, col:col + 1]
    g_r = pre["g_t"][col:col + 1, :]
    ftot_c = pre["ftot"][:, col:col + 1]
    logw = jnp.where(cmask, (f_c - f_r) + ig_r, -jnp.inf)
    b = f_c + m_prev_c
    m_t = jnp.maximum(jnp.max(logw, axis=1, keepdims=True), b)
    wts = jnp.exp(logw - m_t)
    inter = jnp.exp(b - m_t)
    if smask is None:
        gmax = jnp.max(g_r, axis=1, keepdims=True)
    else:
        gmax = jnp.max(jnp.where(smask, g_r, -jnp.inf), axis=1, keepdims=True)
    m_new = jnp.maximum(ftot_c + m_prev_c, gmax)
    ws = jnp.exp(g_c - m_new)
    dec = jnp.exp((ftot_c + m_prev_c) - m_new)
    return dict(m_t=m_t, wts=wts, inter=inter, m_new=m_new, ws=ws, dec=dec)


def _ml_output(gt, qb, kb, vb, inter_term, qn):
    s = _bdot_nt(qb, kb) * gt["wts"]
    num = _bdot(s.astype(BF16), vb) + inter_term * gt["inter"]
    den = jnp.sum(s, axis=1, keepdims=True) + qn * gt["inter"]
    return num / jnp.maximum(jnp.abs(den), jnp.exp(-gt["m_t"]))


def _branch_c_post(conv, c_z, cb, lng, lnb):
    y = conv + cb
    mu = jnp.mean(y, axis=-1, keepdims=True)
    yc = y - mu
    var = jnp.mean(yc * yc, axis=-1, keepdims=True)
    ln = (yc * lax.rsqrt(var + EPS)) * lng + lnb
    return _silu(c_z) * _silu(ln)


_GROUPS = ((0, 2048), (2048, 4096), (4096, 5632), (5632, N_MAIN))


def _mix_chunk(hb_ref, w_ref, wif_ref, or0, o_ref, tab, prm, scr, cdec):
    c2_ref, s2_ref, decay_ref, qdec_ref, kdec_ref = tab
    wa_ref, wc_ref, cb_ref, lng_ref, lnb_ref, rgn_ref, mgn_ref, bi_ref, bf_ref = prm
    exta, extc, shc, s_scr, c_scr, n_scr, m_scr = scr
    orows = slice(or0, or0 + ROWS)
    projected = {}

    def pcol(name, lo=0, w=BRANCH_W):
        c0 = _COL[name]
        g0, g1 = next(g for g in _GROUPS if g[0] <= c0 < g[1])
        if g0 not in projected:
            projected[g0] = _bdot(hb_ref[...], w_ref[:, g0:g1])
        return projected[g0][:, c0 - g0 + lo:c0 - g0 + lo + w]

    ua = pcol("a_c") * pcol("a_u")
    exta[SUBLANES:SUBLANES + ROWS, :] = ua
    conv_a = (wa_ref[0:1, :] * exta[SUBLANES - 2:SUBLANES - 2 + ROWS, :]
              + wa_ref[1:2, :] * exta[SUBLANES - 1:SUBLANES - 1 + ROWS, :] + wa_ref[2:3, :] * ua)
    o_a = _silu(pcol("a_z")) * (pcol("a_b") * conv_a)
    o_ref[orows, 0:512] = o_a.astype(BF16)
    exta[0:SUBLANES, :] = exta[ROWS:ROWS + SUBLANES, :]

    extc[32:32 + ROWS, :] = pcol("glu_v") * jax.nn.sigmoid(pcol("glu_g"))
    span = ROWS + 32 - SUBLANES
    conv_blocks = []
    for lb in range(BRANCH_W // 128):
        lanes = slice(lb * 128, (lb + 1) * 128)
        sh = shc.at[lb % 2]
        for ph in range(1, SUBLANES):
            sh[ph - 1] = extc[ph:ph + span, lanes]
        acc = None
        for ph in range(SUBLANES):
            for a in range(span // SUBLANES - ROWS // SUBLANES + 1):
                j = a * SUBLANES + ph - 2
                if 0 <= j < CONV_C_W:
                    arows = slice(a * SUBLANES, a * SUBLANES + ROWS)
                    shifted = extc[arows, lanes] if ph == 0 else sh[ph - 1, arows, :]
                    term = wc_ref[j:j + 1, lanes] * shifted
                    acc = term if acc is None else acc + term
        conv_blocks.append(acc + wc_ref[CONV_C_W - 1:CONV_C_W, lanes] * extc[32:32 + ROWS, lanes])
    conv_c = jnp.concatenate(conv_blocks, axis=1)
    o_c = _branch_c_post(conv_c, pcol("c_z"), cb_ref[...], lng_ref[...], lnb_ref[...])
    o_ref[orows, 1024:1536] = o_c.astype(BF16)
    extc[0:32, :] = extc[ROWS:ROWS + 32, :]

    c2 = c2_ref[orows, :]
    s2 = s2_ref[orows, :]
    for h in range(HEADS):
        lo = h * HEAD_D
        hs = slice(lo, lo + HEAD_D)
        q = _rotary(pcol("r_q", lo, HEAD_D), c2, s2)
        k = _rotary(pcol("r_k", lo, HEAD_D), c2, s2) * (HEAD_D ** -0.5)
        qb = q.astype(BF16)
        kb = k.astype(BF16)
        vb = pcol("r_v", lo, HEAD_D).astype(BF16)
        s_old = s_scr[h]
        sc = _bdot_nt(qb, kb) * decay_ref[h]
        ret = _bdot(sc.astype(BF16), vb) + _bdot(qb, s_old.astype(BF16)) * qdec_ref[:, hs]
        s_scr[h] = s_old * cdec[h] + _bdot((k * kdec_ref[:, hs]).T.astype(BF16), vb)
        o_r = _silu(pcol("r_z", lo, HEAD_D)) * (_head_norm(ret) * rgn_ref[:, hs])
        o_ref[orows, 512 + lo:512 + lo + HEAD_D] = o_r.astype(BF16)

    cmask, smask = _tile_masks(False)
    pif = _bdot(hb_ref[...], wif_ref[...])
    pre = _ml_prelims(pif[:, 0:128], pif[:, 128:256], bi_ref[...], bf_ref[...], cmask, smask)
    lane = lax.broadcasted_iota(jnp.int32, (ROWS, ROWS), 1)
    m_rows = m_scr[...]
    m_rows_new = jnp.zeros((ROWS, ROWS), F32)
    for h in range(HEADS):
        lo = h * HEAD_D
        hs = slice(lo, lo + HEAD_D)
        gt = _ml_gates(pre, h, m_rows[:, h:h + 1], cmask, smask)
        q = pcol("m_q", lo, HEAD_D)
        k = pcol("m_k", lo, HEAD_D) * (HEAD_D ** -0.5)
        qb = q.astype(BF16)
        kb = k.astype(BF16)
        vb = pcol("m_v", lo, HEAD_D).astype(BF16)
        c_old = c_scr[h]
        n_old = n_scr[:, hs]
        qn = jnp.sum(q * n_old, axis=1, keepdims=True)
        hout = _ml_output(gt, qb, kb, vb, _bdot(qb, c_old.astype(BF16)), qn)
        kw = k * gt["ws"]
        dec_full = jnp.broadcast_to(gt["dec"], (ROWS, ROWS))
        c_scr[h] = c_old * dec_full + _bdot(kw.T.astype(BF16), vb)
        n_scr[:, hs] = n_old * dec_full + _bdot(pre["bones"], kw.astype(BF16))
        m_rows_new = jnp.where(lane == h, jnp.broadcast_to(gt["m_new"], (ROWS, ROWS)), m_rows_new)
        hm = jax.nn.sigmoid(pcol("m_o", lo, HEAD_D)) * hout
        o_m = _silu(pcol("m_z", lo, HEAD_D)) * (_head_norm(hm) * mgn_ref[:, hs])
        o_ref[orows, 1536 + lo:1536 + lo + HEAD_D] = o_m.astype(BF16)
    m_scr[...] = m_rows_new


def _xmix_kernel(x_ref, g_ref, w_ref, wif_ref,
                 c2_ref, s2_ref, decay_ref, qdec_ref, kdec_ref,
                 wa_ref, wc_ref, cb_ref, lng_ref, lnb_ref, rgn_ref, mgn_ref, bi_ref, bf_ref,
                 o_ref, bufa_ref, sret_ref, bufc_ref, cml_ref, nml_ref, mml_ref,
                 hb_scr, exta, extc, shc, s_scr, c_scr, n_scr, m_scr,
                 *, cdec, steps_per_seq):
    s = pl.program_id(0)
    seq_step = s % steps_per_seq
    tab = (c2_ref, s2_ref, decay_ref, qdec_ref, kdec_ref)
    prm = (wa_ref, wc_ref, cb_ref, lng_ref, lnb_ref, rgn_ref, mgn_ref, bi_ref, bf_ref)
    scr = (exta, extc, shc, s_scr, c_scr, n_scr, m_scr)
    nchunk = STEP_ROWS // ROWS

    @pl.when(seq_step == 0)
    def _():
        exta[0:SUBLANES, :] = jnp.zeros((SUBLANES, BRANCH_W), F32)
        extc[0:32, :] = jnp.zeros((32, BRANCH_W), F32)
        s_scr[...] = jnp.zeros_like(s_scr)
        c_scr[...] = jnp.zeros_like(c_scr)
        n_scr[...] = jnp.zeros_like(n_scr)
        m_scr[...] = jnp.zeros_like(m_scr)

    for ci in range(nchunk):
        hb = hb_scr.at[ci % 2]
        hb[...] = _rmsnorm(x_ref[ci * ROWS:(ci + 1) * ROWS, :], g_ref[...]).astype(BF16)
        _mix_chunk(hb, w_ref, wif_ref, ci * ROWS, o_ref, tab, prm, scr, cdec)

    @pl.when(seq_step == steps_per_seq - 1)
    def _():
        bufa_ref[0] = exta[SUBLANES - (CONV_A_W - 1):SUBLANES, :]
        bufc_ref[0] = extc[32 - (CONV_C_W - 1):32, :]
        for h in range(HEADS):
            sret_ref[0, h] = s_scr[h]
            cml_ref[0, h] = c_scr[h]
            nml_ref[0, h:h + 1, :] = n_scr[0:1, h * HEAD_D:(h + 1) * HEAD_D]
        mml_ref[0] = m_scr[0:1, :]


def _xmix_prompt(x, gnorm, w_all, wif, layer, consts, prm, batch, seq):
    steps_per_seq = seq // STEP_ROWS
    nstep = batch * steps_per_seq
    c1 = lambda shape: _resident(shape, lambda s: (0,) * len(shape))
    rope = pl.BlockSpec((STEP_ROWS, HEAD_D), lambda s: (s % steps_per_seq, 0))
    in_specs = [
        pl.BlockSpec((STEP_ROWS, D_MODEL), lambda s: (s, 0)),
        c1((1, D_MODEL)),
        _resident((None, D_MODEL, W_BLOCK), lambda s: (layer, 0, 0)),
        c1((D_MODEL, N_IF)),
        rope, rope,
        c1((HEADS, ROWS, ROWS)),
        c1((ROWS, BRANCH_W)),
        c1((ROWS, BRANCH_W)),
        c1((CONV_A_W, BRANCH_W)),
        c1((CONV_C_W, BRANCH_W)),
    ] + [c1((1, BRANCH_W))] * 5 + [c1((1, 128))] * 2
    per_b3 = lambda s: (s // steps_per_seq, 0, 0)
    per_b4 = lambda s: (s // steps_per_seq, 0, 0, 0)
    out_specs = [
        pl.BlockSpec((STEP_ROWS, N_BRANCH * BRANCH_W), lambda s: (s, 0)),
        pl.BlockSpec((1, CONV_A_W - 1, BRANCH_W), per_b3),
        pl.BlockSpec((1, HEADS, HEAD_D, HEAD_D), per_b4),
        pl.BlockSpec((1, CONV_C_W - 1, BRANCH_W), per_b3),
        pl.BlockSpec((1, HEADS, HEAD_D, HEAD_D), per_b4),
        pl.BlockSpec((1, HEADS, HEAD_D), per_b3),
        pl.BlockSpec((1, 1, 128), per_b3),
    ]
    out_shape = [
        jax.ShapeDtypeStruct((batch * seq, N_BRANCH * BRANCH_W), BF16),
        jax.ShapeDtypeStruct((batch, CONV_A_W - 1, BRANCH_W), F32),
        jax.ShapeDtypeStruct((batch, HEADS, HEAD_D, HEAD_D), F32),
        jax.ShapeDtypeStruct((batch, CONV_C_W - 1, BRANCH_W), F32),
        jax.ShapeDtypeStruct((batch, HEADS, HEAD_D, HEAD_D), F32),
        jax.ShapeDtypeStruct((batch, HEADS, HEAD_D), F32),
        jax.ShapeDtypeStruct((batch, 1, 128), F32),
    ]
    scratch = [
        pltpu.VMEM((2, ROWS, D_MODEL), BF16),
        pltpu.VMEM((ROWS + SUBLANES, BRANCH_W), F32),
        pltpu.VMEM((ROWS + 32, BRANCH_W), F32),
        pltpu.VMEM((2, SUBLANES - 1, ROWS + 32 - SUBLANES, 128), F32),
        pltpu.VMEM((HEADS, HEAD_D, HEAD_D), F32),
        pltpu.VMEM((HEADS, HEAD_D, HEAD_D), F32),
        pltpu.VMEM((ROWS, BRANCH_W), F32),
        pltpu.VMEM((ROWS, ROWS), F32),
    ]
    return pl.pallas_call(
        functools.partial(_xmix_kernel, cdec=consts["cdec"], steps_per_seq=steps_per_seq),
        grid=(nstep,),
        in_specs=in_specs, out_specs=out_specs, out_shape=out_shape, scratch_shapes=scratch,
        compiler_params=pltpu.CompilerParams(
            dimension_semantics=("arbitrary",), vmem_limit_bytes=VMEM_LIMIT),
        name="xmix_prompt",
    )(x, gnorm, w_all, wif, consts["c2"], consts["s2"], consts["decay"], consts["qdec"],
      consts["kdec"], prm["wa"], prm["wc"], prm["cb"], prm["lng"], prm["lnb"], prm["rgn"],
      prm["mgn"], prm["bi"], prm["bf"])


def _mix_sample_kernel(pa_ref, pc_ref, rq_ref, rk_ref, rv_ref, rz_ref, mq_ref, mk_ref, mv_ref,
                       mo_ref, mz_ref, pif_ref, c2_ref, s2_ref, decay_ref, qdec_ref, kdec_ref,
                       cdec_ref, wa_ref, wc_ref, cb_ref, lng_ref, lnb_ref, rgn_ref, mgn_ref,
                       bi_ref, bf_ref, bufa_ref, sret_ref, bufc_ref, cml_ref, nrows_ref, mrows_ref,
                       o_ref, bufa_o, sret_o, bufc_o, cml_o, nrows_o, mrows_o, dec_scr):
    h = pl.program_id(1)
    sq = SAMPLE_SEQS
    nt = ROWS // sq

    @pl.when(h == 0)
    def _():
        ua = pa_ref[:, 512:1024] * pa_ref[:, 1024:1536]
        ea = [bufa_ref[:, 0, :], bufa_ref[:, 1, :]] + [ua[t * sq:(t + 1) * sq, :] for t in range(nt)]
        conv_a = jnp.concatenate(
            [wa_ref[0:1, :] * ea[t] + wa_ref[1:2, :] * ea[t + 1] + wa_ref[2:3, :] * ea[t + 2]
             for t in range(nt)], axis=0)
        o_ref[:, 0:512] = (_silu(pa_ref[:, 1536:2048]) * (pa_ref[:, 0:512] * conv_a)).astype(BF16)
        bufa_o[:, 0, :] = ea[nt]
        bufa_o[:, 1, :] = ea[nt + 1]
        uc = pc_ref[:, 0:512] * jax.nn.sigmoid(pc_ref[:, 512:1024])
        nb = CONV_C_W - 1
        ec = [bufc_ref[:, i, :] for i in range(nb)] + [uc[t * sq:(t + 1) * sq, :] for t in range(nt)]
        planes = []
        for t in range(nt):
            acc = wc_ref[0:1, :] * ec[t]
            for j in range(1, CONV_C_W):
                acc = acc + wc_ref[j:j + 1, :] * ec[t + j]
            planes.append(acc)
        conv_c = jnp.concatenate(planes, axis=0)
        o_ref[:, 1024:1536] = _branch_c_post(conv_c, pc_ref[:, 1024:1536], cb_ref[...], lng_ref[...],
                                             lnb_ref[...]).astype(BF16)
        for i in range(nb):
            bufc_o[:, i, :] = ec[i + nt]

    cmask, smask = _tile_masks(True)
    rowseq = lax.broadcasted_iota(jnp.int32, (ROWS, ROWS), 0) & (sq - 1)
    laneseq = lax.broadcasted_iota(jnp.int32, (ROWS, ROWS), 1) & (sq - 1)

    c2 = c2_ref[...]
    s2 = s2_ref[...]
    rq = _rotary(rq_ref[...], c2, s2)
    rk = _rotary(rk_ref[...], c2, s2) * (HEAD_D ** -0.5)
    rqb = rq.astype(BF16)
    rkb = rk.astype(BF16)
    rvb = rv_ref[...].astype(BF16)
    rkd_t = (rk * kdec_ref[...]).T
    cdec = cdec_ref[0, 0:1, :]

    pre = _ml_prelims(pif_ref[:, 0:128], pif_ref[:, 128:256], bi_ref[...], bf_ref[...], cmask, smask)
    m_prev = mrows_ref[...]
    gt = _ml_gates(pre, 0, m_prev[:, 0:1], cmask, smask)
    mq = mq_ref[...]
    mk = mk_ref[...] * (HEAD_D ** -0.5)
    mqb = mq.astype(BF16)
    mkb = mk.astype(BF16)
    mvb = mv_ref[...].astype(BF16)
    kw = mk * gt["ws"]
    kw_t = kw.T
    dec_full = jnp.broadcast_to(gt["dec"], (ROWS, ROWS))
    dec_scr[...] = dec_full

    def per_seq(b, carry):
        inter_r, inter_m = carry
        sb = sret_ref[b, 0]
        inter_r = jnp.where(rowseq == b, _bdot(rqb, sb.astype(BF16)), inter_r)
        k_sel = jnp.where(laneseq == b, rkd_t, 0.0).astype(BF16)
        sret_o[b, 0] = sb * cdec + _bdot(k_sel, rvb)
        cb_ = cml_ref[b, 0]
        inter_m = jnp.where(rowseq == b, _bdot(mqb, cb_.astype(BF16)), inter_m)
        kw_sel = jnp.where(laneseq == b, kw_t, 0.0).astype(BF16)
        cml_o[b, 0] = cb_ * dec_scr[pl.ds(b, 1), :] + _bdot(kw_sel, mvb)
        return inter_r, inter_m

    zero = jnp.zeros((ROWS, HEAD_D), F32)
    inter_r, inter_m = lax.fori_loop(0, sq, per_seq, (zero, zero), unroll=8)

    sc = _bdot_nt(rqb, rkb) * decay_ref[0]
    ret = _bdot(sc.astype(BF16), rvb) + inter_r * qdec_ref[...]
    o_r = (_silu(rz_ref[...]) * (_head_norm(ret) * rgn_ref[...])).astype(BF16)

    n_old = nrows_ref[...]
    qn = jnp.sum(mq * n_old, axis=1, keepdims=True)
    hout = _ml_output(gt, mqb, mkb, mvb, inter_m, qn)
    nrows_o[...] = n_old * dec_full + _bdot(pre["bones"], kw.astype(BF16))
    mrows_o[...] = jnp.broadcast_to(gt["m_new"], (ROWS, ROWS))
    hm = jax.nn.sigmoid(mo_ref[...]) * hout
    o_m = (_silu(mz_ref[...]) * (_head_norm(hm) * mgn_ref[...])).astype(BF16)

    for hh in range(HEADS):
        @pl.when(h == hh)
        def _(hh=hh):
            o_ref[:, 512 + hh * HEAD_D:512 + (hh + 1) * HEAD_D] = o_r
            o_ref[:, 1536 + hh * HEAD_D:1536 + (hh + 1) * HEAD_D] = o_m


def _mix_sample(p, pif, consts, prm, st, nseq, layer, depth, stacked):
    ntile = nseq // SAMPLE_SEQS
    sq = SAMPLE_SEQS
    head_cols = lambda name: pl.BlockSpec(
        (ROWS, HEAD_D), lambda i, h, _o=_COL[name] // HEAD_D: (i, _o + h))
    const2 = lambda i, h: (0, 0)
    per_head2 = lambda i, h: (0, h)
    conv_a_state = pl.BlockSpec((None, sq, CONV_A_W - 1, BRANCH_W), lambda i, h: (layer, i, 0, 0))
    conv_c_state = pl.BlockSpec((None, sq, CONV_C_W - 1, BRANCH_W), lambda i, h: (layer, i, 0, 0))
    matrix_state = pl.BlockSpec((None, sq, 1, HEAD_D, HEAD_D), lambda i, h: (layer, i, h, 0, 0))
    in_specs = [
        pl.BlockSpec((ROWS, 2048), lambda i, h: (i, 0)),
        pl.BlockSpec((ROWS, 2048), lambda i, h: (i, 2)),
    ] + [head_cols(n) for n in ("r_q", "r_k", "r_v", "r_z", "m_q", "m_k", "m_v", "m_o", "m_z")] + [
        pl.BlockSpec((ROWS, 256), lambda i, h: (i, h)),
        pl.BlockSpec((ROWS, HEAD_D), const2),
        pl.BlockSpec((ROWS, HEAD_D), const2),
        pl.BlockSpec((1, ROWS, ROWS), lambda i, h: (h, 0, 0)),
        pl.BlockSpec((ROWS, HEAD_D), per_head2),
        pl.BlockSpec((ROWS, HEAD_D), per_head2),
        pl.BlockSpec((1, 8, 128), lambda i, h: (h, 0, 0)),
        pl.BlockSpec((CONV_A_W, BRANCH_W), const2),
        pl.BlockSpec((CONV_C_W, BRANCH_W), const2),
        pl.BlockSpec((1, BRANCH_W), const2),
        pl.BlockSpec((1, BRANCH_W), const2),
        pl.BlockSpec((1, BRANCH_W), const2),
        pl.BlockSpec((1, HEAD_D), per_head2),
        pl.BlockSpec((1, HEAD_D), per_head2),
        pl.BlockSpec((1, 128), per_head2),
        pl.BlockSpec((1, 128), per_head2),
        conv_a_state, matrix_state, conv_c_state, matrix_state,
        pl.BlockSpec((ROWS, HEAD_D), lambda i, h: (i, h)),
        pl.BlockSpec((ROWS, HEAD_D), lambda i, h: (i, h)),
    ]
    out_specs = [
        pl.BlockSpec((ROWS, N_BRANCH * BRANCH_W), lambda i, h: (i, 0)),
        conv_a_state, matrix_state, conv_c_state, matrix_state,
        pl.BlockSpec((ROWS, HEAD_D), lambda i, h: (i, h)),
        pl.BlockSpec((ROWS, HEAD_D), lambda i, h: (i, h)),
    ]
    nrow = ntile * ROWS
    out_shape = [
        jax.ShapeDtypeStruct((nrow, N_BRANCH * BRANCH_W), BF16),
        jax.ShapeDtypeStruct((depth, nseq, CONV_A_W - 1, BRANCH_W), F32),
        jax.ShapeDtypeStruct((depth, nseq, HEADS, HEAD_D, HEAD_D), F32),
        jax.ShapeDtypeStruct((depth, nseq, CONV_C_W - 1, BRANCH_W), F32),
        jax.ShapeDtypeStruct((depth, nseq, HEADS, HEAD_D, HEAD_D), F32),
        jax.ShapeDtypeStruct((nrow, HEADS * HEAD_D), F32),
        jax.ShapeDtypeStruct((nrow, HEADS * HEAD_D), F32),
    ]
    args = [p] * 11 + [
        pif, consts["c2"], consts["s2"], consts["decay"], consts["qdec"], consts["kdec"],
        consts["cdec"], prm["wa"], prm["wc"], prm["cb"], prm["lng"], prm["lnb"], prm["rgn"],
        prm["mgn"], prm["bi_h"], prm["bf_h"],
        st["bufa"], st["sret"], st["bufc"], st["cml"], st["nrows"], st["mrows"]]
    assert len(args) == len(in_specs)
    n_in = len(args)
    aliases = {}
    if stacked is not None:
        aliases = {n_in + k: 1 + k for k in range(4)}
        in_specs = in_specs + [pl.BlockSpec(memory_space=pl.ANY)] * 4
        args = args + list(stacked)

    def entry(*refs):
        _mix_sample_kernel(*refs[:n_in], *refs[n_in + len(aliases):])

    return pl.pallas_call(
        entry,
        grid=(ntile, HEADS),
        in_specs=in_specs, out_specs=out_specs, out_shape=out_shape,
        scratch_shapes=[pltpu.VMEM((ROWS, ROWS), F32)],
        input_output_aliases=aliases,
        compiler_params=pltpu.CompilerParams(
            dimension_semantics=("arbitrary", "arbitrary"), vmem_limit_bytes=VMEM_LIMIT),
        name="mix_sample",
    )(*args)


def _rope_tables(pos):
    inv = ROPE_BASE ** (-jnp.arange(0, HEAD_D, 2, dtype=F32) / HEAD_D)
    ang = pos.astype(F32)[:, None] * inv[None, :]
    cos = jnp.cos(ang)
    sin = jnp.sin(ang)
    return jnp.concatenate([cos, cos], axis=1), jnp.concatenate([-sin, sin], axis=1)


def _decay_tables(t_of_row, seq_of_row, chunk_len):
    log_g = np.log1p(-(2.0 ** (-5.0 - np.arange(HEADS, dtype=np.float64))))
    dt = t_of_row[:, None] - t_of_row[None, :]
    ok = (seq_of_row[:, None] == seq_of_row[None, :]) & (dt >= 0)
    decay = np.where(ok[None], np.exp(np.maximum(dt, 0)[None] * log_g[:, None, None]), 0.0)
    qdec = np.exp((t_of_row[:, None] + 1.0) * log_g[None, :])
    kdec = np.exp((chunk_len - 1.0 - t_of_row)[:, None] * log_g[None, :])
    cdec = np.exp(chunk_len * log_g)
    rep = lambda a: jnp.asarray(np.repeat(a, HEAD_D, axis=1), F32)
    return jnp.asarray(decay, F32), rep(qdec), rep(kdec), cdec


def kernel(x_prompt, x_sample, state_conv_a, state_ret, state_conv_c, state_mlstm_C,
           state_mlstm_n, state_mlstm_m, norm_g, w_in, conv_a_w, conv_c_w, conv_c_b,
           ln_c_g, ln_c_b, ret_gn_g, ml_gn_g, ml_gate_b, w_br, w_out, final_g):
    bp, tp, _ = x_prompt.shape
    bs, ts, _ = x_sample.shape
    depth = w_in.shape[0]
    past_len = 16384
    sq = SAMPLE_SEQS
    ntile = bs // sq
    assert tp % STEP_ROWS == 0 and ts * sq == ROWS and bs % sq == 0

    r = np.arange(ROWS)
    c2p, s2p = _rope_tables(jnp.arange(tp, dtype=jnp.int32))
    decay_p, qdec_p, kdec_p, cdec_p = _decay_tables(r.astype(np.float64), np.zeros(ROWS), float(ROWS))
    consts_p = dict(c2=c2p, s2=s2p, decay=decay_p, qdec=qdec_p, kdec=kdec_p,
                    cdec=tuple(float(v) for v in cdec_p))
    t_s = (r // sq).astype(np.float64)
    c2s, s2s = _rope_tables(past_len + jnp.asarray(r // sq, jnp.int32))
    decay_s, qdec_s, kdec_s, cdec_s = _decay_tables(t_s, r % sq, float(ts))
    cdec_s_arr = jnp.asarray(np.broadcast_to(cdec_s[:, None, None], (HEADS, 8, 128)), F32)
    consts_s = dict(c2=c2s, s2=s2s, decay=decay_s, qdec=qdec_s, kdec=kdec_s, cdec=cdec_s_arr)

    xp = x_prompt.reshape(bp * tp, D_MODEL)
    xs = x_sample.reshape(ntile, sq, ts, D_MODEL).transpose(0, 2, 1, 3).reshape(bs * ts, D_MODEL)

    fg = final_g.reshape(1, D_MODEL)
    outs_p = [[] for _ in range(6)]
    n_s, m_s = [], []
    stacked = None
    sel_p = np.zeros((2 * HEADS, N_IF), np.float32)
    sel_s = np.zeros((2 * HEADS, HEADS * N_IF), np.float32)
    for h in range(HEADS):
        sel_p[h, h] = 1.0
        sel_p[HEADS + h, 128 + h] = 1.0
        sel_s[h, h * N_IF] = 1.0
        sel_s[HEADS + h, h * N_IF + 128] = 1.0
    w_bf = w_in.astype(BF16)
    pad_row = lambda v: jnp.pad(v, (0, 128 - v.shape[0])).reshape(1, 128)
    for l in range(depth):
        w_if = w_in[l, :, N_MAIN:N_MAIN + 2 * HEADS]
        spread = lambda sel: jnp.dot(w_if, jnp.asarray(sel), precision=lax.Precision.HIGHEST).astype(BF16)
        w_gate = w_bf[l, :, N_MAIN + 2 * HEADS:]
        gb = ml_gate_b[l]
        prm = dict(
            wa=conv_a_w[l], wc=conv_c_w[l], cb=conv_c_b[l].reshape(1, -1),
            lng=ln_c_g[l].reshape(1, -1), lnb=ln_c_b[l].reshape(1, -1),
            rgn=ret_gn_g[l].reshape(1, -1), mgn=ml_gn_g[l].reshape(1, -1),
            bi=pad_row(gb[:HEADS]), bf=pad_row(gb[HEADS:]),
            bi_h=jnp.concatenate([pad_row(gb[h:h + 1]) for h in range(HEADS)], axis=1),
            bf_h=jnp.concatenate([pad_row(gb[HEADS + h:HEADS + h + 1]) for h in range(HEADS)], axis=1),
        )
        gnorm = norm_g[l].reshape(1, D_MODEL)
        wbr = w_br[l].astype(BF16)
        wout = w_out[l].astype(BF16)
        final = l == depth - 1

        res = _xmix_prompt(xp, gnorm, w_bf, spread(sel_p), l, consts_p, prm, bp, tp)
        for k in range(5):
            outs_p[k].append(res[1 + k])
        outs_p[5].append(res[6][:, 0, :HEADS])
        xp = _merge(xp, gnorm, res[0], w_gate, wbr, wout, fg, final, tm=1024)

        ps, pifs = _inproj(xs, gnorm, w_bf, spread(sel_s), l, tm=bs * ts, tn=2048)
        n_rows = jnp.broadcast_to(
            state_mlstm_n[l].reshape(ntile, 1, sq, HEADS * HEAD_D),
            (ntile, ts, sq, HEADS * HEAD_D)).reshape(bs * ts, HEADS * HEAD_D)
        m_rows = jnp.broadcast_to(
            state_mlstm_m[l].reshape(ntile, 1, sq, HEADS, 1),
            (ntile, ts, sq, HEADS, HEAD_D)).reshape(bs * ts, HEADS * HEAD_D)
        st = dict(bufa=state_conv_a, sret=state_ret, bufc=state_conv_c, cml=state_mlstm_C,
                  nrows=n_rows, mrows=m_rows)
        res = _mix_sample(ps, pifs, consts_s, prm, st, bs, l, depth, stacked)
        stacked = tuple(res[1:5])
        n_s.append(res[5].reshape(ntile, ts, sq, HEADS, HEAD_D)[:, 0].reshape(bs, HEADS, HEAD_D))
        m_s.append(res[6].reshape(ntile, ts, sq, HEADS, HEAD_D)[:, 0, :, :, 0].reshape(bs, HEADS))
        xs = _merge(xs, gnorm, res[0], w_gate, wbr, wout, fg, final, tm=bs * ts)

    y_prompt = xp.reshape(bp, tp, D_MODEL)
    y_sample = xs.reshape(ntile, ts, sq, D_MODEL).transpose(0, 2, 1, 3).reshape(bs, ts, D_MODEL)
    sp = [jnp.stack(a, axis=0) for a in outs_p]
    return (y_prompt, y_sample, sp[0], stacked[0], sp[1], stacked[1], sp[2], stacked[2],
            sp[3], stacked[3], sp[4], jnp.stack(n_s, axis=0), sp[5], jnp.stack(m_s, axis=0))
```

```python
import functools

import numpy as np
import jax
import jax.numpy as jnp
from jax import lax
from jax.experimental import pallas as pl
from jax.experimental.pallas import tpu as pltpu

F32 = jnp.float32
BF16 = jnp.bfloat16

D_MODEL = 1024
BRANCH_W = 512
N_BRANCH = 4
HEADS = 4
HEAD_D = 128
CONV_A_W = 3
CONV_C_W = 31
ROPE_BASE = 10000.0
EPS = 1e-6
N_MAIN = 16 * BRANCH_W
N_IF = 256
N_GATE = N_BRANCH * D_MODEL
ROWS = 128
STEP_ROWS = 4 * ROWS
W_BLOCK = N_MAIN + 512
SAMPLE_SEQS = 32
SUBLANES = 8
VMEM_LIMIT = 56 * 1024 * 1024

_COL = dict(a_b=0, a_c=512, a_u=1024, a_z=1536, r_q=2048, r_k=2560, r_v=3072, r_z=3584,
            glu_v=4096, glu_g=4608, c_z=5120, m_q=5632, m_k=6144, m_v=6656, m_o=7168, m_z=7680)


def _bdot(a, b):
    return jnp.dot(a, b, preferred_element_type=F32)


def _bdot_nt(a, b):
    return lax.dot_general(a, b, (((1,), (1,)), ((), ())), preferred_element_type=F32)


def _exact_mm(mat_b16, x):
    hi = x.astype(BF16)
    r1 = x - hi.astype(F32)
    mid = r1.astype(BF16)
    lo = (r1 - mid.astype(F32)).astype(BF16)
    return _bdot(mat_b16, hi) + _bdot(mat_b16, mid) + _bdot(mat_b16, lo)


def _silu(x):
    return x * jax.nn.sigmoid(x)


def _log_sigmoid(x):
    return jnp.minimum(x, 0.0) - jnp.log1p(jnp.exp(-jnp.abs(x)))


def _head_norm(y):
    mu = jnp.mean(y, axis=-1, keepdims=True)
    yc = y - mu
    var = jnp.mean(yc * yc, axis=-1, keepdims=True)
    return yc * lax.rsqrt(var + EPS)


def _rotary(x, c2, s2):
    return x * c2 + pltpu.roll(x, HEAD_D // 2, 1) * s2


def _rmsnorm(x, g):
    ms = jnp.mean(x * x, axis=-1, keepdims=True)
    return (x * lax.rsqrt(ms + EPS)) * g


def _resident(shape, imap):
    return pl.BlockSpec(shape, imap, pipeline_mode=pl.Buffered(1))


def _inproj_kernel(x_ref, g_ref, w_ref, wif_ref, p_ref, pif_ref, hb_ref):
    @pl.when(pl.program_id(1) == 0)
    def _():
        hb = _rmsnorm(x_ref[...], g_ref[...]).astype(BF16)
        hb_ref[...] = hb
        pif_ref[...] = _bdot(hb, wif_ref[...])

    p_ref[...] = _bdot(hb_ref[...], w_ref[...])


def _inproj(x, g, w_all, wif, layer, tm, tn):
    m = x.shape[0]
    nif = wif.shape[1]
    return pl.pallas_call(
        _inproj_kernel,
        grid=(m // tm, N_MAIN // tn),
        in_specs=[
            pl.BlockSpec((tm, D_MODEL), lambda i, j: (i, 0)),
            pl.BlockSpec((1, D_MODEL), lambda i, j: (0, 0)),
            pl.BlockSpec((None, D_MODEL, tn), lambda i, j: (layer, 0, j)),
            pl.BlockSpec((D_MODEL, nif), lambda i, j: (0, 0)),
        ],
        out_specs=[
            pl.BlockSpec((tm, tn), lambda i, j: (i, j)),
            pl.BlockSpec((tm, nif), lambda i, j: (i, 0)),
        ],
        out_shape=[jax.ShapeDtypeStruct((m, N_MAIN), F32), jax.ShapeDtypeStruct((m, nif), F32)],
        scratch_shapes=[pltpu.VMEM((tm, D_MODEL), BF16)],
        compiler_params=pltpu.CompilerParams(
            dimension_semantics=("arbitrary", "arbitrary"), vmem_limit_bytes=VMEM_LIMIT),
        name="inproj",
    )(x, g, w_all, wif)


def _merge_kernel(x_ref, g_ref, o_ref, wg_ref, wbr_ref, wout_ref, fg_ref, y_ref, hb_ref, *, final):
    x = x_ref[...]
    hb_ref[...] = _rmsnorm(x, g_ref[...]).astype(BF16)
    acc = None
    for n in range(N_BRANCH):
        proj = _bdot(o_ref[:, n * BRANCH_W:(n + 1) * BRANCH_W], wbr_ref[n])
        gate = jax.nn.sigmoid(_bdot(hb_ref[...], wg_ref[:, n * D_MODEL:(n + 1) * D_MODEL]))
        term = gate * proj
        acc = term if acc is None else acc + term
    y = x + _bdot(acc.astype(BF16), wout_ref[...])
    if final:
        y = _rmsnorm(y, fg_ref[...])
    y_ref[...] = y


def _merge(x, g, o, wg, wbr, wout, fg, final, tm):
    m = x.shape[0]
    resident = lambda shape: _resident(shape, lambda i: (0,) * len(shape))
    return pl.pallas_call(
        functools.partial(_merge_kernel, final=final),
        grid=(m // tm,),
        in_specs=[
            pl.BlockSpec((tm, D_MODEL), lambda i: (i, 0)),
            resident((1, D_MODEL)),
            pl.BlockSpec((tm, N_BRANCH * BRANCH_W), lambda i: (i, 0)),
            resident((D_MODEL, N_GATE)),
            resident((N_BRANCH, BRANCH_W, D_MODEL)),
            resident((D_MODEL, D_MODEL)),
            resident((1, D_MODEL)),
        ],
        out_specs=pl.BlockSpec((tm, D_MODEL), lambda i: (i, 0)),
        out_shape=jax.ShapeDtypeStruct((m, D_MODEL), F32),
        scratch_shapes=[pltpu.VMEM((tm, D_MODEL), BF16)],
        compiler_params=pltpu.CompilerParams(
            dimension_semantics=("arbitrary",), vmem_limit_bytes=VMEM_LIMIT),
        name="merge",
    )(x, g, o, wg, wbr, wout, fg)


def _tile_masks(sample):
    row = lax.broadcasted_iota(jnp.int32, (ROWS, ROWS), 0)
    col = lax.broadcasted_iota(jnp.int32, (ROWS, ROWS), 1)
    if not sample:
        return row >= col, None
    same = (row & (SAMPLE_SEQS - 1)) == (col & (SAMPLE_SEQS - 1))
    return jnp.logical_and(same, row >= col), same


def _ml_prelims(pi, pf, bi, bfo, cmask, smask):
    ai = pi + bi
    lf = _log_sigmoid(pf + bfo)
    tri = jnp.where(cmask, 1.0, 0.0).astype(BF16)
    if smask is None:
        bones = jnp.ones((ROWS, ROWS), BF16)
    else:
        bones = jnp.where(smask, 1.0, 0.0).astype(BF16)
    fcum = _exact_mm(tri, lf)
    ftot = _exact_mm(bones, lf)
    g = (ftot - fcum) + ai
    return dict(ai_t=ai.T, fcum=fcum, fcum_t=fcum.T, ftot=ftot, g=g, g_t=g.T, bones=bones)


def _ml_gates(pre, col, m_prev_c, cmask, smask):
    f_c = pre["fcum"][:, col:col + 1]
    f_r = pre["fcum_t"][col:col + 1, :]
    ig_r = pre["ai_t"][col:col + 1, :]
    g_c = pre["g"][:, col:col + 1]
    g_r = pre["g_t"][col:col + 1, :]
    ftot_c = pre["ftot"][:, col:col + 1]
    logw = jnp.where(cmask, (f_c - f_r) + ig_r, -jnp.inf)
    b = f_c + m_prev_c
    m_t = jnp.maximum(jnp.max(logw, axis=1, keepdims=True), b)
    wts = jnp.exp(logw - m_t)
    inter = jnp.exp(b - m_t)
    if smask is None:
        gmax = jnp.max(g_r, axis=1, keepdims=True)
    else:
        gmax = jnp.max(jnp.where(smask, g_r, -jnp.inf), axis=1, keepdims=True)
    m_new = jnp.maximum(ftot_c + m_prev_c, gmax)
    ws = jnp.exp(g_c - m_new)
    dec = jnp.exp((ftot_c + m_prev_c) - m_new)
    return dict(m_t=m_t, wts=wts, inter=inter, m_new=m_new, ws=ws, dec=dec)


def _ml_output(gt, qb, kb, vb, inter_term, qn):
    s = _bdot_nt(qb, kb) * gt["wts"]
    num = _bdot(s.astype(BF16), vb) + inter_term * gt["inter"]
    den = jnp.sum(s, axis=1, keepdims=True) + qn * gt["inter"]
    return num / jnp.maximum(jnp.abs(den), jnp.exp(-gt["m_t"]))


def _branch_c_post(conv, c_z, cb, lng, lnb):
    y = conv + cb
    mu = jnp.mean(y, axis=-1, keepdims=True)
    yc = y - mu
    var = jnp.mean(yc * yc, axis=-1, keepdims=True)
    ln = (yc * lax.rsqrt(var + EPS)) * lng + lnb
    return _silu(c_z) * _silu(ln)


_GROUPS = ((0, 2048), (2048, 4096), (4096, 5632), (5632, N_MAIN))


def _mix_chunk(hb_ref, w_ref, wif_ref, or0, o_ref, tab, prm, scr, cdec):
    c2_ref, s2_ref, decay_ref, qdec_ref, kdec_ref = tab
    wa_ref, wc_ref, cb_ref, lng_ref, lnb_ref, rgn_ref, mgn_ref, bi_ref, bf_ref = prm
    exta, extc, shc, s_scr, c_scr, n_scr, m_scr = scr
    orows = slice(or0, or0 + ROWS)
    projected = {}

    def pcol(name, lo=0, w=BRANCH_W):
        c0 = _COL[name]
        g0, g1 = next(g for g in _GROUPS if g[0] <= c0 < g[1])
        if g0 not in projected:
            projected[g0] = _bdot(hb_ref[...], w_ref[:, g0:g1])
        return projected[g0][:, c0 - g0 + lo:c0 - g0 + lo + w]

    ua = pcol("a_c") * pcol("a_u")
    exta[SUBLANES:SUBLANES + ROWS, :] = ua
    conv_a = (wa_ref[0:1, :] * exta[SUBLANES - 2:SUBLANES - 2 + ROWS, :]
              + wa_ref[1:2, :] * exta[SUBLANES - 1:SUBLANES - 1 + ROWS, :] + wa_ref[2:3, :] * ua)
    o_a = _silu(pcol("a_z")) * (pcol("a_b") * conv_a)
    o_ref[orows, 0:512] = o_a.astype(BF16)
    exta[0:SUBLANES, :] = exta[ROWS:ROWS + SUBLANES, :]

    extc[32:32 + ROWS, :] = pcol("glu_v") * jax.nn.sigmoid(pcol("glu_g"))
    span = ROWS + 32 - SUBLANES
    conv_blocks = []
    for lb in range(BRANCH_W // 128):
        lanes = slice(lb * 128, (lb + 1) * 128)
        sh = shc.at[lb % 2]
        for ph in range(1, SUBLANES):
            sh[ph - 1] = extc[ph:ph + span, lanes]
        acc = None
        for ph in range(SUBLANES):
            for a in range(span // SUBLANES - ROWS // SUBLANES + 1):
                j = a * SUBLANES + ph - 2
                if 0 <= j < CONV_C_W:
                    arows = slice(a * SUBLANES, a * SUBLANES + ROWS)
                    shifted = extc[arows, lanes] if ph == 0 else sh[ph - 1, arows, :]
                    term = wc_ref[j:j + 1, lanes] * shifted
                    acc = term if acc is None else acc + term
        conv_blocks.append(acc + wc_ref[CONV_C_W - 1:CONV_C_W, lanes] * extc[32:32 + ROWS, lanes])
    conv_c = jnp.concatenate(conv_blocks, axis=1)
    o_c = _branch_c_post(conv_c, pcol("c_z"), cb_ref[...], lng_ref[...], lnb_ref[...])
    o_ref[orows, 1024:1536] = o_c.astype(BF16)
    extc[0:32, :] = extc[ROWS:ROWS + 32, :]

    c2 = c2_ref[orows, :]
    s2 = s2_ref[orows, :]
    for h in range(HEADS):
        lo = h * HEAD_D
        hs = slice(lo, lo + HEAD_D)
        q = _rotary(pcol("r_q", lo, HEAD_D), c2, s2)
        k = _rotary(pcol("r_k", lo, HEAD_D), c2, s2) * (HEAD_D ** -0.5)
        qb = q.astype(BF16)
        kb = k.astype(BF16)
        vb = pcol("r_v", lo, HEAD_D).astype(BF16)
        s_old = s_scr[h]
        sc = _bdot_nt(qb, kb) * decay_ref[h]
        ret = _bdot(jnp.concatenate([sc.astype(BF16), (q * qdec_ref[:, hs]).astype(BF16)], axis=1),
                    jnp.concatenate([vb, s_old.astype(BF16)], axis=0))
        s_scr[h] = s_old * cdec[h] + _bdot((k * kdec_ref[:, hs]).T.astype(BF16), vb)
        o_r = _silu(pcol("r_z", lo, HEAD_D)) * (_head_norm(ret) * rgn_ref[:, hs])
        o_ref[orows, 512 + lo:512 + lo + HEAD_D] = o_r.astype(BF16)

    cmask, smask = _tile_masks(False)
    pif = _bdot(hb_ref[...], wif_ref[...])
    pre = _ml_prelims(pif[:, 0:128], pif[:, 128:256], bi_ref[...], bf_ref[...], cmask, smask)
    lane = lax.broadcasted_iota(jnp.int32, (ROWS, ROWS), 1)
    m_rows = m_scr[...]
    m_rows_new = jnp.zeros((ROWS, ROWS), F32)
    for h in range(HEADS):
        lo = h * HEAD_D
        hs = slice(lo, lo + HEAD_D)
        gt = _ml_gates(pre, h, m_rows[:, h:h + 1], cmask, smask)
        q = pcol("m_q", lo, HEAD_D)
        k = pcol("m_k", lo, HEAD_D) * (HEAD_D ** -0.5)
        qb = q.astype(BF16)
        kb = k.astype(BF16)
        vb = pcol("m_v", lo, HEAD_D).astype(BF16)
        c_old = c_scr[h]
        n_old = n_scr[:, hs]
        qn = jnp.sum(q * n_old, axis=1, keepdims=True)
        s = _bdot_nt(qb, kb) * gt["wts"]
        num = _bdot(jnp.concatenate([s.astype(BF16), (q * gt["inter"]).astype(BF16)], axis=1),
                    jnp.concatenate([vb, c_old.astype(BF16)], axis=0))
        den = jnp.sum(s, axis=1, keepdims=True) + qn * gt["inter"]
        hout = num / jnp.maximum(jnp.abs(den), jnp.exp(-gt["m_t"]))
        kw = k * gt["ws"]
        dec_full = jnp.broadcast_to(gt["dec"], (ROWS, ROWS))
        c_scr[h] = c_old * dec_full + _bdot(kw.T.astype(BF16), vb)
        n_scr[:, hs] = n_old * dec_full + _bdot(pre["bones"], kw.astype(BF16))
        m_rows_new = jnp.where(lane == h, jnp.broadcast_to(gt["m_new"], (ROWS, ROWS)), m_rows_new)
        hm = jax.nn.sigmoid(pcol("m_o", lo, HEAD_D)) * hout
        o_m = _silu(pcol("m_z", lo, HEAD_D)) * (_head_norm(hm) * mgn_ref[:, hs])
        o_ref[orows, 1536 + lo:1536 + lo + HEAD_D] = o_m.astype(BF16)
    m_scr[...] = m_rows_new


def _xmix_kernel(x_ref, g_ref, w_ref, wif_ref,
                 c2_ref, s2_ref, decay_ref, qdec_ref, kdec_ref,
                 wa_ref, wc_ref, cb_ref, lng_ref, lnb_ref, rgn_ref, mgn_ref, bi_ref, bf_ref,
                 o_ref, bufa_ref, sret_ref, bufc_ref, cml_ref, nml_ref, mml_ref,
                 hb_scr, exta, extc, shc, s_scr, c_scr, n_scr, m_scr,
                 *, cdec, steps_per_seq):
    s = pl.program_id(0)
    seq_step = s % steps_per_seq
    tab = (c2_ref, s2_ref, decay_ref, qdec_ref, kdec_ref)
    prm = (wa_ref, wc_ref, cb_ref, lng_ref, lnb_ref, rgn_ref, mgn_ref, bi_ref, bf_ref)
    scr = (exta, extc, shc, s_scr, c_scr, n_scr, m_scr)
    nchunk = STEP_ROWS // ROWS

    @pl.when(seq_step == 0)
    def _():
        exta[0:SUBLANES, :] = jnp.zeros((SUBLANES, BRANCH_W), F32)
        extc[0:32, :] = jnp.zeros((32, BRANCH_W), F32)
        s_scr[...] = jnp.zeros_like(s_scr)
        c_scr[...] = jnp.zeros_like(c_scr)
        n_scr[...] = jnp.zeros_like(n_scr)
        m_scr[...] = jnp.zeros_like(m_scr)

    for ci in range(nchunk):
        hb = hb_scr.at[ci % 2]
        hb[...] = _rmsnorm(x_ref[ci * ROWS:(ci + 1) * ROWS, :], g_ref[...]).astype(BF16)
        _mix_chunk(hb, w_ref, wif_ref, ci * ROWS, o_ref, tab, prm, scr, cdec)

    @pl.when(seq_step == steps_per_seq - 1)
    def _():
        bufa_ref[0] = exta[SUBLANES - (CONV_A_W - 1):SUBLANES, :]
        bufc_ref[0] = extc[32 - (CONV_C_W - 1):32, :]
        for h in range(HEADS):
            sret_ref[0, h] = s_scr[h]
            cml_ref[0, h] = c_scr[h]
            nml_ref[0, h:h + 1, :] = n_scr[0:1, h * HEAD_D:(h + 1) * HEAD_D]
        mml_ref[0] = m_scr[0:1, :]


def _xmix_prompt(x, gnorm, w_all, wif, layer, consts, prm, batch, seq):
    steps_per_seq = seq // STEP_ROWS
    nstep = batch * steps_per_seq
    c1 = lambda shape: _resident(shape, lambda s: (0,) * len(shape))
    rope = pl.BlockSpec((STEP_ROWS, HEAD_D), lambda s: (s % steps_per_seq, 0))
    in_specs = [
        pl.BlockSpec((STEP_ROWS, D_MODEL), lambda s: (s, 0)),
        c1((1, D_MODEL)),
        _resident((None, D_MODEL, W_BLOCK), lambda s: (layer, 0, 0)),
        c1((D_MODEL, N_IF)),
        rope, rope,
        c1((HEADS, ROWS, ROWS)),
        c1((ROWS, BRANCH_W)),
        c1((ROWS, BRANCH_W)),
        c1((CONV_A_W, BRANCH_W)),
        c1((CONV_C_W, BRANCH_W)),
    ] + [c1((1, BRANCH_W))] * 5 + [c1((1, 128))] * 2
    per_b3 = lambda s: (s // steps_per_seq, 0, 0)
    per_b4 = lambda s: (s // steps_per_seq, 0, 0, 0)
    out_specs = [
        pl.BlockSpec((STEP_ROWS, N_BRANCH * BRANCH_W), lambda s: (s, 0)),
        pl.BlockSpec((1, CONV_A_W - 1, BRANCH_W), per_b3),
        pl.BlockSpec((1, HEADS, HEAD_D, HEAD_D), per_b4),
        pl.BlockSpec((1, CONV_C_W - 1, BRANCH_W), per_b3),
        pl.BlockSpec((1, HEADS, HEAD_D, HEAD_D), per_b4),
        pl.BlockSpec((1, HEADS, HEAD_D), per_b3),
        pl.BlockSpec((1, 1, 128), per_b3),
    ]
    out_shape = [
        jax.ShapeDtypeStruct((batch * seq, N_BRANCH * BRANCH_W), BF16),
        jax.ShapeDtypeStruct((batch, CONV_A_W - 1, BRANCH_W), F32),
        jax.ShapeDtypeStruct((batch, HEADS, HEAD_D, HEAD_D), F32),
        jax.ShapeDtypeStruct((batch, CONV_C_W - 1, BRANCH_W), F32),
        jax.ShapeDtypeStruct((batch, HEADS, HEAD_D, HEAD_D), F32),
        jax.ShapeDtypeStruct((batch, HEADS, HEAD_D), F32),
        jax.ShapeDtypeStruct((batch, 1, 128), F32),
    ]
    scratch = [
        pltpu.VMEM((2, ROWS, D_MODEL), BF16),
        pltpu.VMEM((ROWS + SUBLANES, BRANCH_W), F32),
        pltpu.VMEM((ROWS + 32, BRANCH_W), F32),
        pltpu.VMEM((2, SUBLANES - 1, ROWS + 32 - SUBLANES, 128), F32),
        pltpu.VMEM((HEADS, HEAD_D, HEAD_D), F32),
        pltpu.VMEM((HEADS, HEAD_D, HEAD_D), F32),
        pltpu.VMEM((ROWS, BRANCH_W), F32),
        pltpu.VMEM((ROWS, ROWS), F32),
    ]
    return pl.pallas_call(
        functools.partial(_xmix_kernel, cdec=consts["cdec"], steps_per_seq=steps_per_seq),
        grid=(nstep,),
        in_specs=in_specs, out_specs=out_specs, out_shape=out_shape, scratch_shapes=scratch,
        compiler_params=pltpu.CompilerParams(
            dimension_semantics=("arbitrary",), vmem_limit_bytes=VMEM_LIMIT),
        name="xmix_prompt",
    )(x, gnorm, w_all, wif, consts["c2"], consts["s2"], consts["decay"], consts["qdec"],
      consts["kdec"], prm["wa"], prm["wc"], prm["cb"], prm["lng"], prm["lnb"], prm["rgn"],
      prm["mgn"], prm["bi"], prm["bf"])


def _mix_sample_kernel(pa_ref, pc_ref, rq_ref, rk_ref, rv_ref, rz_ref, mq_ref, mk_ref, mv_ref,
                       mo_ref, mz_ref, pif_ref, c2_ref, s2_ref, decay_ref, qdec_ref, kdec_ref,
                       cdec_ref, wa_ref, wc_ref, cb_ref, lng_ref, lnb_ref, rgn_ref, mgn_ref,
                       bi_ref, bf_ref, bufa_ref, sret_ref, bufc_ref, cml_ref, nrows_ref, mrows_ref,
                       o_ref, bufa_o, sret_o, bufc_o, cml_o, nrows_o, mrows_o, dec_scr):
    h = pl.program_id(1)
    sq = SAMPLE_SEQS
    nt = ROWS // sq

    @pl.when(h == 0)
    def _():
        ua = pa_ref[:, 512:1024] * pa_ref[:, 1024:1536]
        ea = [bufa_ref[:, 0, :], bufa_ref[:, 1, :]] + [ua[t * sq:(t + 1) * sq, :] for t in range(nt)]
        conv_a = jnp.concatenate(
            [wa_ref[0:1, :] * ea[t] + wa_ref[1:2, :] * ea[t + 1] + wa_ref[2:3, :] * ea[t + 2]
             for t in range(nt)], axis=0)
        o_ref[:, 0:512] = (_silu(pa_ref[:, 1536:2048]) * (pa_ref[:, 0:512] * conv_a)).astype(BF16)
        bufa_o[:, 0, :] = ea[nt]
        bufa_o[:, 1, :] = ea[nt + 1]
        uc = pc_ref[:, 0:512] * jax.nn.sigmoid(pc_ref[:, 512:1024])
        nb = CONV_C_W - 1
        ec = [bufc_ref[:, i, :] for i in range(nb)] + [uc[t * sq:(t + 1) * sq, :] for t in range(nt)]
        planes = []
        for t in range(nt):
            acc = wc_ref[0:1, :] * ec[t]
            for j in range(1, CONV_C_W):
                acc = acc + wc_ref[j:j + 1, :] * ec[t + j]
            planes.append(acc)
        conv_c = jnp.concatenate(planes, axis=0)
        o_ref[:, 1024:1536] = _branch_c_post(conv_c, pc_ref[:, 1024:1536], cb_ref[...], lng_ref[...],
                                             lnb_ref[...]).astype(BF16)
        for i in range(nb):
            bufc_o[:, i, :] = ec[i + nt]

    cmask, smask = _tile_masks(True)
    rowseq = lax.broadcasted_iota(jnp.int32, (ROWS, ROWS), 0) & (sq - 1)
    laneseq = lax.broadcasted_iota(jnp.int32, (ROWS, ROWS), 1) & (sq - 1)

    c2 = c2_ref[...]
    s2 = s2_ref[...]
    rq = _rotary(rq_ref[...], c2, s2)
    rk = _rotary(rk_ref[...], c2, s2) * (HEAD_D ** -0.5)
    rqb = rq.astype(BF16)
    rkb = rk.astype(BF16)
    rvb = rv_ref[...].astype(BF16)
    rkd_t = (rk * kdec_ref[...]).T
    cdec = cdec_ref[0, 0:1, :]

    pre = _ml_prelims(pif_ref[:, 0:128], pif_ref[:, 128:256], bi_ref[...], bf_ref[...], cmask, smask)
    m_prev = mrows_ref[...]
    gt = _ml_gates(pre, 0, m_prev[:, 0:1], cmask, smask)
    mq = mq_ref[...]
    mk = mk_ref[...] * (HEAD_D ** -0.5)
    mqb = mq.astype(BF16)
    mkb = mk.astype(BF16)
    mvb = mv_ref[...].astype(BF16)
    kw = mk * gt["ws"]
    kw_t = kw.T
    dec_full = jnp.broadcast_to(gt["dec"], (ROWS, ROWS))
    dec_scr[...] = dec_full

    def per_seq(b, carry):
        inter_r, inter_m = carry
        sb = sret_ref[b, 0]
        inter_r = jnp.where(rowseq == b, _bdot(rqb, sb.astype(BF16)), inter_r)
        k_sel = jnp.where(laneseq == b, rkd_t, 0.0).astype(BF16)
        sret_o[b, 0] = sb * cdec + _bdot(k_sel, rvb)
        cb_ = cml_ref[b, 0]
        inter_m = jnp.where(rowseq == b, _bdot(mqb, cb_.astype(BF16)), inter_m)
        kw_sel = jnp.where(laneseq == b, kw_t, 0.0).astype(BF16)
        cml_o[b, 0] = cb_ * dec_scr[pl.ds(b, 1), :] + _bdot(kw_sel, mvb)
        return inter_r, inter_m

    zero = jnp.zeros((ROWS, HEAD_D), F32)
    inter_r, inter_m = lax.fori_loop(0, sq, per_seq, (zero, zero), unroll=8)

    sc = _bdot_nt(rqb, rkb) * decay_ref[0]
    ret = _bdot(sc.astype(BF16), rvb) + inter_r * qdec_ref[...]
    o_r = (_silu(rz_ref[...]) * (_head_norm(ret) * rgn_ref[...])).astype(BF16)

    n_old = nrows_ref[...]
    qn = jnp.sum(mq * n_old, axis=1, keepdims=True)
    hout = _ml_output(gt, mqb, mkb, mvb, inter_m, qn)
    nrows_o[...] = n_old * dec_full + _bdot(pre["bones"], kw.astype(BF16))
    mrows_o[...] = jnp.broadcast_to(gt["m_new"], (ROWS, ROWS))
    hm = jax.nn.sigmoid(mo_ref[...]) * hout
    o_m = (_silu(mz_ref[...]) * (_head_norm(hm) * mgn_ref[...])).astype(BF16)

    for hh in range(HEADS):
        @pl.when(h == hh)
        def _(hh=hh):
            o_ref[:, 512 + hh * HEAD_D:512 + (hh + 1) * HEAD_D] = o_r
            o_ref[:, 1536 + hh * HEAD_D:1536 + (hh + 1) * HEAD_D] = o_m


def _mix_sample(p, pif, consts, prm, st, nseq, layer, depth, stacked):
    ntile = nseq // SAMPLE_SEQS
    sq = SAMPLE_SEQS
    head_cols = lambda name: pl.BlockSpec(
        (ROWS, HEAD_D), lambda i, h, _o=_COL[name] // HEAD_D: (i, _o + h))
    const2 = lambda i, h: (0, 0)
    per_head2 = lambda i, h: (0, h)
    conv_a_state = pl.BlockSpec((None, sq, CONV_A_W - 1, BRANCH_W), lambda i, h: (layer, i, 0, 0))
    conv_c_state = pl.BlockSpec((None, sq, CONV_C_W - 1, BRANCH_W), lambda i, h: (layer, i, 0, 0))
    matrix_state = pl.BlockSpec((None, sq, 1, HEAD_D, HEAD_D), lambda i, h: (layer, i, h, 0, 0))
    in_specs = [
        pl.BlockSpec((ROWS, 2048), lambda i, h: (i, 0)),
        pl.BlockSpec((ROWS, 2048), lambda i, h: (i, 2)),
    ] + [head_cols(n) for n in ("r_q", "r_k", "r_v", "r_z", "m_q", "m_k", "m_v", "m_o", "m_z")] + [
        pl.BlockSpec((ROWS, 256), lambda i, h: (i, h)),
        pl.BlockSpec((ROWS, HEAD_D), const2),
        pl.BlockSpec((ROWS, HEAD_D), const2),
        pl.BlockSpec((1, ROWS, ROWS), lambda i, h: (h, 0, 0)),
        pl.BlockSpec((ROWS, HEAD_D), per_head2),
        pl.BlockSpec((ROWS, HEAD_D), per_head2),
        pl.BlockSpec((1, 8, 128), lambda i, h: (h, 0, 0)),
        pl.BlockSpec((CONV_A_W, BRANCH_W), const2),
        pl.BlockSpec((CONV_C_W, BRANCH_W), const2),
        pl.BlockSpec((1, BRANCH_W), const2),
        pl.BlockSpec((1, BRANCH_W), const2),
        pl.BlockSpec((1, BRANCH_W), const2),
        pl.BlockSpec((1, HEAD_D), per_head2),
        pl.BlockSpec((1, HEAD_D), per_head2),
        pl.BlockSpec((1, 128), per_head2),
        pl.BlockSpec((1, 128), per_head2),
        conv_a_state, matrix_state, conv_c_state, matrix_state,
        pl.BlockSpec((ROWS, HEAD_D), lambda i, h: (i, h)),
        pl.BlockSpec((ROWS, HEAD_D), lambda i, h: (i, h)),
    ]
    out_specs = [
        pl.BlockSpec((ROWS, N_BRANCH * BRANCH_W), lambda i, h: (i, 0)),
        conv_a_state, matrix_state, conv_c_state, matrix_state,
        pl.BlockSpec((ROWS, HEAD_D), lambda i, h: (i, h)),
        pl.BlockSpec((ROWS, HEAD_D), lambda i, h: (i, h)),
    ]
    nrow = ntile * ROWS
    out_shape = [
        jax.ShapeDtypeStruct((nrow, N_BRANCH * BRANCH_W), BF16),
        jax.ShapeDtypeStruct((depth, nseq, CONV_A_W - 1, BRANCH_W), F32),
        jax.ShapeDtypeStruct((depth, nseq, HEADS, HEAD_D, HEAD_D), F32),
        jax.ShapeDtypeStruct((depth, nseq, CONV_C_W - 1, BRANCH_W), F32),
        jax.ShapeDtypeStruct((depth, nseq, HEADS, HEAD_D, HEAD_D), F32),
        jax.ShapeDtypeStruct((nrow, HEADS * HEAD_D), F32),
        jax.ShapeDtypeStruct((nrow, HEADS * HEAD_D), F32),
    ]
    args = [p] * 11 + [
        pif, consts["c2"], consts["s2"], consts["decay"], consts["qdec"], consts["kdec"],
        consts["cdec"], prm["wa"], prm["wc"], prm["cb"], prm["lng"], prm["lnb"], prm["rgn"],
        prm["mgn"], prm["bi_h"], prm["bf_h"],
        st["bufa"], st["sret"], st["bufc"], st["cml"], st["nrows"], st["mrows"]]
    assert len(args) == len(in_specs)
    n_in = len(args)
    aliases = {}
    if stacked is not None:
        aliases = {n_in + k: 1 + k for k in range(4)}
        in_specs = in_specs + [pl.BlockSpec(memory_space=pl.ANY)] * 4
        args = args + list(stacked)

    def entry(*refs):
        _mix_sample_kernel(*refs[:n_in], *refs[n_in + len(aliases):])

    return pl.pallas_call(
        entry,
        grid=(ntile, HEADS),
        in_specs=in_specs, out_specs=out_specs, out_shape=out_shape,
        scratch_shapes=[pltpu.VMEM((ROWS, ROWS), F32)],
        input_output_aliases=aliases,
        compiler_params=pltpu.CompilerParams(
            dimension_semantics=("arbitrary", "arbitrary"), vmem_limit_bytes=VMEM_LIMIT),
        name="mix_sample",
    )(*args)


def _rope_tables(pos):
    inv = ROPE_BASE ** (-jnp.arange(0, HEAD_D, 2, dtype=F32) / HEAD_D)
    ang = pos.astype(F32)[:, None] * inv[None, :]
    cos = jnp.cos(ang)
    sin = jnp.sin(ang)
    return jnp.concatenate([cos, cos], axis=1), jnp.concatenate([-sin, sin], axis=1)


def _decay_tables(t_of_row, seq_of_row, chunk_len):
    log_g = np.log1p(-(2.0 ** (-5.0 - np.arange(HEADS, dtype=np.float64))))
    dt = t_of_row[:, None] - t_of_row[None, :]
    ok = (seq_of_row[:, None] == seq_of_row[None, :]) & (dt >= 0)
    decay = np.where(ok[None], np.exp(np.maximum(dt, 0)[None] * log_g[:, None, None]), 0.0)
    qdec = np.exp((t_of_row[:, None] + 1.0) * log_g[None, :])
    kdec = np.exp((chunk_len - 1.0 - t_of_row)[:, None] * log_g[None, :])
    cdec = np.exp(chunk_len * log_g)
    rep = lambda a: jnp.asarray(np.repeat(a, HEAD_D, axis=1), F32)
    return jnp.asarray(decay, F32), rep(qdec), rep(kdec), cdec


def kernel(x_prompt, x_sample, state_conv_a, state_ret, state_conv_c, state_mlstm_C,
           state_mlstm_n, state_mlstm_m, norm_g, w_in, conv_a_w, conv_c_w, conv_c_b,
           ln_c_g, ln_c_b, ret_gn_g, ml_gn_g, ml_gate_b, w_br, w_out, final_g):
    bp, tp, _ = x_prompt.shape
    bs, ts, _ = x_sample.shape
    depth = w_in.shape[0]
    past_len = 16384
    sq = SAMPLE_SEQS
    ntile = bs // sq
    assert tp % STEP_ROWS == 0 and ts * sq == ROWS and bs % sq == 0

    r = np.arange(ROWS)
    c2p, s2p = _rope_tables(jnp.arange(tp, dtype=jnp.int32))
    decay_p, qdec_p, kdec_p, cdec_p = _decay_tables(r.astype(np.float64), np.zeros(ROWS), float(ROWS))
    consts_p = dict(c2=c2p, s2=s2p, decay=decay_p, qdec=qdec_p, kdec=kdec_p,
                    cdec=tuple(float(v) for v in cdec_p))
    t_s = (r // sq).astype(np.float64)
    c2s, s2s = _rope_tables(past_len + jnp.asarray(r // sq, jnp.int32))
    decay_s, qdec_s, kdec_s, cdec_s = _decay_tables(t_s, r % sq, float(ts))
    cdec_s_arr = jnp.asarray(np.broadcast_to(cdec_s[:, None, None], (HEADS, 8, 128)), F32)
    consts_s = dict(c2=c2s, s2=s2s, decay=decay_s, qdec=qdec_s, kdec=kdec_s, cdec=cdec_s_arr)

    xp = x_prompt.reshape(bp * tp, D_MODEL)
    xs = x_sample.reshape(ntile, sq, ts, D_MODEL).transpose(0, 2, 1, 3).reshape(bs * ts, D_MODEL)

    fg = final_g.reshape(1, D_MODEL)
    outs_p = [[] for _ in range(6)]
    n_s, m_s = [], []
    stacked = None
    sel_p = np.zeros((2 * HEADS, N_IF), np.float32)
    sel_s = np.zeros((2 * HEADS, HEADS * N_IF), np.float32)
    for h in range(HEADS):
        sel_p[h, h] = 1.0
        sel_p[HEADS + h, 128 + h] = 1.0
        sel_s[h, h * N_IF] = 1.0
        sel_s[HEADS + h, h * N_IF + 128] = 1.0
    w_bf = w_in.astype(BF16)
    pad_row = lambda v: jnp.pad(v, (0, 128 - v.shape[0])).reshape(1, 128)
    for l in range(depth):
        w_if = w_in[l, :, N_MAIN:N_MAIN + 2 * HEADS]
        spread = lambda sel: jnp.dot(w_if, jnp.asarray(sel), precision=lax.Precision.HIGHEST).astype(BF16)
        w_gate = w_bf[l, :, N_MAIN + 2 * HEADS:]
        gb = ml_gate_b[l]
        prm = dict(
            wa=conv_a_w[l], wc=conv_c_w[l], cb=conv_c_b[l].reshape(1, -1),
            lng=ln_c_g[l].reshape(1, -1), lnb=ln_c_b[l].reshape(1, -1),
            rgn=ret_gn_g[l].reshape(1, -1), mgn=ml_gn_g[l].reshape(1, -1),
            bi=pad_row(gb[:HEADS]), bf=pad_row(gb[HEADS:]),
            bi_h=jnp.concatenate([pad_row(gb[h:h + 1]) for h in range(HEADS)], axis=1),
            bf_h=jnp.concatenate([pad_row(gb[HEADS + h:HEADS + h + 1]) for h in range(HEADS)], axis=1),
        )
        gnorm = norm_g[l].reshape(1, D_MODEL)
        wbr = w_br[l].astype(BF16)
        wout = w_out[l].astype(BF16)
        final = l == depth - 1

        res = _xmix_prompt(xp, gnorm, w_bf, spread(sel_p), l, consts_p, prm, bp, tp)
        for k in range(5):
            outs_p[k].append(res[1 + k])
        outs_p[5].append(res[6][:, 0, :HEADS])
        xp = _merge(xp, gnorm, res[0], w_gate, wbr, wout, fg, final, tm=512)

        ps, pifs = _inproj(xs, gnorm, w_bf, spread(sel_s), l, tm=bs * ts, tn=2048)
        n_rows = jnp.broadcast_to(
            state_mlstm_n[l].reshape(ntile, 1, sq, HEADS * HEAD_D),
            (ntile, ts, sq, HEADS * HEAD_D)).reshape(bs * ts, HEADS * HEAD_D)
        m_rows = jnp.broadcast_to(
            state_mlstm_m[l].reshape(ntile, 1, sq, HEADS, 1),
            (ntile, ts, sq, HEADS, HEAD_D)).reshape(bs * ts, HEADS * HEAD_D)
        st = dict(bufa=state_conv_a, sret=state_ret, bufc=state_conv_c, cml=state_mlstm_C,
                  nrows=n_rows, mrows=m_rows)
        res = _mix_sample(ps, pifs, consts_s, prm, st, bs, l, depth, stacked)
        stacked = tuple(res[1:5])
        n_s.append(res[5].reshape(ntile, ts, sq, HEADS, HEAD_D)[:, 0].reshape(bs, HEADS, HEAD_D))
        m_s.append(res[6].reshape(ntile, ts, sq, HEADS, HEAD_D)[:, 0, :, :, 0].reshape(bs, HEADS))
        xs = _merge(xs, gnorm, res[0], w_gate, wbr, wout, fg, final, tm=bs * ts)

    y_prompt = xp.reshape(bp, tp, D_MODEL)
    y_sample = xs.reshape(ntile, ts, sq, D_MODEL).transpose(0, 2, 1, 3).reshape(bs, ts, D_MODEL)
    sp = [jnp.stack(a, axis=0) for a in outs_p]
    return (y_prompt, y_sample, sp[0], stacked[0], sp[1], stacked[1], sp[2], stacked[2],
            sp[3], stacked[3], sp[4], jnp.stack(n_s, axis=0), sp[5], jnp.stack(m_s, axis=0))
```

```python
import functools

import numpy as np
import jax
import jax.numpy as jnp
from jax import lax
from jax.experimental import pallas as pl
from jax.experimental.pallas import tpu as pltpu

F32 = jnp.float32
BF16 = jnp.bfloat16

D_MODEL = 1024
BRANCH_W = 512
N_BRANCH = 4
HEADS = 4
HEAD_D = 128
CONV_A_W = 3
CONV_C_W = 31
ROPE_BASE = 10000.0
EPS = 1e-6
N_MAIN = 16 * BRANCH_W
N_IF = 256
N_GATE = N_BRANCH * D_MODEL
ROWS = 128
STEP_ROWS = 4 * ROWS
W_BLOCK = N_MAIN + 512
SAMPLE_SEQS = 32
SUBLANES = 8
PAST_LEN = 16384
C_TAIL = -(-(CONV_C_W - 1) // SUBLANES) * SUBLANES
C_PAD = C_TAIL - (CONV_C_W - 1)
VMEM_LIMIT = 56 * 1024 * 1024

_COL = dict(a_b=0, a_c=512, a_u=1024, a_z=1536, r_q=2048, r_k=2560, r_v=3072, r_z=3584,
            glu_v=4096, glu_g=4608, c_z=5120, m_q=5632, m_k=6144, m_v=6656, m_o=7168, m_z=7680)


def _bdot(a, b):
    return jnp.dot(a, b, preferred_element_type=F32)


def _bdot_nt(a, b):
    return lax.dot_general(a, b, (((1,), (1,)), ((), ())), preferred_element_type=F32)


def _exact_mm(mat_b16, x):
    hi = x.astype(BF16)
    r1 = x - hi.astype(F32)
    mid = r1.astype(BF16)
    lo = (r1 - mid.astype(F32)).astype(BF16)
    return _bdot(mat_b16, hi) + _bdot(mat_b16, mid) + _bdot(mat_b16, lo)


def _silu(x):
    return x * jax.nn.sigmoid(x)


def _log_sigmoid(x):
    return jnp.minimum(x, 0.0) - jnp.log1p(jnp.exp(-jnp.abs(x)))


def _head_norm(y):
    mu = jnp.mean(y, axis=-1, keepdims=True)
    yc = y - mu
    var = jnp.mean(yc * yc, axis=-1, keepdims=True)
    return yc * lax.rsqrt(var + EPS)


def _rotary(x, c2, s2):
    return x * c2 + pltpu.roll(x, HEAD_D // 2, 1) * s2


def _rmsnorm(x, g):
    ms = jnp.mean(x * x, axis=-1, keepdims=True)
    return (x * lax.rsqrt(ms + EPS)) * g


def _resident(shape, imap):
    return pl.BlockSpec(shape, imap, pipeline_mode=pl.Buffered(1))


def _inproj_kernel(x_ref, g_ref, w_ref, wif_ref, p_ref, pif_ref, hb_ref):
    @pl.when(pl.program_id(1) == 0)
    def _():
        hb = _rmsnorm(x_ref[...], g_ref[...]).astype(BF16)
        hb_ref[...] = hb
        pif_ref[...] = _bdot(hb, wif_ref[...])

    p_ref[...] = _bdot(hb_ref[...], w_ref[...])


def _inproj(x, g, w_all, wif, layer, tm, tn):
    m = x.shape[0]
    nif = wif.shape[1]
    return pl.pallas_call(
        _inproj_kernel,
        grid=(m // tm, N_MAIN // tn),
        in_specs=[
            pl.BlockSpec((tm, D_MODEL), lambda i, j: (i, 0)),
            pl.BlockSpec((1, D_MODEL), lambda i, j: (0, 0)),
            pl.BlockSpec((None, D_MODEL, tn), lambda i, j: (layer, 0, j)),
            pl.BlockSpec((D_MODEL, nif), lambda i, j: (0, 0)),
        ],
        out_specs=[
            pl.BlockSpec((tm, tn), lambda i, j: (i, j)),
            pl.BlockSpec((tm, nif), lambda i, j: (i, 0)),
        ],
        out_shape=[jax.ShapeDtypeStruct((m, N_MAIN), F32), jax.ShapeDtypeStruct((m, nif), F32)],
        scratch_shapes=[pltpu.VMEM((tm, D_MODEL), BF16)],
        compiler_params=pltpu.CompilerParams(
            dimension_semantics=("arbitrary", "arbitrary"), vmem_limit_bytes=VMEM_LIMIT),
        name="inproj",
    )(x, g, w_all, wif)


def _merge_kernel(x_ref, g_ref, o_ref, wg_ref, wbr_ref, wout_ref, fg_ref, y_ref, hb_ref, *, final):
    x = x_ref[...]
    hb_ref[...] = _rmsnorm(x, g_ref[...]).astype(BF16)
    acc = None
    for n in range(N_BRANCH):
        proj = _bdot(o_ref[:, n * BRANCH_W:(n + 1) * BRANCH_W], wbr_ref[n])
        gate = jax.nn.sigmoid(_bdot(hb_ref[...], wg_ref[:, n * D_MODEL:(n + 1) * D_MODEL]))
        term = gate * proj
        acc = term if acc is None else acc + term
    y = x + _bdot(acc.astype(BF16), wout_ref[...])
    if final:
        y = _rmsnorm(y, fg_ref[...])
    y_ref[...] = y


def _merge(x, g, o, wg, wbr, wout, fg, final, tm):
    m = x.shape[0]
    resident = lambda shape: _resident(shape, lambda i: (0,) * len(shape))
    return pl.pallas_call(
        functools.partial(_merge_kernel, final=final),
        grid=(m // tm,),
        in_specs=[
            pl.BlockSpec((tm, D_MODEL), lambda i: (i, 0)),
            resident((1, D_MODEL)),
            pl.BlockSpec((tm, N_BRANCH * BRANCH_W), lambda i: (i, 0)),
            resident((D_MODEL, N_GATE)),
            resident((N_BRANCH, BRANCH_W, D_MODEL)),
            resident((D_MODEL, D_MODEL)),
            resident((1, D_MODEL)),
        ],
        out_specs=pl.BlockSpec((tm, D_MODEL), lambda i: (i, 0)),
        out_shape=jax.ShapeDtypeStruct((m, D_MODEL), F32),
        scratch_shapes=[pltpu.VMEM((tm, D_MODEL), BF16)],
        compiler_params=pltpu.CompilerParams(
            dimension_semantics=("arbitrary",), vmem_limit_bytes=VMEM_LIMIT),
        name="merge",
    )(x, g, o, wg, wbr, wout, fg)


def _tile_masks(sample):
    row = lax.broadcasted_iota(jnp.int32, (ROWS, ROWS), 0)
    col = lax.broadcasted_iota(jnp.int32, (ROWS, ROWS), 1)
    if not sample:
        return row >= col, None
    same = (row & (SAMPLE_SEQS - 1)) == (col & (SAMPLE_SEQS - 1))
    return jnp.logical_and(same, row >= col), same


def _ml_prelims(pi, pf, bi, bfo, cmask, smask):
    ai = pi + bi
    lf = _log_sigmoid(pf + bfo)
    tri = jnp.where(cmask, 1.0, 0.0).astype(BF16)
    if smask is None:
        bones = jnp.ones((ROWS, ROWS), BF16)
    else:
        bones = jnp.where(smask, 1.0, 0.0).astype(BF16)
    fcum = _exact_mm(tri, lf)
    ftot = _exact_mm(bones, lf)
    g = (ftot - fcum) + ai
    return dict(ai_t=ai.T, fcum=fcum, fcum_t=fcum.T, ftot=ftot, g=g, g_t=g.T, bones=bones)


def _ml_gates(pre, col, m_prev_c, cmask, smask):
    f_c = pre["fcum"][:, col:col + 1]
    f_r = pre["fcum_t"][col:col + 1, :]
    ig_r = pre["ai_t"][col:col + 1, :]
    g_c = pre["g"][:, col:col + 1]
    g_r = pre["g_t"][col:col + 1, :]
    ftot_c = pre["ftot"][:, col:col + 1]
    logw = jnp.where(cmask, (f_c - f_r) + ig_r, -jnp.inf)
    b = f_c + m_prev_c
    m_t = jnp.maximum(jnp.max(logw, axis=1, keepdims=True), b)
    wts = jnp.exp(logw - m_t)
    inter = jnp.exp(b - m_t)
    if smask is None:
        gmax = jnp.max(g_r, axis=1, keepdims=True)
    else:
        gmax = jnp.max(jnp.where(smask, g_r, -jnp.inf), axis=1, keepdims=True)
    m_new = jnp.maximum(ftot_c + m_prev_c, gmax)
    ws = jnp.exp(g_c - m_new)
    dec = jnp.exp((ftot_c + m_prev_c) - m_new)
    return dict(m_t=m_t, wts=wts, inter=inter, m_new=m_new, ws=ws, dec=dec)


def _ml_output(gt, qb, kb, vb, inter_term, qn):
    s = _bdot_nt(qb, kb) * gt["wts"]
    num = _bdot(s.astype(BF16), vb) + inter_term * gt["inter"]
    den = jnp.sum(s, axis=1, keepdims=True) + qn * gt["inter"]
    return num / jnp.maximum(jnp.abs(den), jnp.exp(-gt["m_t"]))


def _branch_c_post(conv, c_z, cb, lng, lnb):
    y = conv + cb
    mu = jnp.mean(y, axis=-1, keepdims=True)
    yc = y - mu
    var = jnp.mean(yc * yc, axis=-1, keepdims=True)
    ln = (yc * lax.rsqrt(var + EPS)) * lng + lnb
    return _silu(c_z) * _silu(ln)


_GROUPS = ((0, 2048), (2048, 4096), (4096, 5632), (5632, N_MAIN))


def _mix_chunk(hb_ref, w_ref, wif_ref, or0, o_ref, tab, prm, scr, cdec):
    c2_ref, s2_ref, decay_ref, qdec_ref, kdec_ref = tab
    wa_ref, wc_ref, cb_ref, lng_ref, lnb_ref, rgn_ref, mgn_ref, bi_ref, bf_ref = prm
    exta, extc, shc, s_scr, c_scr, n_scr, m_scr = scr
    orows = slice(or0, or0 + ROWS)
    projected = {}

    def pcol(name, lo=0, w=BRANCH_W):
        c0 = _COL[name]
        g0, g1 = next(g for g in _GROUPS if g[0] <= c0 < g[1])
        if g0 not in projected:
            projected[g0] = _bdot(hb_ref[...], w_ref[:, g0:g1])
        return projected[g0][:, c0 - g0 + lo:c0 - g0 + lo + w]

    ua = pcol("a_c") * pcol("a_u")
    exta[SUBLANES:SUBLANES + ROWS, :] = ua
    conv_a = (wa_ref[0:1, :] * exta[SUBLANES - 2:SUBLANES - 2 + ROWS, :]
              + wa_ref[1:2, :] * exta[SUBLANES - 1:SUBLANES - 1 + ROWS, :] + wa_ref[2:3, :] * ua)
    o_a = _silu(pcol("a_z")) * (pcol("a_b") * conv_a)
    o_ref[orows, 0:512] = o_a.astype(BF16)
    exta[0:SUBLANES, :] = exta[ROWS:ROWS + SUBLANES, :]

    extc[C_TAIL:C_TAIL + ROWS, :] = pcol("glu_v") * jax.nn.sigmoid(pcol("glu_g"))
    span = ROWS + C_TAIL - SUBLANES
    conv_blocks = []
    for lb in range(BRANCH_W // 128):
        lanes = slice(lb * 128, (lb + 1) * 128)
        sh = shc.at[lb % 2]
        for ph in range(1, SUBLANES):
            sh[ph - 1] = extc[ph:ph + span, lanes]
        acc = None
        for ph in range(SUBLANES):
            for a in range(span // SUBLANES - ROWS // SUBLANES + 1):
                j = a * SUBLANES + ph - C_PAD
                if 0 <= j < CONV_C_W:
                    arows = slice(a * SUBLANES, a * SUBLANES + ROWS)
                    shifted = extc[arows, lanes] if ph == 0 else sh[ph - 1, arows, :]
                    term = wc_ref[j:j + 1, lanes] * shifted
                    acc = term if acc is None else acc + term
        conv_blocks.append(acc + wc_ref[CONV_C_W - 1:CONV_C_W, lanes] * extc[C_TAIL:C_TAIL + ROWS, lanes])
    conv_c = jnp.concatenate(conv_blocks, axis=1)
    o_c = _branch_c_post(conv_c, pcol("c_z"), cb_ref[...], lng_ref[...], lnb_ref[...])
    o_ref[orows, 1024:1536] = o_c.astype(BF16)
    extc[0:C_TAIL, :] = extc[ROWS:ROWS + C_TAIL, :]

    c2 = c2_ref[orows, :]
    s2 = s2_ref[orows, :]
    for h in range(HEADS):
        lo = h * HEAD_D
        hs = slice(lo, lo + HEAD_D)
        q = _rotary(pcol("r_q", lo, HEAD_D), c2, s2)
        k = _rotary(pcol("r_k", lo, HEAD_D), c2, s2) * (HEAD_D ** -0.5)
        qb = q.astype(BF16)
        kb = k.astype(BF16)
        vb = pcol("r_v", lo, HEAD_D).astype(BF16)
        s_old = s_scr[h]
        sc = _bdot_nt(qb, kb) * decay_ref[h]
        ret = _bdot(sc.astype(BF16), vb) + _bdot(qb, s_old.astype(BF16)) * qdec_ref[:, hs]
        s_scr[h] = s_old * cdec[h] + _bdot((k * kdec_ref[:, hs]).T.astype(BF16), vb)
        o_r = _silu(pcol("r_z", lo, HEAD_D)) * (_head_norm(ret) * rgn_ref[:, hs])
        o_ref[orows, 512 + lo:512 + lo + HEAD_D] = o_r.astype(BF16)

    cmask, smask = _tile_masks(False)
    pif = _bdot(hb_ref[...], wif_ref[...])
    pre = _ml_prelims(pif[:, 0:128], pif[:, 128:256], bi_ref[...], bf_ref[...], cmask, smask)
    lane = lax.broadcasted_iota(jnp.int32, (ROWS, ROWS), 1)
    m_rows = m_scr[...]
    m_rows_new = jnp.zeros((ROWS, ROWS), F32)
    for h in range(HEADS):
        lo = h * HEAD_D
        hs = slice(lo, lo + HEAD_D)
        gt = _ml_gates(pre, h, m_rows[:, h:h + 1], cmask, smask)
        q = pcol("m_q", lo, HEAD_D)
        k = pcol("m_k", lo, HEAD_D) * (HEAD_D ** -0.5)
        qb = q.astype(BF16)
        kb = k.astype(BF16)
        vb = pcol("m_v", lo, HEAD_D).astype(BF16)
        c_old = c_scr[h]
        n_old = n_scr[:, hs]
        qn = jnp.sum(q * n_old, axis=1, keepdims=True)
        hout = _ml_output(gt, qb, kb, vb, _bdot(qb, c_old.astype(BF16)), qn)
        kw = k * gt["ws"]
        dec_full = jnp.broadcast_to(gt["dec"], (ROWS, ROWS))
        c_scr[h] = c_old * dec_full + _bdot(kw.T.astype(BF16), vb)
        n_scr[:, hs] = n_old * dec_full + _bdot(pre["bones"], kw.astype(BF16))
        m_rows_new = jnp.where(lane == h, jnp.broadcast_to(gt["m_new"], (ROWS, ROWS)), m_rows_new)
        hm = jax.nn.sigmoid(pcol("m_o", lo, HEAD_D)) * hout
        o_m = _silu(pcol("m_z", lo, HEAD_D)) * (_head_norm(hm) * mgn_ref[:, hs])
        o_ref[orows, 1536 + lo:1536 + lo + HEAD_D] = o_m.astype(BF16)
    m_scr[...] = m_rows_new


def _xmix_kernel(x_ref, g_ref, w_ref, wif_ref,
                 c2_ref, s2_ref, decay_ref, qdec_ref, kdec_ref,
                 wa_ref, wc_ref, cb_ref, lng_ref, lnb_ref, rgn_ref, mgn_ref, bi_ref, bf_ref,
                 o_ref, bufa_ref, sret_ref, bufc_ref, cml_ref, nml_ref, mml_ref,
                 hb_scr, exta, extc, shc, s_scr, c_scr, n_scr, m_scr,
                 *, cdec, steps_per_seq):
    s = pl.program_id(0)
    seq_step = s % steps_per_seq
    tab = (c2_ref, s2_ref, decay_ref, qdec_ref, kdec_ref)
    prm = (wa_ref, wc_ref, cb_ref, lng_ref, lnb_ref, rgn_ref, mgn_ref, bi_ref, bf_ref)
    scr = (exta, extc, shc, s_scr, c_scr, n_scr, m_scr)
    nchunk = STEP_ROWS // ROWS

    @pl.when(seq_step == 0)
    def _():
        exta[0:SUBLANES, :] = jnp.zeros((SUBLANES, BRANCH_W), F32)
        extc[0:C_TAIL, :] = jnp.zeros((C_TAIL, BRANCH_W), F32)
        s_scr[...] = jnp.zeros_like(s_scr)
        c_scr[...] = jnp.zeros_like(c_scr)
        n_scr[...] = jnp.zeros_like(n_scr)
        m_scr[...] = jnp.zeros_like(m_scr)

    for ci in range(nchunk):
        hb = hb_scr.at[ci % 2]
        hb[...] = _rmsnorm(x_ref[ci * ROWS:(ci + 1) * ROWS, :], g_ref[...]).astype(BF16)
        _mix_chunk(hb, w_ref, wif_ref, ci * ROWS, o_ref, tab, prm, scr, cdec)

    @pl.when(seq_step == steps_per_seq - 1)
    def _():
        bufa_ref[0] = exta[SUBLANES - (CONV_A_W - 1):SUBLANES, :]
        bufc_ref[0] = extc[C_PAD:C_TAIL, :]
        for h in range(HEADS):
            sret_ref[0, h] = s_scr[h]
            cml_ref[0, h] = c_scr[h]
            nml_ref[0, h:h + 1, :] = n_scr[0:1, h * HEAD_D:(h + 1) * HEAD_D]
        mml_ref[0] = m_scr[0:1, :]


def _xmix_prompt(x, gnorm, w_all, wif, layer, consts, prm, batch, seq):
    steps_per_seq = seq // STEP_ROWS
    nstep = batch * steps_per_seq
    c1 = lambda shape: _resident(shape, lambda s: (0,) * len(shape))
    rope = pl.BlockSpec((STEP_ROWS, HEAD_D), lambda s: (s % steps_per_seq, 0))
    in_specs = [
        pl.BlockSpec((STEP_ROWS, D_MODEL), lambda s: (s, 0)),
        c1((1, D_MODEL)),
        _resident((None, D_MODEL, W_BLOCK), lambda s: (layer, 0, 0)),
        c1((D_MODEL, N_IF)),
        rope, rope,
        c1((HEADS, ROWS, ROWS)),
        c1((ROWS, BRANCH_W)),
        c1((ROWS, BRANCH_W)),
        c1((CONV_A_W, BRANCH_W)),
        c1((CONV_C_W, BRANCH_W)),
    ] + [c1((1, BRANCH_W))] * 5 + [c1((1, 128))] * 2
    per_b3 = lambda s: (s // steps_per_seq, 0, 0)
    per_b4 = lambda s: (s // steps_per_seq, 0, 0, 0)
    out_specs = [
        pl.BlockSpec((STEP_ROWS, N_BRANCH * BRANCH_W), lambda s: (s, 0)),
        pl.BlockSpec((1, CONV_A_W - 1, BRANCH_W), per_b3),
        pl.BlockSpec((1, HEADS, HEAD_D, HEAD_D), per_b4),
        pl.BlockSpec((1, CONV_C_W - 1, BRANCH_W), per_b3),
        pl.BlockSpec((1, HEADS, HEAD_D, HEAD_D), per_b4),
        pl.BlockSpec((1, HEADS, HEAD_D), per_b3),
        pl.BlockSpec((1, 1, 128), per_b3),
    ]
    out_shape = [
        jax.ShapeDtypeStruct((batch * seq, N_BRANCH * BRANCH_W), BF16),
        jax.ShapeDtypeStruct((batch, CONV_A_W - 1, BRANCH_W), F32),
        jax.ShapeDtypeStruct((batch, HEADS, HEAD_D, HEAD_D), F32),
        jax.ShapeDtypeStruct((batch, CONV_C_W - 1, BRANCH_W), F32),
        jax.ShapeDtypeStruct((batch, HEADS, HEAD_D, HEAD_D), F32),
        jax.ShapeDtypeStruct((batch, HEADS, HEAD_D), F32),
        jax.ShapeDtypeStruct((batch, 1, 128), F32),
    ]
    scratch = [
        pltpu.VMEM((2, ROWS, D_MODEL), BF16),
        pltpu.VMEM((ROWS + SUBLANES, BRANCH_W), F32),
        pltpu.VMEM((ROWS + C_TAIL, BRANCH_W), F32),
        pltpu.VMEM((2, SUBLANES - 1, ROWS + C_TAIL - SUBLANES, 128), F32),
        pltpu.VMEM((HEADS, HEAD_D, HEAD_D), F32),
        pltpu.VMEM((HEADS, HEAD_D, HEAD_D), F32),
        pltpu.VMEM((ROWS, BRANCH_W), F32),
        pltpu.VMEM((ROWS, ROWS), F32),
    ]
    return pl.pallas_call(
        functools.partial(_xmix_kernel, cdec=consts["cdec"], steps_per_seq=steps_per_seq),
        grid=(nstep,),
        in_specs=in_specs, out_specs=out_specs, out_shape=out_shape, scratch_shapes=scratch,
        compiler_params=pltpu.CompilerParams(
            dimension_semantics=("arbitrary",), vmem_limit_bytes=VMEM_LIMIT),
        name="xmix_prompt",
    )(x, gnorm, w_all, wif, consts["c2"], consts["s2"], consts["decay"], consts["qdec"],
      consts["kdec"], prm["wa"], prm["wc"], prm["cb"], prm["lng"], prm["lnb"], prm["rgn"],
      prm["mgn"], prm["bi"], prm["bf"])


def _mix_sample_kernel(pa_ref, pc_ref, rq_ref, rk_ref, rv_ref, rz_ref, mq_ref, mk_ref, mv_ref,
                       mo_ref, mz_ref, pif_ref, c2_ref, s2_ref, decay_ref, qdec_ref, kdec_ref,
                       cdec_ref, wa_ref, wc_ref, cb_ref, lng_ref, lnb_ref, rgn_ref, mgn_ref,
                       bi_ref, bf_ref, bufa_ref, sret_ref, bufc_ref, cml_ref, nrows_ref, mrows_ref,
                       o_ref, bufa_o, sret_o, bufc_o, cml_o, nrows_o, mrows_o, dec_scr):
    h = pl.program_id(1)
    sq = SAMPLE_SEQS
    nt = ROWS // sq

    @pl.when(h == 0)
    def _():
        ua = pa_ref[:, 512:1024] * pa_ref[:, 1024:1536]
        ea = [bufa_ref[:, 0, :], bufa_ref[:, 1, :]] + [ua[t * sq:(t + 1) * sq, :] for t in range(nt)]
        conv_a = jnp.concatenate(
            [wa_ref[0:1, :] * ea[t] + wa_ref[1:2, :] * ea[t + 1] + wa_ref[2:3, :] * ea[t + 2]
             for t in range(nt)], axis=0)
        o_ref[:, 0:512] = (_silu(pa_ref[:, 1536:2048]) * (pa_ref[:, 0:512] * conv_a)).astype(BF16)
        bufa_o[:, 0, :] = ea[nt]
        bufa_o[:, 1, :] = ea[nt + 1]
        uc = pc_ref[:, 0:512] * jax.nn.sigmoid(pc_ref[:, 512:1024])
        nb = CONV_C_W - 1
        ec = [bufc_ref[:, i, :] for i in range(nb)] + [uc[t * sq:(t + 1) * sq, :] for t in range(nt)]
        planes = []
        for t in range(nt):
            acc = wc_ref[0:1, :] * ec[t]
            for j in range(1, CONV_C_W):
                acc = acc + wc_ref[j:j + 1, :] * ec[t + j]
            planes.append(acc)
        conv_c = jnp.concatenate(planes, axis=0)
        o_ref[:, 1024:1536] = _branch_c_post(conv_c, pc_ref[:, 1024:1536], cb_ref[...], lng_ref[...],
                                             lnb_ref[...]).astype(BF16)
        for i in range(nb):
            bufc_o[:, i, :] = ec[i + nt]

    cmask, smask = _tile_masks(True)
    rowseq = lax.broadcasted_iota(jnp.int32, (ROWS, ROWS), 0) & (sq - 1)
    laneseq = lax.broadcasted_iota(jnp.int32, (ROWS, ROWS), 1) & (sq - 1)

    c2 = c2_ref[...]
    s2 = s2_ref[...]
    rq = _rotary(rq_ref[...], c2, s2)
    rk = _rotary(rk_ref[...], c2, s2) * (HEAD_D ** -0.5)
    rqb = rq.astype(BF16)
    rkb = rk.astype(BF16)
    rvb = rv_ref[...].astype(BF16)
    rkd_t = (rk * kdec_ref[...]).T
    cdec = cdec_ref[0, 0:1, :]

    pre = _ml_prelims(pif_ref[:, 0:128], pif_ref[:, 128:256], bi_ref[...], bf_ref[...], cmask, smask)
    m_prev = mrows_ref[...]
    gt = _ml_gates(pre, 0, m_prev[:, 0:1], cmask, smask)
    mq = mq_ref[...]
    mk = mk_ref[...] * (HEAD_D ** -0.5)
    mqb = mq.astype(BF16)
    mkb = mk.astype(BF16)
    mvb = mv_ref[...].astype(BF16)
    kw = mk * gt["ws"]
    kw_t = kw.T
    dec_full = jnp.broadcast_to(gt["dec"], (ROWS, ROWS))
    dec_scr[...] = dec_full

    def per_seq(b, carry):
        inter_r, inter_m = carry
        sb = sret_ref[b, 0]
        inter_r = jnp.where(rowseq == b, _bdot(rqb, sb.astype(BF16)), inter_r)
        k_sel = jnp.where(laneseq == b, rkd_t, 0.0).astype(BF16)
        sret_o[b, 0] = sb * cdec + _bdot(k_sel, rvb)
        cb_ = cml_ref[b, 0]
        inter_m = jnp.where(rowseq == b, _bdot(mqb, cb_.astype(BF16)), inter_m)
        kw_sel = jnp.where(laneseq == b, kw_t, 0.0).astype(BF16)
        cml_o[b, 0] = cb_ * dec_scr[pl.ds(b, 1), :] + _bdot(kw_sel, mvb)
        return inter_r, inter_m

    zero = jnp.zeros((ROWS, HEAD_D), F32)
    inter_r, inter_m = lax.fori_loop(0, sq, per_seq, (zero, zero), unroll=8)

    sc = _bdot_nt(rqb, rkb) * decay_ref[0]
    ret = _bdot(sc.astype(BF16), rvb) + inter_r * qdec_ref[...]
    o_r = (_silu(rz_ref[...]) * (_head_norm(ret) * rgn_ref[...])).astype(BF16)

    n_old = nrows_ref[...]
    qn = jnp.sum(mq * n_old, axis=1, keepdims=True)
    hout = _ml_output(gt, mqb, mkb, mvb, inter_m, qn)
    nrows_o[...] = n_old * dec_full + _bdot(pre["bones"], kw.astype(BF16))
    mrows_o[...] = jnp.broadcast_to(gt["m_new"], (ROWS, ROWS))
    hm = jax.nn.sigmoid(mo_ref[...]) * hout
    o_m = (_silu(mz_ref[...]) * (_head_norm(hm) * mgn_ref[...])).astype(BF16)

    for hh in range(HEADS):
        @pl.when(h == hh)
        def _(hh=hh):
            o_ref[:, 512 + hh * HEAD_D:512 + (hh + 1) * HEAD_D] = o_r
            o_ref[:, 1536 + hh * HEAD_D:1536 + (hh + 1) * HEAD_D] = o_m


def _mix_sample(p, pif, consts, prm, st, nseq, layer, depth, stacked):
    ntile = nseq // SAMPLE_SEQS
    sq = SAMPLE_SEQS
    head_cols = lambda name: pl.BlockSpec(
        (ROWS, HEAD_D), lambda i, h, _o=_COL[name] // HEAD_D: (i, _o + h))
    const2 = lambda i, h: (0, 0)
    per_head2 = lambda i, h: (0, h)
    conv_a_state = pl.BlockSpec((None, sq, CONV_A_W - 1, BRANCH_W), lambda i, h: (layer, i, 0, 0))
    conv_c_state = pl.BlockSpec((None, sq, CONV_C_W - 1, BRANCH_W), lambda i, h: (layer, i, 0, 0))
    matrix_state = pl.BlockSpec((None, sq, 1, HEAD_D, HEAD_D), lambda i, h: (layer, i, h, 0, 0))
    in_specs = [
        pl.BlockSpec((ROWS, 2048), lambda i, h: (i, 0)),
        pl.BlockSpec((ROWS, 2048), lambda i, h: (i, 2)),
    ] + [head_cols(n) for n in ("r_q", "r_k", "r_v", "r_z", "m_q", "m_k", "m_v", "m_o", "m_z")] + [
        pl.BlockSpec((ROWS, 256), lambda i, h: (i, h)),
        pl.BlockSpec((ROWS, HEAD_D), const2),
        pl.BlockSpec((ROWS, HEAD_D), const2),
        pl.BlockSpec((1, ROWS, ROWS), lambda i, h: (h, 0, 0)),
        pl.BlockSpec((ROWS, HEAD_D), per_head2),
        pl.BlockSpec((ROWS, HEAD_D), per_head2),
        pl.BlockSpec((1, 8, 128), lambda i, h: (h, 0, 0)),
        pl.BlockSpec((CONV_A_W, BRANCH_W), const2),
        pl.BlockSpec((CONV_C_W, BRANCH_W), const2),
        pl.BlockSpec((1, BRANCH_W), const2),
        pl.BlockSpec((1, BRANCH_W), const2),
        pl.BlockSpec((1, BRANCH_W), const2),
        pl.BlockSpec((1, HEAD_D), per_head2),
        pl.BlockSpec((1, HEAD_D), per_head2),
        pl.BlockSpec((1, 128), per_head2),
        pl.BlockSpec((1, 128), per_head2),
        conv_a_state, matrix_state, conv_c_state, matrix_state,
        pl.BlockSpec((ROWS, HEAD_D), lambda i, h: (i, h)),
        pl.BlockSpec((ROWS, HEAD_D), lambda i, h: (i, h)),
    ]
    out_specs = [
        pl.BlockSpec((ROWS, N_BRANCH * BRANCH_W), lambda i, h: (i, 0)),
        conv_a_state, matrix_state, conv_c_state, matrix_state,
        pl.BlockSpec((ROWS, HEAD_D), lambda i, h: (i, h)),
        pl.BlockSpec((ROWS, HEAD_D), lambda i, h: (i, h)),
    ]
    nrow = ntile * ROWS
    out_shape = [
        jax.ShapeDtypeStruct((nrow, N_BRANCH * BRANCH_W), BF16),
        jax.ShapeDtypeStruct((depth, nseq, CONV_A_W - 1, BRANCH_W), F32),
        jax.ShapeDtypeStruct((depth, nseq, HEADS, HEAD_D, HEAD_D), F32),
        jax.ShapeDtypeStruct((depth, nseq, CONV_C_W - 1, BRANCH_W), F32),
        jax.ShapeDtypeStruct((depth, nseq, HEADS, HEAD_D, HEAD_D), F32),
        jax.ShapeDtypeStruct((nrow, HEADS * HEAD_D), F32),
        jax.ShapeDtypeStruct((nrow, HEADS * HEAD_D), F32),
    ]
    args = [p] * 11 + [
        pif, consts["c2"], consts["s2"], consts["decay"], consts["qdec"], consts["kdec"],
        consts["cdec"], prm["wa"], prm["wc"], prm["cb"], prm["lng"], prm["lnb"], prm["rgn"],
        prm["mgn"], prm["bi_h"], prm["bf_h"],
        st["bufa"], st["sret"], st["bufc"], st["cml"], st["nrows"], st["mrows"]]
    assert len(args) == len(in_specs)
    n_in = len(args)
    aliases = {}
    if stacked is not None:
        aliases = {n_in + k: 1 + k for k in range(4)}
        in_specs = in_specs + [pl.BlockSpec(memory_space=pl.ANY)] * 4
        args = args + list(stacked)

    def entry(*refs):
        _mix_sample_kernel(*refs[:n_in], *refs[n_in + len(aliases):])

    return pl.pallas_call(
        entry,
        grid=(ntile, HEADS),
        in_specs=in_specs, out_specs=out_specs, out_shape=out_shape,
        scratch_shapes=[pltpu.VMEM((ROWS, ROWS), F32)],
        input_output_aliases=aliases,
        compiler_params=pltpu.CompilerParams(
            dimension_semantics=("arbitrary", "arbitrary"), vmem_limit_bytes=VMEM_LIMIT),
        name="mix_sample",
    )(*args)


def _rope_tables(pos):
    inv = ROPE_BASE ** (-jnp.arange(0, HEAD_D, 2, dtype=F32) / HEAD_D)
    ang = pos.astype(F32)[:, None] * inv[None, :]
    cos = jnp.cos(ang)
    sin = jnp.sin(ang)
    return jnp.concatenate([cos, cos], axis=1), jnp.concatenate([-sin, sin], axis=1)


def _decay_tables(t_of_row, seq_of_row, chunk_len):
    log_g = np.log1p(-(2.0 ** (-5.0 - np.arange(HEADS, dtype=np.float64))))
    dt = t_of_row[:, None] - t_of_row[None, :]
    ok = (seq_of_row[:, None] == seq_of_row[None, :]) & (dt >= 0)
    decay = np.where(ok[None], np.exp(np.maximum(dt, 0)[None] * log_g[:, None, None]), 0.0)
    qdec = np.exp((t_of_row[:, None] + 1.0) * log_g[None, :])
    kdec = np.exp((chunk_len - 1.0 - t_of_row)[:, None] * log_g[None, :])
    cdec = np.exp(chunk_len * log_g)
    rep = lambda a: jnp.asarray(np.repeat(a, HEAD_D, axis=1), F32)
    return jnp.asarray(decay, F32), rep(qdec), rep(kdec), cdec


def kernel(x_prompt, x_sample, state_conv_a, state_ret, state_conv_c, state_mlstm_C,
           state_mlstm_n, state_mlstm_m, norm_g, w_in, conv_a_w, conv_c_w, conv_c_b,
           ln_c_g, ln_c_b, ret_gn_g, ml_gn_g, ml_gate_b, w_br, w_out, final_g):
    bp, tp, _ = x_prompt.shape
    bs, ts, _ = x_sample.shape
    depth = w_in.shape[0]
    sq = SAMPLE_SEQS
    ntile = bs // sq
    assert tp % STEP_ROWS == 0 and ts * sq == ROWS and bs % sq == 0

    r = np.arange(ROWS)
    c2p, s2p = _rope_tables(jnp.arange(tp, dtype=jnp.int32))
    decay_p, qdec_p, kdec_p, cdec_p = _decay_tables(r.astype(np.float64), np.zeros(ROWS), float(ROWS))
    consts_p = dict(c2=c2p, s2=s2p, decay=decay_p, qdec=qdec_p, kdec=kdec_p,
                    cdec=tuple(float(v) for v in cdec_p))
    t_s = (r // sq).astype(np.float64)
    c2s, s2s = _rope_tables(PAST_LEN + jnp.asarray(r // sq, jnp.int32))
    decay_s, qdec_s, kdec_s, cdec_s = _decay_tables(t_s, r % sq, float(ts))
    cdec_s_arr = jnp.asarray(np.broadcast_to(cdec_s[:, None, None], (HEADS, 8, 128)), F32)
    consts_s = dict(c2=c2s, s2=s2s, decay=decay_s, qdec=qdec_s, kdec=kdec_s, cdec=cdec_s_arr)

    xp = x_prompt.reshape(bp * tp, D_MODEL)
    xs = x_sample.reshape(ntile, sq, ts, D_MODEL).transpose(0, 2, 1, 3).reshape(bs * ts, D_MODEL)

    fg = final_g.reshape(1, D_MODEL)
    outs_p = [[] for _ in range(6)]
    n_s, m_s = [], []
    stacked = None
    sel_p = np.zeros((2 * HEADS, N_IF), np.float32)
    sel_s = np.zeros((2 * HEADS, HEADS * N_IF), np.float32)
    for h in range(HEADS):
        sel_p[h, h] = 1.0
        sel_p[HEADS + h, 128 + h] = 1.0
        sel_s[h, h * N_IF] = 1.0
        sel_s[HEADS + h, h * N_IF + 128] = 1.0
    w_bf = w_in.astype(BF16)
    pad_row = lambda v: jnp.pad(v, (0, 128 - v.shape[0])).reshape(1, 128)
    for l in range(depth):
        w_if = w_in[l, :, N_MAIN:N_MAIN + 2 * HEADS]
        spread = lambda sel: jnp.dot(w_if, jnp.asarray(sel), precision=lax.Precision.HIGHEST).astype(BF16)
        w_gate = w_bf[l, :, N_MAIN + 2 * HEADS:]
        gb = ml_gate_b[l]
        prm = dict(
            wa=conv_a_w[l], wc=conv_c_w[l], cb=conv_c_b[l].reshape(1, -1),
            lng=ln_c_g[l].reshape(1, -1), lnb=ln_c_b[l].reshape(1, -1),
            rgn=ret_gn_g[l].reshape(1, -1), mgn=ml_gn_g[l].reshape(1, -1),
            bi=pad_row(gb[:HEADS]), bf=pad_row(gb[HEADS:]),
            bi_h=jnp.concatenate([pad_row(gb[h:h + 1]) for h in range(HEADS)], axis=1),
            bf_h=jnp.concatenate([pad_row(gb[HEADS + h:HEADS + h + 1]) for h in range(HEADS)], axis=1),
        )
        gnorm = norm_g[l].reshape(1, D_MODEL)
        wbr = w_br[l].astype(BF16)
        wout = w_out[l].astype(BF16)
        final = l == depth - 1

        res = _xmix_prompt(xp, gnorm, w_bf, spread(sel_p), l, consts_p, prm, bp, tp)
        for k in range(5):
            outs_p[k].append(res[1 + k])
        outs_p[5].append(res[6][:, 0, :HEADS])
        xp = _merge(xp, gnorm, res[0], w_gate, wbr, wout, fg, final, tm=512)

        ps, pifs = _inproj(xs, gnorm, w_bf, spread(sel_s), l, tm=bs * ts, tn=2048)
        n_rows = jnp.broadcast_to(
            state_mlstm_n[l].reshape(ntile, 1, sq, HEADS * HEAD_D),
            (ntile, ts, sq, HEADS * HEAD_D)).reshape(bs * ts, HEADS * HEAD_D)
        m_rows = jnp.broadcast_to(
            state_mlstm_m[l].reshape(ntile, 1, sq, HEADS, 1),
            (ntile, ts, sq, HEADS, HEAD_D)).reshape(bs * ts, HEADS * HEAD_D)
        st = dict(bufa=state_conv_a, sret=state_ret, bufc=state_conv_c, cml=state_mlstm_C,
                  nrows=n_rows, mrows=m_rows)
        res = _mix_sample(ps, pifs, consts_s, prm, st, bs, l, depth, stacked)
        stacked = tuple(res[1:5])
        n_s.append(res[5].reshape(ntile, ts, sq, HEADS, HEAD_D)[:, 0].reshape(bs, HEADS, HEAD_D))
        m_s.append(res[6].reshape(ntile, ts, sq, HEADS, HEAD_D)[:, 0, :, :, 0].reshape(bs, HEADS))
        xs = _merge(xs, gnorm, res[0], w_gate, wbr, wout, fg, final, tm=bs * ts)

    y_prompt = xp.reshape(bp, tp, D_MODEL)
    y_sample = xs.reshape(ntile, ts, sq, D_MODEL).transpose(0, 2, 1, 3).reshape(bs, ts, D_MODEL)
    sp = [jnp.stack(a, axis=0) for a in outs_p]
    return (y_prompt, y_sample, sp[0], stacked[0], sp[1], stacked[1], sp[2], stacked[2],
            sp[3], stacked[3], sp[4], jnp.stack(n_s, axis=0), sp[5], jnp.stack(m_s, axis=0))
```

```python
import functools

import numpy as np
import jax
import jax.numpy as jnp
from jax import lax
from jax.experimental import pallas as pl
from jax.experimental.pallas import tpu as pltpu

F32 = jnp.float32
BF16 = jnp.bfloat16

D_MODEL = 1024
BRANCH_W = 512
N_BRANCH = 4
HEADS = 4
HEAD_D = 128
CONV_A_W = 3
CONV_C_W = 31
ROPE_BASE = 10000.0
EPS = 1e-6
N_MAIN = 16 * BRANCH_W
N_IF = 256
N_GATE = N_BRANCH * D_MODEL
ROWS = 128
STEP_ROWS = 4 * ROWS
W_BLOCK = N_MAIN + 512
SAMPLE_SEQS = 32
SUBLANES = 8
PAST_LEN = 16384
C_TAIL = -(-(CONV_C_W - 1) // SUBLANES) * SUBLANES
C_PAD = C_TAIL - (CONV_C_W - 1)
VMEM_LIMIT = 56 * 1024 * 1024

_COL = dict(a_b=0, a_c=512, a_u=1024, a_z=1536, r_q=2048, r_k=2560, r_v=3072, r_z=3584,
            glu_v=4096, glu_g=4608, c_z=5120, m_q=5632, m_k=6144, m_v=6656, m_o=7168, m_z=7680)


def _bdot(a, b):
    return jnp.dot(a, b, preferred_element_type=F32)


def _bdot_nt(a, b):
    return lax.dot_general(a, b, (((1,), (1,)), ((), ())), preferred_element_type=F32)


def _exact_mm(mat_b16, x):
    hi = x.astype(BF16)
    r1 = x - hi.astype(F32)
    mid = r1.astype(BF16)
    lo = (r1 - mid.astype(F32)).astype(BF16)
    return _bdot(mat_b16, hi) + _bdot(mat_b16, mid) + _bdot(mat_b16, lo)


def _silu(x):
    return x * jax.nn.sigmoid(x)


def _log_sigmoid(x):
    return jnp.minimum(x, 0.0) - jnp.log1p(jnp.exp(-jnp.abs(x)))


def _head_norm(y):
    mu = jnp.mean(y, axis=-1, keepdims=True)
    yc = y - mu
    var = jnp.mean(yc * yc, axis=-1, keepdims=True)
    return yc * lax.rsqrt(var + EPS)


def _rotary(x, c2, s2):
    return x * c2 + pltpu.roll(x, HEAD_D // 2, 1) * s2


def _rmsnorm(x, g):
    ms = jnp.mean(x * x, axis=-1, keepdims=True)
    return (x * lax.rsqrt(ms + EPS)) * g


def _resident(shape, imap):
    return pl.BlockSpec(shape, imap, pipeline_mode=pl.Buffered(1))


def _inproj_kernel(x_ref, g_ref, w_ref, wif_ref, p_ref, pif_ref, hb_ref):
    @pl.when(pl.program_id(1) == 0)
    def _():
        hb = _rmsnorm(x_ref[...], g_ref[...]).astype(BF16)
        hb_ref[...] = hb
        pif_ref[...] = _bdot(hb, wif_ref[...])

    p_ref[...] = _bdot(hb_ref[...], w_ref[...])


def _inproj(x, g, w_all, wif, layer, tm, tn):
    m = x.shape[0]
    nif = wif.shape[1]
    return pl.pallas_call(
        _inproj_kernel,
        grid=(m // tm, N_MAIN // tn),
        in_specs=[
            pl.BlockSpec((tm, D_MODEL), lambda i, j: (i, 0)),
            pl.BlockSpec((1, D_MODEL), lambda i, j: (0, 0)),
            pl.BlockSpec((None, D_MODEL, tn), lambda i, j: (layer, 0, j)),
            pl.BlockSpec((D_MODEL, nif), lambda i, j: (0, 0)),
        ],
        out_specs=[
            pl.BlockSpec((tm, tn), lambda i, j: (i, j)),
            pl.BlockSpec((tm, nif), lambda i, j: (i, 0)),
        ],
        out_shape=[jax.ShapeDtypeStruct((m, N_MAIN), F32), jax.ShapeDtypeStruct((m, nif), F32)],
        scratch_shapes=[pltpu.VMEM((tm, D_MODEL), BF16)],
        compiler_params=pltpu.CompilerParams(
            dimension_semantics=("arbitrary", "arbitrary"), vmem_limit_bytes=VMEM_LIMIT),
        name="inproj",
    )(x, g, w_all, wif)


def _merge_kernel(x_ref, g_ref, o_ref, wg_ref, wbr_ref, wout_ref, fg_ref, y_ref, hb_ref, *, final):
    x = x_ref[...]
    hb_ref[...] = _rmsnorm(x, g_ref[...]).astype(BF16)
    acc = None
    for n in range(N_BRANCH):
        proj = _bdot(o_ref[:, n * BRANCH_W:(n + 1) * BRANCH_W], wbr_ref[n])
        gate = jax.nn.sigmoid(_bdot(hb_ref[...], wg_ref[:, n * D_MODEL:(n + 1) * D_MODEL]))
        term = gate * proj
        acc = term if acc is None else acc + term
    y = x + _bdot(acc.astype(BF16), wout_ref[...])
    if final:
        y = _rmsnorm(y, fg_ref[...])
    y_ref[...] = y


def _merge(x, g, o, wg, wbr, wout, fg, final, tm):
    m = x.shape[0]
    resident = lambda shape: _resident(shape, lambda i: (0,) * len(shape))
    return pl.pallas_call(
        functools.partial(_merge_kernel, final=final),
        grid=(m // tm,),
        in_specs=[
            pl.BlockSpec((tm, D_MODEL), lambda i: (i, 0)),
            resident((1, D_MODEL)),
            pl.BlockSpec((tm, N_BRANCH * BRANCH_W), lambda i: (i, 0)),
            resident((D_MODEL, N_GATE)),
            resident((N_BRANCH, BRANCH_W, D_MODEL)),
            resident((D_MODEL, D_MODEL)),
            resident((1, D_MODEL)),
        ],
        out_specs=pl.BlockSpec((tm, D_MODEL), lambda i: (i, 0)),
        out_shape=jax.ShapeDtypeStruct((m, D_MODEL), F32),
        scratch_shapes=[pltpu.VMEM((tm, D_MODEL), BF16)],
        compiler_params=pltpu.CompilerParams(
            dimension_semantics=("arbitrary",), vmem_limit_bytes=VMEM_LIMIT),
        name="merge",
    )(x, g, o, wg, wbr, wout, fg)


def _tile_masks(sample):
    row = lax.broadcasted_iota(jnp.int32, (ROWS, ROWS), 0)
    col = lax.broadcasted_iota(jnp.int32, (ROWS, ROWS), 1)
    if not sample:
        return row >= col, None
    same = (row & (SAMPLE_SEQS - 1)) == (col & (SAMPLE_SEQS - 1))
    return jnp.logical_and(same, row >= col), same


def _ml_prelims(pi, pf, bi, bfo, cmask, smask):
    ai = pi + bi
    lf = _log_sigmoid(pf + bfo)
    tri = jnp.where(cmask, 1.0, 0.0).astype(BF16)
    if smask is None:
        bones = jnp.ones((ROWS, ROWS), BF16)
    else:
        bones = jnp.where(smask, 1.0, 0.0).astype(BF16)
    fcum = _exact_mm(tri, lf)
    ftot = _exact_mm(bones, lf)
    g = (ftot - fcum) + ai
    return dict(ai_t=ai.T, fcum=fcum, fcum_t=fcum.T, ftot=ftot, g=g, g_t=g.T, bones=bones)


def _ml_gates(pre, col, m_prev_c, cmask, smask):
    f_c = pre["fcum"][:, col:col + 1]
    f_r = pre["fcum_t"][col:col + 1, :]
    ig_r = pre["ai_t"][col:col + 1, :]
    g_c = pre["g"][:, col:col + 1]
    g_r = pre["g_t"][col:col + 1, :]
    ftot_c = pre["ftot"][:, col:col + 1]
    logw = jnp.where(cmask, (f_c - f_r) + ig_r, -jnp.inf)
    b = f_c + m_prev_c
    m_t = jnp.maximum(jnp.max(logw, axis=1, keepdims=True), b)
    wts = jnp.exp(logw - m_t)
    inter = jnp.exp(b - m_t)
    if smask is None:
        gmax = jnp.max(g_r, axis=1, keepdims=True)
    else:
        gmax = jnp.max(jnp.where(smask, g_r, -jnp.inf), axis=1, keepdims=True)
    m_new = jnp.maximum(ftot_c + m_prev_c, gmax)
    ws = jnp.exp(g_c - m_new)
    dec = jnp.exp((ftot_c + m_prev_c) - m_new)
    return dict(m_t=m_t, wts=wts, inter=inter, m_new=m_new, ws=ws, dec=dec)


def _ml_output(gt, qb, kb, vb, inter_term, qn):
    s = _bdot_nt(qb, kb) * gt["wts"]
    num = _bdot(s.astype(BF16), vb) + inter_term * gt["inter"]
    den = jnp.sum(s, axis=1, keepdims=True) + qn * gt["inter"]
    return num / jnp.maximum(jnp.abs(den), jnp.exp(-gt["m_t"]))


def _branch_c_post(conv, c_z, cb, lng, lnb):
    y = conv + cb
    mu = jnp.mean(y, axis=-1, keepdims=True)
    yc = y - mu
    var = jnp.mean(yc * yc, axis=-1, keepdims=True)
    ln = (yc * lax.rsqrt(var + EPS)) * lng + lnb
    return _silu(c_z) * _silu(ln)


_GROUPS = ((0, 2048), (2048, 4096), (4096, 5632), (5632, N_MAIN))


def _mix_chunk(hb_ref, w_ref, wif_ref, or0, o_ref, tab, prm, scr, cdec):
    c2_ref, s2_ref, decay_ref, qdec_ref, kdec_ref = tab
    wa_ref, wc_ref, cb_ref, lng_ref, lnb_ref, rgn_ref, mgn_ref, bi_ref, bf_ref = prm
    exta, extc, shc, s_scr, c_scr, n_scr, m_scr = scr
    orows = slice(or0, or0 + ROWS)
    projected = {}

    def pcol(name, lo=0, w=BRANCH_W):
        c0 = _COL[name]
        g0, g1 = next(g for g in _GROUPS if g[0] <= c0 < g[1])
        if g0 not in projected:
            projected[g0] = _bdot(hb_ref[...], w_ref[:, g0:g1])
        return projected[g0][:, c0 - g0 + lo:c0 - g0 + lo + w]

    ua = pcol("a_c") * pcol("a_u")
    exta[SUBLANES:SUBLANES + ROWS, :] = ua
    conv_a = (wa_ref[0:1, :] * exta[SUBLANES - 2:SUBLANES - 2 + ROWS, :]
              + wa_ref[1:2, :] * exta[SUBLANES - 1:SUBLANES - 1 + ROWS, :] + wa_ref[2:3, :] * ua)
    o_a = _silu(pcol("a_z")) * (pcol("a_b") * conv_a)
    o_ref[orows, 0:512] = o_a.astype(BF16)
    exta[0:SUBLANES, :] = exta[ROWS:ROWS + SUBLANES, :]

    extc[C_TAIL:C_TAIL + ROWS, :] = pcol("glu_v") * jax.nn.sigmoid(pcol("glu_g"))
    span = ROWS + C_TAIL - SUBLANES
    conv_blocks = []
    for lb in range(BRANCH_W // 128):
        lanes = slice(lb * 128, (lb + 1) * 128)
        sh = shc.at[lb % 2]
        for ph in range(1, SUBLANES):
            sh[ph - 1] = extc[ph:ph + span, lanes]
        acc = None
        for ph in range(SUBLANES):
            for a in range(span // SUBLANES - ROWS // SUBLANES + 1):
                j = a * SUBLANES + ph - C_PAD
                if 0 <= j < CONV_C_W:
                    arows = slice(a * SUBLANES, a * SUBLANES + ROWS)
                    shifted = extc[arows, lanes] if ph == 0 else sh[ph - 1, arows, :]
                    term = wc_ref[j:j + 1, lanes] * shifted
                    acc = term if acc is None else acc + term
        conv_blocks.append(acc + wc_ref[CONV_C_W - 1:CONV_C_W, lanes] * extc[C_TAIL:C_TAIL + ROWS, lanes])
    conv_c = jnp.concatenate(conv_blocks, axis=1)
    o_c = _branch_c_post(conv_c, pcol("c_z"), cb_ref[...], lng_ref[...], lnb_ref[...])
    o_ref[orows, 1024:1536] = o_c.astype(BF16)
    extc[0:C_TAIL, :] = extc[ROWS:ROWS + C_TAIL, :]

    c2 = c2_ref[orows, :]
    s2 = s2_ref[orows, :]
    for h in range(HEADS):
        lo = h * HEAD_D
        hs = slice(lo, lo + HEAD_D)
        q = _rotary(pcol("r_q", lo, HEAD_D), c2, s2)
        k = _rotary(pcol("r_k", lo, HEAD_D), c2, s2) * (HEAD_D ** -0.5)
        qb = q.astype(BF16)
        kb = k.astype(BF16)
        vb = pcol("r_v", lo, HEAD_D).astype(BF16)
        s_old = s_scr[h]
        sc = _bdot_nt(qb, kb) * decay_ref[h]
        ret = _bdot(sc.astype(BF16), vb) + _bdot(qb, s_old.astype(BF16)) * qdec_ref[:, hs]
        s_scr[h] = s_old * cdec[h] + _bdot((k * kdec_ref[:, hs]).T.astype(BF16), vb)
        o_r = _silu(pcol("r_z", lo, HEAD_D)) * (_head_norm(ret) * rgn_ref[:, hs])
        o_ref[orows, 512 + lo:512 + lo + HEAD_D] = o_r.astype(BF16)

    cmask, smask = _tile_masks(False)
    pif = _bdot(hb_ref[...], wif_ref[...])
    pre = _ml_prelims(pif[:, 0:128], pif[:, 128:256], bi_ref[...], bf_ref[...], cmask, smask)
    lane = lax.broadcasted_iota(jnp.int32, (ROWS, ROWS), 1)
    m_rows = m_scr[...]
    m_rows_new = jnp.zeros((ROWS, ROWS), F32)
    for h in range(HEADS):
        lo = h * HEAD_D
        hs = slice(lo, lo + HEAD_D)
        gt = _ml_gates(pre, h, m_rows[:, h:h + 1], cmask, smask)
        q = pcol("m_q", lo, HEAD_D)
        k = pcol("m_k", lo, HEAD_D) * (HEAD_D ** -0.5)
        qb = q.astype(BF16)
        kb = k.astype(BF16)
        vb = pcol("m_v", lo, HEAD_D).astype(BF16)
        c_old = c_scr[h]
        n_old = n_scr[:, hs]
        qn = jnp.sum(q * n_old, axis=1, keepdims=True)
        hout = _ml_output(gt, qb, kb, vb, _bdot(qb, c_old.astype(BF16)), qn)
        kw = k * gt["ws"]
        dec_full = jnp.broadcast_to(gt["dec"], (ROWS, ROWS))
        c_scr[h] = c_old * dec_full + _bdot(kw.T.astype(BF16), vb)
        n_scr[:, hs] = n_old * dec_full + _bdot(pre["bones"], kw.astype(BF16))
        m_rows_new = jnp.where(lane == h, jnp.broadcast_to(gt["m_new"], (ROWS, ROWS)), m_rows_new)
        hm = jax.nn.sigmoid(pcol("m_o", lo, HEAD_D)) * hout
        o_m = _silu(pcol("m_z", lo, HEAD_D)) * (_head_norm(hm) * mgn_ref[:, hs])
        o_ref[orows, 1536 + lo:1536 + lo + HEAD_D] = o_m.astype(BF16)
    m_scr[...] = m_rows_new


def _xmix_kernel(x_ref, g_ref, w_ref, wif_ref,
                 c2_ref, s2_ref, decay_ref, qdec_ref, kdec_ref,
                 wa_ref, wc_ref, cb_ref, lng_ref, lnb_ref, rgn_ref, mgn_ref, bi_ref, bf_ref,
                 o_ref, bufa_ref, sret_ref, bufc_ref, cml_ref, nml_ref, mml_ref,
                 hb_scr, exta, extc, shc, s_scr, c_scr, n_scr, m_scr,
                 *, cdec, steps_per_seq):
    s = pl.program_id(0)
    seq_step = s % steps_per_seq
    tab = (c2_ref, s2_ref, decay_ref, qdec_ref, kdec_ref)
    prm = (wa_ref, wc_ref, cb_ref, lng_ref, lnb_ref, rgn_ref, mgn_ref, bi_ref, bf_ref)
    scr = (exta, extc, shc, s_scr, c_scr, n_scr, m_scr)
    nchunk = STEP_ROWS // ROWS

    @pl.when(seq_step == 0)
    def _():
        exta[0:SUBLANES, :] = jnp.zeros((SUBLANES, BRANCH_W), F32)
        extc[0:C_TAIL, :] = jnp.zeros((C_TAIL, BRANCH_W), F32)
        s_scr[...] = jnp.zeros_like(s_scr)
        c_scr[...] = jnp.zeros_like(c_scr)
        n_scr[...] = jnp.zeros_like(n_scr)
        m_scr[...] = jnp.zeros_like(m_scr)

    for ci in range(nchunk):
        hb = hb_scr.at[ci % 2]
        hb[...] = _rmsnorm(x_ref[ci * ROWS:(ci + 1) * ROWS, :], g_ref[...]).astype(BF16)
        _mix_chunk(hb, w_ref, wif_ref, ci * ROWS, o_ref, tab, prm, scr, cdec)

    @pl.when(seq_step == steps_per_seq - 1)
    def _():
        bufa_ref[0] = exta[SUBLANES - (CONV_A_W - 1):SUBLANES, :]
        bufc_ref[0] = extc[C_PAD:C_TAIL, :]
        for h in range(HEADS):
            sret_ref[0, h] = s_scr[h]
            cml_ref[0, h] = c_scr[h]
            nml_ref[0, h:h + 1, :] = n_scr[0:1, h * HEAD_D:(h + 1) * HEAD_D]
        mml_ref[0] = m_scr[0:1, :]


def _xmix_prompt(x, gnorm, w_all, wif, layer, consts, prm, batch, seq):
    steps_per_seq = seq // STEP_ROWS
    nstep = batch * steps_per_seq
    c1 = lambda shape: _resident(shape, lambda s: (0,) * len(shape))
    rope = pl.BlockSpec((STEP_ROWS, HEAD_D), lambda s: (s % steps_per_seq, 0))
    in_specs = [
        pl.BlockSpec((STEP_ROWS, D_MODEL), lambda s: (s, 0)),
        c1((1, D_MODEL)),
        _resident((None, D_MODEL, W_BLOCK), lambda s: (layer, 0, 0)),
        c1((D_MODEL, N_IF)),
        rope, rope,
        c1((HEADS, ROWS, ROWS)),
        c1((ROWS, BRANCH_W)),
        c1((ROWS, BRANCH_W)),
        c1((CONV_A_W, BRANCH_W)),
        c1((CONV_C_W, BRANCH_W)),
    ] + [c1((1, BRANCH_W))] * 5 + [c1((1, 128))] * 2
    per_b3 = lambda s: (s // steps_per_seq, 0, 0)
    per_b4 = lambda s: (s // steps_per_seq, 0, 0, 0)
    out_specs = [
        pl.BlockSpec((STEP_ROWS, N_BRANCH * BRANCH_W), lambda s: (s, 0)),
        pl.BlockSpec((1, CONV_A_W - 1, BRANCH_W), per_b3),
        pl.BlockSpec((1, HEADS, HEAD_D, HEAD_D), per_b4),
        pl.BlockSpec((1, CONV_C_W - 1, BRANCH_W), per_b3),
        pl.BlockSpec((1, HEADS, HEAD_D, HEAD_D), per_b4),
        pl.BlockSpec((1, HEADS, HEAD_D), per_b3),
        pl.BlockSpec((1, 1, 128), per_b3),
    ]
    out_shape = [
        jax.ShapeDtypeStruct((batch * seq, N_BRANCH * BRANCH_W), BF16),
        jax.ShapeDtypeStruct((batch, CONV_A_W - 1, BRANCH_W), F32),
        jax.ShapeDtypeStruct((batch, HEADS, HEAD_D, HEAD_D), F32),
        jax.ShapeDtypeStruct((batch, CONV_C_W - 1, BRANCH_W), F32),
        jax.ShapeDtypeStruct((batch, HEADS, HEAD_D, HEAD_D), F32),
        jax.ShapeDtypeStruct((batch, HEADS, HEAD_D), F32),
        jax.ShapeDtypeStruct((batch, 1, 128), F32),
    ]
    scratch = [
        pltpu.VMEM((2, ROWS, D_MODEL), BF16),
        pltpu.VMEM((ROWS + SUBLANES, BRANCH_W), F32),
        pltpu.VMEM((ROWS + C_TAIL, BRANCH_W), F32),
        pltpu.VMEM((2, SUBLANES - 1, ROWS + C_TAIL - SUBLANES, 128), F32),
        pltpu.VMEM((HEADS, HEAD_D, HEAD_D), F32),
        pltpu.VMEM((HEADS, HEAD_D, HEAD_D), F32),
        pltpu.VMEM((ROWS, BRANCH_W), F32),
        pltpu.VMEM((ROWS, ROWS), F32),
    ]
    return pl.pallas_call(
        functools.partial(_xmix_kernel, cdec=consts["cdec"], steps_per_seq=steps_per_seq),
        grid=(nstep,),
        in_specs=in_specs, out_specs=out_specs, out_shape=out_shape, scratch_shapes=scratch,
        compiler_params=pltpu.CompilerParams(
            dimension_semantics=("arbitrary",), vmem_limit_bytes=VMEM_LIMIT),
        name="xmix_prompt",
    )(x, gnorm, w_all, wif, consts["c2"], consts["s2"], consts["decay"], consts["qdec"],
      consts["kdec"], prm["wa"], prm["wc"], prm["cb"], prm["lng"], prm["lnb"], prm["rgn"],
      prm["mgn"], prm["bi"], prm["bf"])


def _mix_sample_kernel(pa_ref, pc_ref, rq_ref, rk_ref, rv_ref, rz_ref, mq_ref, mk_ref, mv_ref,
                       mo_ref, mz_ref, pif_ref, c2_ref, s2_ref, decay_ref, qdec_ref, kdec_ref,
                       cdec_ref, wa_ref, wc_ref, cb_ref, lng_ref, lnb_ref, rgn_ref, mgn_ref,
                       bi_ref, bf_ref, bufa_ref, sret_ref, bufc_ref, cml_ref, nrows_ref, mrows_ref,
                       o_ref, bufa_o, sret_o, bufc_o, cml_o, nrows_o, mrows_o, dec_scr):
    h = pl.program_id(1)
    sq = SAMPLE_SEQS
    nt = ROWS // sq

    @pl.when(h == 0)
    def _():
        ua = pa_ref[:, 512:1024] * pa_ref[:, 1024:1536]
        ea = [bufa_ref[:, 0, :], bufa_ref[:, 1, :]] + [ua[t * sq:(t + 1) * sq, :] for t in range(nt)]
        conv_a = jnp.concatenate(
            [wa_ref[0:1, :] * ea[t] + wa_ref[1:2, :] * ea[t + 1] + wa_ref[2:3, :] * ea[t + 2]
             for t in range(nt)], axis=0)
        o_ref[:, 0:512] = (_silu(pa_ref[:, 1536:2048]) * (pa_ref[:, 0:512] * conv_a)).astype(BF16)
        bufa_o[:, 0, :] = ea[nt]
        bufa_o[:, 1, :] = ea[nt + 1]
        uc = pc_ref[:, 0:512] * jax.nn.sigmoid(pc_ref[:, 512:1024])
        nb = CONV_C_W - 1
        ec = [bufc_ref[:, i, :] for i in range(nb)] + [uc[t * sq:(t + 1) * sq, :] for t in range(nt)]
        planes = []
        for t in range(nt):
            acc = wc_ref[0:1, :] * ec[t]
            for j in range(1, CONV_C_W):
                acc = acc + wc_ref[j:j + 1, :] * ec[t + j]
            planes.append(acc)
        conv_c = jnp.concatenate(planes, axis=0)
        o_ref[:, 1024:1536] = _branch_c_post(conv_c, pc_ref[:, 1024:1536], cb_ref[...], lng_ref[...],
                                             lnb_ref[...]).astype(BF16)
        for i in range(nb):
            bufc_o[:, i, :] = ec[i + nt]

    cmask, smask = _tile_masks(True)
    rowseq = lax.broadcasted_iota(jnp.int32, (ROWS, ROWS), 0) & (sq - 1)
    laneseq = lax.broadcasted_iota(jnp.int32, (ROWS, ROWS), 1) & (sq - 1)

    c2 = c2_ref[...]
    s2 = s2_ref[...]
    rq = _rotary(rq_ref[...], c2, s2)
    rk = _rotary(rk_ref[...], c2, s2) * (HEAD_D ** -0.5)
    rqb = rq.astype(BF16)
    rkb = rk.astype(BF16)
    rvb = rv_ref[...].astype(BF16)
    rkd_t = (rk * kdec_ref[...]).T
    cdec = cdec_ref[0, 0:1, :]

    pre = _ml_prelims(pif_ref[:, 0:128], pif_ref[:, 128:256], bi_ref[...], bf_ref[...], cmask, smask)
    m_prev = mrows_ref[...]
    gt = _ml_gates(pre, 0, m_prev[:, 0:1], cmask, smask)
    mq = mq_ref[...]
    mk = mk_ref[...] * (HEAD_D ** -0.5)
    mqb = mq.astype(BF16)
    mkb = mk.astype(BF16)
    mvb = mv_ref[...].astype(BF16)
    kw = mk * gt["ws"]
    kw_t = kw.T
    dec_full = jnp.broadcast_to(gt["dec"], (ROWS, ROWS))
    dec_scr[...] = dec_full

    def per_seq(b, carry):
        inter_r, inter_m = carry
        sb = sret_ref[b, 0]
        inter_r = jnp.where(rowseq == b, _bdot(rqb, sb.astype(BF16)), inter_r)
        k_sel = jnp.where(laneseq == b, rkd_t, 0.0).astype(BF16)
        sret_o[b, 0] = sb * cdec + _bdot(k_sel, rvb)
        cb_ = cml_ref[b, 0]
        inter_m = jnp.where(rowseq == b, _bdot(mqb, cb_.astype(BF16)), inter_m)
        kw_sel = jnp.where(laneseq == b, kw_t, 0.0).astype(BF16)
        cml_o[b, 0] = cb_ * dec_scr[pl.ds(b, 1), :] + _bdot(kw_sel, mvb)
        return inter_r, inter_m

    zero = jnp.zeros((ROWS, HEAD_D), F32)
    inter_r, inter_m = lax.fori_loop(0, sq, per_seq, (zero, zero), unroll=16)

    sc = _bdot_nt(rqb, rkb) * decay_ref[0]
    ret = _bdot(sc.astype(BF16), rvb) + inter_r * qdec_ref[...]
    o_r = (_silu(rz_ref[...]) * (_head_norm(ret) * rgn_ref[...])).astype(BF16)

    n_old = nrows_ref[...]
    qn = jnp.sum(mq * n_old, axis=1, keepdims=True)
    hout = _ml_output(gt, mqb, mkb, mvb, inter_m, qn)
    nrows_o[...] = n_old * dec_full + _bdot(pre["bones"], kw.astype(BF16))
    mrows_o[...] = jnp.broadcast_to(gt["m_new"], (ROWS, ROWS))
    hm = jax.nn.sigmoid(mo_ref[...]) * hout
    o_m = (_silu(mz_ref[...]) * (_head_norm(hm) * mgn_ref[...])).astype(BF16)

    for hh in range(HEADS):
        @pl.when(h == hh)
        def _(hh=hh):
            o_ref[:, 512 + hh * HEAD_D:512 + (hh + 1) * HEAD_D] = o_r
            o_ref[:, 1536 + hh * HEAD_D:1536 + (hh + 1) * HEAD_D] = o_m


def _mix_sample(p, pif, consts, prm, st, nseq, layer, depth, stacked):
    ntile = nseq // SAMPLE_SEQS
    sq = SAMPLE_SEQS
    head_cols = lambda name: pl.BlockSpec(
        (ROWS, HEAD_D), lambda i, h, _o=_COL[name] // HEAD_D: (i, _o + h))
    const2 = lambda i, h: (0, 0)
    per_head2 = lambda i, h: (0, h)
    conv_a_state = pl.BlockSpec((None, sq, CONV_A_W - 1, BRANCH_W), lambda i, h: (layer, i, 0, 0))
    conv_c_state = pl.BlockSpec((None, sq, CONV_C_W - 1, BRANCH_W), lambda i, h: (layer, i, 0, 0))
    matrix_state = pl.BlockSpec((None, sq, 1, HEAD_D, HEAD_D), lambda i, h: (layer, i, h, 0, 0))
    in_specs = [
        pl.BlockSpec((ROWS, 2048), lambda i, h: (i, 0)),
        pl.BlockSpec((ROWS, 2048), lambda i, h: (i, 2)),
    ] + [head_cols(n) for n in ("r_q", "r_k", "r_v", "r_z", "m_q", "m_k", "m_v", "m_o", "m_z")] + [
        pl.BlockSpec((ROWS, 256), lambda i, h: (i, h)),
        pl.BlockSpec((ROWS, HEAD_D), const2),
        pl.BlockSpec((ROWS, HEAD_D), const2),
        pl.BlockSpec((1, ROWS, ROWS), lambda i, h: (h, 0, 0)),
        pl.BlockSpec((ROWS, HEAD_D), per_head2),
        pl.BlockSpec((ROWS, HEAD_D), per_head2),
        pl.BlockSpec((1, 8, 128), lambda i, h: (h, 0, 0)),
        pl.BlockSpec((CONV_A_W, BRANCH_W), const2),
        pl.BlockSpec((CONV_C_W, BRANCH_W), const2),
        pl.BlockSpec((1, BRANCH_W), const2),
        pl.BlockSpec((1, BRANCH_W), const2),
        pl.BlockSpec((1, BRANCH_W), const2),
        pl.BlockSpec((1, HEAD_D), per_head2),
        pl.BlockSpec((1, HEAD_D), per_head2),
        pl.BlockSpec((1, 128), per_head2),
        pl.BlockSpec((1, 128), per_head2),
        conv_a_state, matrix_state, conv_c_state, matrix_state,
        pl.BlockSpec((ROWS, HEAD_D), lambda i, h: (i, h)),
        pl.BlockSpec((ROWS, HEAD_D), lambda i, h: (i, h)),
    ]
    out_specs = [
        pl.BlockSpec((ROWS, N_BRANCH * BRANCH_W), lambda i, h: (i, 0)),
        conv_a_state, matrix_state, conv_c_state, matrix_state,
        pl.BlockSpec((ROWS, HEAD_D), lambda i, h: (i, h)),
        pl.BlockSpec((ROWS, HEAD_D), lambda i, h: (i, h)),
    ]
    nrow = ntile * ROWS
    out_shape = [
        jax.ShapeDtypeStruct((nrow, N_BRANCH * BRANCH_W), BF16),
        jax.ShapeDtypeStruct((depth, nseq, CONV_A_W - 1, BRANCH_W), F32),
        jax.ShapeDtypeStruct((depth, nseq, HEADS, HEAD_D, HEAD_D), F32),
        jax.ShapeDtypeStruct((depth, nseq, CONV_C_W - 1, BRANCH_W), F32),
        jax.ShapeDtypeStruct((depth, nseq, HEADS, HEAD_D, HEAD_D), F32),
        jax.ShapeDtypeStruct((nrow, HEADS * HEAD_D), F32),
        jax.ShapeDtypeStruct((nrow, HEADS * HEAD_D), F32),
    ]
    args = [p] * 11 + [
        pif, consts["c2"], consts["s2"], consts["decay"], consts["qdec"], consts["kdec"],
        consts["cdec"], prm["wa"], prm["wc"], prm["cb"], prm["lng"], prm["lnb"], prm["rgn"],
        prm["mgn"], prm["bi_h"], prm["bf_h"],
        st["bufa"], st["sret"], st["bufc"], st["cml"], st["nrows"], st["mrows"]]
    assert len(args) == len(in_specs)
    n_in = len(args)
    aliases = {}
    if stacked is not None:
        aliases = {n_in + k: 1 + k for k in range(4)}
        in_specs = in_specs + [pl.BlockSpec(memory_space=pl.ANY)] * 4
        args = args + list(stacked)

    def entry(*refs):
        _mix_sample_kernel(*refs[:n_in], *refs[n_in + len(aliases):])

    return pl.pallas_call(
        entry,
        grid=(ntile, HEADS),
        in_specs=in_specs, out_specs=out_specs, out_shape=out_shape,
        scratch_shapes=[pltpu.VMEM((ROWS, ROWS), F32)],
        input_output_aliases=aliases,
        compiler_params=pltpu.CompilerParams(
            dimension_semantics=("arbitrary", "arbitrary"), vmem_limit_bytes=VMEM_LIMIT),
        name="mix_sample",
    )(*args)


def _rope_tables(pos):
    inv = ROPE_BASE ** (-jnp.arange(0, HEAD_D, 2, dtype=F32) / HEAD_D)
    ang = pos.astype(F32)[:, None] * inv[None, :]
    cos = jnp.cos(ang)
    sin = jnp.sin(ang)
    return jnp.concatenate([cos, cos], axis=1), jnp.concatenate([-sin, sin], axis=1)


def _decay_tables(t_of_row, seq_of_row, chunk_len):
    log_g = np.log1p(-(2.0 ** (-5.0 - np.arange(HEADS, dtype=np.float64))))
    dt = t_of_row[:, None] - t_of_row[None, :]
    ok = (seq_of_row[:, None] == seq_of_row[None, :]) & (dt >= 0)
    decay = np.where(ok[None], np.exp(np.maximum(dt, 0)[None] * log_g[:, None, None]), 0.0)
    qdec = np.exp((t_of_row[:, None] + 1.0) * log_g[None, :])
    kdec = np.exp((chunk_len - 1.0 - t_of_row)[:, None] * log_g[None, :])
    cdec = np.exp(chunk_len * log_g)
    rep = lambda a: jnp.asarray(np.repeat(a, HEAD_D, axis=1), F32)
    return jnp.asarray(decay, F32), rep(qdec), rep(kdec), cdec


def kernel(x_prompt, x_sample, state_conv_a, state_ret, state_conv_c, state_mlstm_C,
           state_mlstm_n, state_mlstm_m, norm_g, w_in, conv_a_w, conv_c_w, conv_c_b,
           ln_c_g, ln_c_b, ret_gn_g, ml_gn_g, ml_gate_b, w_br, w_out, final_g):
    bp, tp, _ = x_prompt.shape
    bs, ts, _ = x_sample.shape
    depth = w_in.shape[0]
    sq = SAMPLE_SEQS
    ntile = bs // sq
    assert tp % STEP_ROWS == 0 and ts * sq == ROWS and bs % sq == 0

    r = np.arange(ROWS)
    c2p, s2p = _rope_tables(jnp.arange(tp, dtype=jnp.int32))
    decay_p, qdec_p, kdec_p, cdec_p = _decay_tables(r.astype(np.float64), np.zeros(ROWS), float(ROWS))
    consts_p = dict(c2=c2p, s2=s2p, decay=decay_p, qdec=qdec_p, kdec=kdec_p,
                    cdec=tuple(float(v) for v in cdec_p))
    t_s = (r // sq).astype(np.float64)
    c2s, s2s = _rope_tables(PAST_LEN + jnp.asarray(r // sq, jnp.int32))
    decay_s, qdec_s, kdec_s, cdec_s = _decay_tables(t_s, r % sq, float(ts))
    cdec_s_arr = jnp.asarray(np.broadcast_to(cdec_s[:, None, None], (HEADS, 8, 128)), F32)
    consts_s = dict(c2=c2s, s2=s2s, decay=decay_s, qdec=qdec_s, kdec=kdec_s, cdec=cdec_s_arr)

    xp = x_prompt.reshape(bp * tp, D_MODEL)
    xs = x_sample.reshape(ntile, sq, ts, D_MODEL).transpose(0, 2, 1, 3).reshape(bs * ts, D_MODEL)

    fg = final_g.reshape(1, D_MODEL)
    outs_p = [[] for _ in range(6)]
    n_s, m_s = [], []
    stacked = None
    sel_p = np.zeros((2 * HEADS, N_IF), np.float32)
    sel_s = np.zeros((2 * HEADS, HEADS * N_IF), np.float32)
    for h in range(HEADS):
        sel_p[h, h] = 1.0
        sel_p[HEADS + h, 128 + h] = 1.0
        sel_s[h, h * N_IF] = 1.0
        sel_s[HEADS + h, h * N_IF + 128] = 1.0
    w_bf = w_in.astype(BF16)
    pad_row = lambda v: jnp.pad(v, (0, 128 - v.shape[0])).reshape(1, 128)
    for l in range(depth):
        w_if = w_in[l, :, N_MAIN:N_MAIN + 2 * HEADS]
        spread = lambda sel: jnp.dot(w_if, jnp.asarray(sel), precision=lax.Precision.HIGHEST).astype(BF16)
        w_gate = w_bf[l, :, N_MAIN + 2 * HEADS:]
        gb = ml_gate_b[l]
        prm = dict(
            wa=conv_a_w[l], wc=conv_c_w[l], cb=conv_c_b[l].reshape(1, -1),
            lng=ln_c_g[l].reshape(1, -1), lnb=ln_c_b[l].reshape(1, -1),
            rgn=ret_gn_g[l].reshape(1, -1), mgn=ml_gn_g[l].reshape(1, -1),
            bi=pad_row(gb[:HEADS]), bf=pad_row(gb[HEADS:]),
            bi_h=jnp.concatenate([pad_row(gb[h:h + 1]) for h in range(HEADS)], axis=1),
            bf_h=jnp.concatenate([pad_row(gb[HEADS + h:HEADS + h + 1]) for h in range(HEADS)], axis=1),
        )
        gnorm = norm_g[l].reshape(1, D_MODEL)
        wbr = w_br[l].astype(BF16)
        wout = w_out[l].astype(BF16)
        final = l == depth - 1

        res = _xmix_prompt(xp, gnorm, w_bf, spread(sel_p), l, consts_p, prm, bp, tp)
        for k in range(5):
            outs_p[k].append(res[1 + k])
        outs_p[5].append(res[6][:, 0, :HEADS])
        xp = _merge(xp, gnorm, res[0], w_gate, wbr, wout, fg, final, tm=512)

        ps, pifs = _inproj(xs, gnorm, w_bf, spread(sel_s), l, tm=bs * ts, tn=2048)
        n_rows = jnp.broadcast_to(
            state_mlstm_n[l].reshape(ntile, 1, sq, HEADS * HEAD_D),
            (ntile, ts, sq, HEADS * HEAD_D)).reshape(bs * ts, HEADS * HEAD_D)
        m_rows = jnp.broadcast_to(
            state_mlstm_m[l].reshape(ntile, 1, sq, HEADS, 1),
            (ntile, ts, sq, HEADS, HEAD_D)).reshape(bs * ts, HEADS * HEAD_D)
        st = dict(bufa=state_conv_a, sret=state_ret, bufc=state_conv_c, cml=state_mlstm_C,
                  nrows=n_rows, mrows=m_rows)
        res = _mix_sample(ps, pifs, consts_s, prm, st, bs, l, depth, stacked)
        stacked = tuple(res[1:5])
        n_s.append(res[5].reshape(ntile, ts, sq, HEADS, HEAD_D)[:, 0].reshape(bs, HEADS, HEAD_D))
        m_s.append(res[6].reshape(ntile, ts, sq, HEADS, HEAD_D)[:, 0, :, :, 0].reshape(bs, HEADS))
        xs = _merge(xs, gnorm, res[0], w_gate, wbr, wout, fg, final, tm=bs * ts)

    y_prompt = xp.reshape(bp, tp, D_MODEL)
    y_sample = xs.reshape(ntile, ts, sq, D_MODEL).transpose(0, 2, 1, 3).reshape(bs, ts, D_MODEL)
    sp = [jnp.stack(a, axis=0) for a in outs_p]
    return (y_prompt, y_sample, sp[0], stacked[0], sp[1], stacked[1], sp[2], stacked[2],
            sp[3], stacked[3], sp[4], jnp.stack(n_s, axis=0), sp[5], jnp.stack(m_s, axis=0))
```

```python
import functools

import numpy as np
import jax
import jax.numpy as jnp
from jax import lax
from jax.experimental import pallas as pl
from jax.experimental.pallas import tpu as pltpu

F32 = jnp.float32
BF16 = jnp.bfloat16

D_MODEL = 1024
BRANCH_W = 512
N_BRANCH = 4
HEADS = 4
HEAD_D = 128
CONV_A_W = 3
CONV_C_W = 31
ROPE_BASE = 10000.0
EPS = 1e-6
N_MAIN = 16 * BRANCH_W
N_IF = 256
N_GATE = N_BRANCH * D_MODEL
ROWS = 128
STEP_ROWS = 4 * ROWS
W_BLOCK = N_MAIN + 512
SAMPLE_SEQS = 32
SUBLANES = 8
PAST_LEN = 16384
C_TAIL = -(-(CONV_C_W - 1) // SUBLANES) * SUBLANES
C_PAD = C_TAIL - (CONV_C_W - 1)
VMEM_LIMIT = 56 * 1024 * 1024

_COL = dict(a_b=0, a_c=512, a_u=1024, a_z=1536, r_q=2048, r_k=2560, r_v=3072, r_z=3584,
            glu_v=4096, glu_g=4608, c_z=5120, m_q=5632, m_k=6144, m_v=6656, m_o=7168, m_z=7680)


def _bdot(a, b):
    return jnp.dot(a, b, preferred_element_type=F32)


def _bdot_nt(a, b):
    return lax.dot_general(a, b, (((1,), (1,)), ((), ())), preferred_element_type=F32)


def _exact_mm(mat_b16, x):
    hi = x.astype(BF16)
    r1 = x - hi.astype(F32)
    mid = r1.astype(BF16)
    lo = (r1 - mid.astype(F32)).astype(BF16)
    return _bdot(mat_b16, hi) + _bdot(mat_b16, mid) + _bdot(mat_b16, lo)


def _sigmoid(x):
    return 0.5 * jnp.tanh(0.5 * x) + 0.5


def _silu(x):
    return x * _sigmoid(x)


def _log_sigmoid(x):
    return jnp.minimum(x, 0.0) - jnp.log1p(jnp.exp(-jnp.abs(x)))


def _head_norm(y):
    mu = jnp.mean(y, axis=-1, keepdims=True)
    yc = y - mu
    var = jnp.mean(yc * yc, axis=-1, keepdims=True)
    return yc * lax.rsqrt(var + EPS)


def _rotary(x, c2, s2):
    return x * c2 + pltpu.roll(x, HEAD_D // 2, 1) * s2


def _rmsnorm(x, g):
    ms = jnp.mean(x * x, axis=-1, keepdims=True)
    return (x * lax.rsqrt(ms + EPS)) * g


def _resident(shape, imap):
    return pl.BlockSpec(shape, imap, pipeline_mode=pl.Buffered(1))


def _inproj_kernel(x_ref, g_ref, w_ref, wif_ref, p_ref, pif_ref, hb_ref):
    @pl.when(pl.program_id(1) == 0)
    def _():
        hb = _rmsnorm(x_ref[...], g_ref[...]).astype(BF16)
        hb_ref[...] = hb
        pif_ref[...] = _bdot(hb, wif_ref[...])

    p_ref[...] = _bdot(hb_ref[...], w_ref[...])


def _inproj(x, g, w_all, wif, layer, tm, tn):
    m = x.shape[0]
    nif = wif.shape[1]
    return pl.pallas_call(
        _inproj_kernel,
        grid=(m // tm, N_MAIN // tn),
        in_specs=[
            pl.BlockSpec((tm, D_MODEL), lambda i, j: (i, 0)),
            pl.BlockSpec((1, D_MODEL), lambda i, j: (0, 0)),
            pl.BlockSpec((None, D_MODEL, tn), lambda i, j: (layer, 0, j)),
            pl.BlockSpec((D_MODEL, nif), lambda i, j: (0, 0)),
        ],
        out_specs=[
            pl.BlockSpec((tm, tn), lambda i, j: (i, j)),
            pl.BlockSpec((tm, nif), lambda i, j: (i, 0)),
        ],
        out_shape=[jax.ShapeDtypeStruct((m, N_MAIN), F32), jax.ShapeDtypeStruct((m, nif), F32)],
        scratch_shapes=[pltpu.VMEM((tm, D_MODEL), BF16)],
        compiler_params=pltpu.CompilerParams(
            dimension_semantics=("arbitrary", "arbitrary"), vmem_limit_bytes=VMEM_LIMIT),
        name="inproj",
    )(x, g, w_all, wif)


def _merge_kernel(x_ref, g_ref, o_ref, wg_ref, wbr_ref, wout_ref, fg_ref, y_ref, hb_ref, *, final):
    x = x_ref[...]
    hb_ref[...] = _rmsnorm(x, g_ref[...]).astype(BF16)
    acc = None
    for n in range(N_BRANCH):
        proj = _bdot(o_ref[:, n * BRANCH_W:(n + 1) * BRANCH_W], wbr_ref[n])
        gate = _sigmoid(_bdot(hb_ref[...], wg_ref[:, n * D_MODEL:(n + 1) * D_MODEL]))
        term = gate * proj
        acc = term if acc is None else acc + term
    y = x + _bdot(acc.astype(BF16), wout_ref[...])
    if final:
        y = _rmsnorm(y, fg_ref[...])
    y_ref[...] = y


def _merge(x, g, o, wg, wbr, wout, fg, final, tm):
    m = x.shape[0]
    resident = lambda shape: _resident(shape, lambda i: (0,) * len(shape))
    return pl.pallas_call(
        functools.partial(_merge_kernel, final=final),
        grid=(m // tm,),
        in_specs=[
            pl.BlockSpec((tm, D_MODEL), lambda i: (i, 0)),
            resident((1, D_MODEL)),
            pl.BlockSpec((tm, N_BRANCH * BRANCH_W), lambda i: (i, 0)),
            resident((D_MODEL, N_GATE)),
            resident((N_BRANCH, BRANCH_W, D_MODEL)),
            resident((D_MODEL, D_MODEL)),
            resident((1, D_MODEL)),
        ],
        out_specs=pl.BlockSpec((tm, D_MODEL), lambda i: (i, 0)),
        out_shape=jax.ShapeDtypeStruct((m, D_MODEL), F32),
        scratch_shapes=[pltpu.VMEM((tm, D_MODEL), BF16)],
        compiler_params=pltpu.CompilerParams(
            dimension_semantics=("arbitrary",), vmem_limit_bytes=VMEM_LIMIT),
        name="merge",
    )(x, g, o, wg, wbr, wout, fg)


def _tile_masks(sample):
    row = lax.broadcasted_iota(jnp.int32, (ROWS, ROWS), 0)
    col = lax.broadcasted_iota(jnp.int32, (ROWS, ROWS), 1)
    if not sample:
        return row >= col, None
    same = (row & (SAMPLE_SEQS - 1)) == (col & (SAMPLE_SEQS - 1))
    return jnp.logical_and(same, row >= col), same


def _ml_prelims(pi, pf, bi, bfo, cmask, smask):
    ai = pi + bi
    lf = _log_sigmoid(pf + bfo)
    tri = jnp.where(cmask, 1.0, 0.0).astype(BF16)
    if smask is None:
        bones = jnp.ones((ROWS, ROWS), BF16)
    else:
        bones = jnp.where(smask, 1.0, 0.0).astype(BF16)
    fcum = _exact_mm(tri, lf)
    ftot = _exact_mm(bones, lf)
    g = (ftot - fcum) + ai
    return dict(ai_t=ai.T, fcum=fcum, fcum_t=fcum.T, ftot=ftot, g=g, g_t=g.T, bones=bones)


def _ml_gates(pre, col, m_prev_c, cmask, smask):
    f_c = pre["fcum"][:, col:col + 1]
    f_r = pre["fcum_t"][col:col + 1, :]
    ig_r = pre["ai_t"][col:col + 1, :]
    g_c = pre["g"][:, col:col + 1]
    g_r = pre["g_t"][col:col + 1, :]
    ftot_c = pre["ftot"][:, col:col + 1]
    logw = jnp.where(cmask, (f_c - f_r) + ig_r, -jnp.inf)
    b = f_c + m_prev_c
    m_t = jnp.maximum(jnp.max(logw, axis=1, keepdims=True), b)
    wts = jnp.exp(logw - m_t)
    inter = jnp.exp(b - m_t)
    if smask is None:
        gmax = jnp.max(g_r, axis=1, keepdims=True)
    else:
        gmax = jnp.max(jnp.where(smask, g_r, -jnp.inf), axis=1, keepdims=True)
    m_new = jnp.maximum(ftot_c + m_prev_c, gmax)
    ws = jnp.exp(g_c - m_new)
    dec = jnp.exp((ftot_c + m_prev_c) - m_new)
    return dict(m_t=m_t, wts=wts, inter=inter, m_new=m_new, ws=ws, dec=dec)


def _ml_output(gt, qb, kb, vb, inter_term, qn):
    s = _bdot_nt(qb, kb) * gt["wts"]
    num = _bdot(s.astype(BF16), vb) + inter_term * gt["inter"]
    den = jnp.sum(s, axis=1, keepdims=True) + qn * gt["inter"]
    return num / jnp.maximum(jnp.abs(den), jnp.exp(-gt["m_t"]))


def _branch_c_post(conv, c_z, cb, lng, lnb):
    y = conv + cb
    mu = jnp.mean(y, axis=-1, keepdims=True)
    yc = y - mu
    var = jnp.mean(yc * yc, axis=-1, keepdims=True)
    ln = (yc * lax.rsqrt(var + EPS)) * lng + lnb
    return _silu(c_z) * _silu(ln)


_GROUPS = ((0, 2048), (2048, 4096), (4096, 5632), (5632, N_MAIN))


def _mix_chunk(hb_ref, w_ref, wif_ref, or0, o_ref, tab, prm, scr, cdec):
    c2_ref, s2_ref, decay_ref, qdec_ref, kdec_ref = tab
    wa_ref, wc_ref, cb_ref, lng_ref, lnb_ref, rgn_ref, mgn_ref, bi_ref, bf_ref = prm
    exta, extc, shc, s_scr, c_scr, n_scr, m_scr = scr
    orows = slice(or0, or0 + ROWS)
    projected = {}

    def pcol(name, lo=0, w=BRANCH_W):
        c0 = _COL[name]
        g0, g1 = next(g for g in _GROUPS if g[0] <= c0 < g[1])
        if g0 not in projected:
            projected[g0] = _bdot(hb_ref[...], w_ref[:, g0:g1])
        return projected[g0][:, c0 - g0 + lo:c0 - g0 + lo + w]

    ua = pcol("a_c") * pcol("a_u")
    exta[SUBLANES:SUBLANES + ROWS, :] = ua
    conv_a = (wa_ref[0:1, :] * exta[SUBLANES - 2:SUBLANES - 2 + ROWS, :]
              + wa_ref[1:2, :] * exta[SUBLANES - 1:SUBLANES - 1 + ROWS, :] + wa_ref[2:3, :] * ua)
    o_a = _silu(pcol("a_z")) * (pcol("a_b") * conv_a)
    o_ref[orows, 0:512] = o_a.astype(BF16)
    exta[0:SUBLANES, :] = exta[ROWS:ROWS + SUBLANES, :]

    extc[C_TAIL:C_TAIL + ROWS, :] = pcol("glu_v") * _sigmoid(pcol("glu_g"))
    span = ROWS + C_TAIL - SUBLANES
    conv_blocks = []
    for lb in range(BRANCH_W // 128):
        lanes = slice(lb * 128, (lb + 1) * 128)
        sh = shc.at[lb % 2]
        for ph in range(1, SUBLANES):
            sh[ph - 1] = extc[ph:ph + span, lanes]
        acc = None
        for ph in range(SUBLANES):
            for a in range(span // SUBLANES - ROWS // SUBLANES + 1):
                j = a * SUBLANES + ph - C_PAD
                if 0 <= j < CONV_C_W:
                    arows = slice(a * SUBLANES, a * SUBLANES + ROWS)
                    shifted = extc[arows, lanes] if ph == 0 else sh[ph - 1, arows, :]
                    term = wc_ref[j:j + 1, lanes] * shifted
                    acc = term if acc is None else acc + term
        conv_blocks.append(acc + wc_ref[CONV_C_W - 1:CONV_C_W, lanes] * extc[C_TAIL:C_TAIL + ROWS, lanes])
    conv_c = jnp.concatenate(conv_blocks, axis=1)
    o_c = _branch_c_post(conv_c, pcol("c_z"), cb_ref[...], lng_ref[...], lnb_ref[...])
    o_ref[orows, 1024:1536] = o_c.astype(BF16)
    extc[0:C_TAIL, :] = extc[ROWS:ROWS + C_TAIL, :]

    c2 = c2_ref[orows, :]
    s2 = s2_ref[orows, :]
    for h in range(HEADS):
        lo = h * HEAD_D
        hs = slice(lo, lo + HEAD_D)
        q = _rotary(pcol("r_q", lo, HEAD_D), c2, s2)
        k = _rotary(pcol("r_k", lo, HEAD_D), c2, s2) * (HEAD_D ** -0.5)
        qb = q.astype(BF16)
        kb = k.astype(BF16)
        vb = pcol("r_v", lo, HEAD_D).astype(BF16)
        s_old = s_scr[h]
        sc = _bdot_nt(qb, kb) * decay_ref[h]
        ret = _bdot(sc.astype(BF16), vb) + _bdot(qb, s_old.astype(BF16)) * qdec_ref[:, hs]
        s_scr[h] = s_old * cdec[h] + _bdot((k * kdec_ref[:, hs]).T.astype(BF16), vb)
        o_r = _silu(pcol("r_z", lo, HEAD_D)) * (_head_norm(ret) * rgn_ref[:, hs])
        o_ref[orows, 512 + lo:512 + lo + HEAD_D] = o_r.astype(BF16)

    cmask, smask = _tile_masks(False)
    pif = _bdot(hb_ref[...], wif_ref[...])
    pre = _ml_prelims(pif[:, 0:128], pif[:, 128:256], bi_ref[...], bf_ref[...], cmask, smask)
    lane = lax.broadcasted_iota(jnp.int32, (ROWS, ROWS), 1)
    m_rows = m_scr[...]
    m_rows_new = jnp.zeros((ROWS, ROWS), F32)
    for h in range(HEADS):
        lo = h * HEAD_D
        hs = slice(lo, lo + HEAD_D)
        gt = _ml_gates(pre, h, m_rows[:, h:h + 1], cmask, smask)
        q = pcol("m_q", lo, HEAD_D)
        k = pcol("m_k", lo, HEAD_D) * (HEAD_D ** -0.5)
        qb = q.astype(BF16)
        kb = k.astype(BF16)
        vb = pcol("m_v", lo, HEAD_D).astype(BF16)
        c_old = c_scr[h]
        n_old = n_scr[:, hs]
        qn = jnp.sum(q * n_old, axis=1, keepdims=True)
        hout = _ml_output(gt, qb, kb, vb, _bdot(qb, c_old.astype(BF16)), qn)
        kw = k * gt["ws"]
        dec_full = jnp.broadcast_to(gt["dec"], (ROWS, ROWS))
        c_scr[h] = c_old * dec_full + _bdot(kw.T.astype(BF16), vb)
        n_scr[:, hs] = n_old * dec_full + _bdot(pre["bones"], kw.astype(BF16))
        m_rows_new = jnp.where(lane == h, jnp.broadcast_to(gt["m_new"], (ROWS, ROWS)), m_rows_new)
        hm = _sigmoid(pcol("m_o", lo, HEAD_D)) * hout
        o_m = _silu(pcol("m_z", lo, HEAD_D)) * (_head_norm(hm) * mgn_ref[:, hs])
        o_ref[orows, 1536 + lo:1536 + lo + HEAD_D] = o_m.astype(BF16)
    m_scr[...] = m_rows_new


def _xmix_kernel(x_ref, g_ref, w_ref, wif_ref,
                 c2_ref, s2_ref, decay_ref, qdec_ref, kdec_ref,
                 wa_ref, wc_ref, cb_ref, lng_ref, lnb_ref, rgn_ref, mgn_ref, bi_ref, bf_ref,
                 o_ref, bufa_ref, sret_ref, bufc_ref, cml_ref, nml_ref, mml_ref,
                 hb_scr, exta, extc, shc, s_scr, c_scr, n_scr, m_scr,
                 *, cdec, steps_per_seq):
    s = pl.program_id(0)
    seq_step = s % steps_per_seq
    tab = (c2_ref, s2_ref, decay_ref, qdec_ref, kdec_ref)
    prm = (wa_ref, wc_ref, cb_ref, lng_ref, lnb_ref, rgn_ref, mgn_ref, bi_ref, bf_ref)
    scr = (exta, extc, shc, s_scr, c_scr, n_scr, m_scr)
    nchunk = STEP_ROWS // ROWS

    @pl.when(seq_step == 0)
    def _():
        exta[0:SUBLANES, :] = jnp.zeros((SUBLANES, BRANCH_W), F32)
        extc[0:C_TAIL, :] = jnp.zeros((C_TAIL, BRANCH_W), F32)
        s_scr[...] = jnp.zeros_like(s_scr)
        c_scr[...] = jnp.zeros_like(c_scr)
        n_scr[...] = jnp.zeros_like(n_scr)
        m_scr[...] = jnp.zeros_like(m_scr)

    for ci in range(nchunk):
        hb = hb_scr.at[ci % 2]
        hb[...] = _rmsnorm(x_ref[ci * ROWS:(ci + 1) * ROWS, :], g_ref[...]).astype(BF16)
        _mix_chunk(hb, w_ref, wif_ref, ci * ROWS, o_ref, tab, prm, scr, cdec)

    @pl.when(seq_step == steps_per_seq - 1)
    def _():
        bufa_ref[0] = exta[SUBLANES - (CONV_A_W - 1):SUBLANES, :]
        bufc_ref[0] = extc[C_PAD:C_TAIL, :]
        for h in range(HEADS):
            sret_ref[0, h] = s_scr[h]
            cml_ref[0, h] = c_scr[h]
            nml_ref[0, h:h + 1, :] = n_scr[0:1, h * HEAD_D:(h + 1) * HEAD_D]
        mml_ref[0] = m_scr[0:1, :]


def _xmix_prompt(x, gnorm, w_all, wif, layer, consts, prm, batch, seq):
    steps_per_seq = seq // STEP_ROWS
    nstep = batch * steps_per_seq
    c1 = lambda shape: _resident(shape, lambda s: (0,) * len(shape))
    rope = pl.BlockSpec((STEP_ROWS, HEAD_D), lambda s: (s % steps_per_seq, 0))
    in_specs = [
        pl.BlockSpec((STEP_ROWS, D_MODEL), lambda s: (s, 0)),
        c1((1, D_MODEL)),
        _resident((None, D_MODEL, W_BLOCK), lambda s: (layer, 0, 0)),
        c1((D_MODEL, N_IF)),
        rope, rope,
        c1((HEADS, ROWS, ROWS)),
        c1((ROWS, BRANCH_W)),
        c1((ROWS, BRANCH_W)),
        c1((CONV_A_W, BRANCH_W)),
        c1((CONV_C_W, BRANCH_W)),
    ] + [c1((1, BRANCH_W))] * 5 + [c1((1, 128))] * 2
    per_b3 = lambda s: (s // steps_per_seq, 0, 0)
    per_b4 = lambda s: (s // steps_per_seq, 0, 0, 0)
    out_specs = [
        pl.BlockSpec((STEP_ROWS, N_BRANCH * BRANCH_W), lambda s: (s, 0)),
        pl.BlockSpec((1, CONV_A_W - 1, BRANCH_W), per_b3),
        pl.BlockSpec((1, HEADS, HEAD_D, HEAD_D), per_b4),
        pl.BlockSpec((1, CONV_C_W - 1, BRANCH_W), per_b3),
        pl.BlockSpec((1, HEADS, HEAD_D, HEAD_D), per_b4),
        pl.BlockSpec((1, HEADS, HEAD_D), per_b3),
        pl.BlockSpec((1, 1, 128), per_b3),
    ]
    out_shape = [
        jax.ShapeDtypeStruct((batch * seq, N_BRANCH * BRANCH_W), BF16),
        jax.ShapeDtypeStruct((batch, CONV_A_W - 1, BRANCH_W), F32),
        jax.ShapeDtypeStruct((batch, HEADS, HEAD_D, HEAD_D), F32),
        jax.ShapeDtypeStruct((batch, CONV_C_W - 1, BRANCH_W), F32),
        jax.ShapeDtypeStruct((batch, HEADS, HEAD_D, HEAD_D), F32),
        jax.ShapeDtypeStruct((batch, HEADS, HEAD_D), F32),
        jax.ShapeDtypeStruct((batch, 1, 128), F32),
    ]
    scratch = [
        pltpu.VMEM((2, ROWS, D_MODEL), BF16),
        pltpu.VMEM((ROWS + SUBLANES, BRANCH_W), F32),
        pltpu.VMEM((ROWS + C_TAIL, BRANCH_W), F32),
        pltpu.VMEM((2, SUBLANES - 1, ROWS + C_TAIL - SUBLANES, 128), F32),
        pltpu.VMEM((HEADS, HEAD_D, HEAD_D), F32),
        pltpu.VMEM((HEADS, HEAD_D, HEAD_D), F32),
        pltpu.VMEM((ROWS, BRANCH_W), F32),
        pltpu.VMEM((ROWS, ROWS), F32),
    ]
    return pl.pallas_call(
        functools.partial(_xmix_kernel, cdec=consts["cdec"], steps_per_seq=steps_per_seq),
        grid=(nstep,),
        in_specs=in_specs, out_specs=out_specs, out_shape=out_shape, scratch_shapes=scratch,
        compiler_params=pltpu.CompilerParams(
            dimension_semantics=("arbitrary",), vmem_limit_bytes=VMEM_LIMIT),
        name="xmix_prompt",
    )(x, gnorm, w_all, wif, consts["c2"], consts["s2"], consts["decay"], consts["qdec"],
      consts["kdec"], prm["wa"], prm["wc"], prm["cb"], prm["lng"], prm["lnb"], prm["rgn"],
      prm["mgn"], prm["bi"], prm["bf"])


def _mix_sample_kernel(pa_ref, pc_ref, rq_ref, rk_ref, rv_ref, rz_ref, mq_ref, mk_ref, mv_ref,
                       mo_ref, mz_ref, pif_ref, c2_ref, s2_ref, decay_ref, qdec_ref, kdec_ref,
                       cdec_ref, wa_ref, wc_ref, cb_ref, lng_ref, lnb_ref, rgn_ref, mgn_ref,
                       bi_ref, bf_ref, bufa_ref, sret_ref, bufc_ref, cml_ref, nrows_ref, mrows_ref,
                       o_ref, bufa_o, sret_o, bufc_o, cml_o, nrows_o, mrows_o, dec_scr):
    h = pl.program_id(1)
    sq = SAMPLE_SEQS
    nt = ROWS // sq

    @pl.when(h == 0)
    def _():
        ua = pa_ref[:, 512:1024] * pa_ref[:, 1024:1536]
        ea = [bufa_ref[:, 0, :], bufa_ref[:, 1, :]] + [ua[t * sq:(t + 1) * sq, :] for t in range(nt)]
        conv_a = jnp.concatenate(
            [wa_ref[0:1, :] * ea[t] + wa_ref[1:2, :] * ea[t + 1] + wa_ref[2:3, :] * ea[t + 2]
             for t in range(nt)], axis=0)
        o_ref[:, 0:512] = (_silu(pa_ref[:, 1536:2048]) * (pa_ref[:, 0:512] * conv_a)).astype(BF16)
        bufa_o[:, 0, :] = ea[nt]
        bufa_o[:, 1, :] = ea[nt + 1]
        uc = pc_ref[:, 0:512] * _sigmoid(pc_ref[:, 512:1024])
        nb = CONV_C_W - 1
        ec = [bufc_ref[:, i, :] for i in range(nb)] + [uc[t * sq:(t + 1) * sq, :] for t in range(nt)]
        planes = []
        for t in range(nt):
            acc = wc_ref[0:1, :] * ec[t]
            for j in range(1, CONV_C_W):
                acc = acc + wc_ref[j:j + 1, :] * ec[t + j]
            planes.append(acc)
        conv_c = jnp.concatenate(planes, axis=0)
        o_ref[:, 1024:1536] = _branch_c_post(conv_c, pc_ref[:, 1024:1536], cb_ref[...], lng_ref[...],
                                             lnb_ref[...]).astype(BF16)
        for i in range(nb):
            bufc_o[:, i, :] = ec[i + nt]

    cmask, smask = _tile_masks(True)
    rowseq = lax.broadcasted_iota(jnp.int32, (ROWS, ROWS), 0) & (sq - 1)
    laneseq = lax.broadcasted_iota(jnp.int32, (ROWS, ROWS), 1) & (sq - 1)

    c2 = c2_ref[...]
    s2 = s2_ref[...]
    rq = _rotary(rq_ref[...], c2, s2)
    rk = _rotary(rk_ref[...], c2, s2) * (HEAD_D ** -0.5)
    rqb = rq.astype(BF16)
    rkb = rk.astype(BF16)
    rvb = rv_ref[...].astype(BF16)
    rkd_t = (rk * kdec_ref[...]).T
    cdec = cdec_ref[0, 0:1, :]

    pre = _ml_prelims(pif_ref[:, 0:128], pif_ref[:, 128:256], bi_ref[...], bf_ref[...], cmask, smask)
    m_prev = mrows_ref[...]
    gt = _ml_gates(pre, 0, m_prev[:, 0:1], cmask, smask)
    mq = mq_ref[...]
    mk = mk_ref[...] * (HEAD_D ** -0.5)
    mqb = mq.astype(BF16)
    mkb = mk.astype(BF16)
    mvb = mv_ref[...].astype(BF16)
    kw = mk * gt["ws"]
    kw_t = kw.T
    dec_full = jnp.broadcast_to(gt["dec"], (ROWS, ROWS))
    dec_scr[...] = dec_full

    def per_seq(b, carry):
        inter_r, inter_m = carry
        sb = sret_ref[b, 0]
        inter_r = jnp.where(rowseq == b, _bdot(rqb, sb.astype(BF16)), inter_r)
        k_sel = jnp.where(laneseq == b, rkd_t, 0.0).astype(BF16)
        sret_o[b, 0] = sb * cdec + _bdot(k_sel, rvb)
        cb_ = cml_ref[b, 0]
        inter_m = jnp.where(rowseq == b, _bdot(mqb, cb_.astype(BF16)), inter_m)
        kw_sel = jnp.where(laneseq == b, kw_t, 0.0).astype(BF16)
        cml_o[b, 0] = cb_ * dec_scr[pl.ds(b, 1), :] + _bdot(kw_sel, mvb)
        return inter_r, inter_m

    zero = jnp.zeros((ROWS, HEAD_D), F32)
    inter_r, inter_m = lax.fori_loop(0, sq, per_seq, (zero, zero), unroll=16)

    sc = _bdot_nt(rqb, rkb) * decay_ref[0]
    ret = _bdot(sc.astype(BF16), rvb) + inter_r * qdec_ref[...]
    o_r = (_silu(rz_ref[...]) * (_head_norm(ret) * rgn_ref[...])).astype(BF16)

    n_old = nrows_ref[...]
    qn = jnp.sum(mq * n_old, axis=1, keepdims=True)
    hout = _ml_output(gt, mqb, mkb, mvb, inter_m, qn)
    nrows_o[...] = n_old * dec_full + _bdot(pre["bones"], kw.astype(BF16))
    mrows_o[...] = jnp.broadcast_to(gt["m_new"], (ROWS, ROWS))
    hm = _sigmoid(mo_ref[...]) * hout
    o_m = (_silu(mz_ref[...]) * (_head_norm(hm) * mgn_ref[...])).astype(BF16)

    for hh in range(HEADS):
        @pl.when(h == hh)
        def _(hh=hh):
            o_ref[:, 512 + hh * HEAD_D:512 + (hh + 1) * HEAD_D] = o_r
            o_ref[:, 1536 + hh * HEAD_D:1536 + (hh + 1) * HEAD_D] = o_m


def _mix_sample(p, pif, consts, prm, st, nseq, layer, depth, stacked):
    ntile = nseq // SAMPLE_SEQS
    sq = SAMPLE_SEQS
    head_cols = lambda name: pl.BlockSpec(
        (ROWS, HEAD_D), lambda i, h, _o=_COL[name] // HEAD_D: (i, _o + h))
    const2 = lambda i, h: (0, 0)
    per_head2 = lambda i, h: (0, h)
    conv_a_state = pl.BlockSpec((None, sq, CONV_A_W - 1, BRANCH_W), lambda i, h: (layer, i, 0, 0))
    conv_c_state = pl.BlockSpec((None, sq, CONV_C_W - 1, BRANCH_W), lambda i, h: (layer, i, 0, 0))
    matrix_state = pl.BlockSpec((None, sq, 1, HEAD_D, HEAD_D), lambda i, h: (layer, i, h, 0, 0))
    in_specs = [
        pl.BlockSpec((ROWS, 2048), lambda i, h: (i, 0)),
        pl.BlockSpec((ROWS, 2048), lambda i, h: (i, 2)),
    ] + [head_cols(n) for n in ("r_q", "r_k", "r_v", "r_z", "m_q", "m_k", "m_v", "m_o", "m_z")] + [
        pl.BlockSpec((ROWS, 256), lambda i, h: (i, h)),
        pl.BlockSpec((ROWS, HEAD_D), const2),
        pl.BlockSpec((ROWS, HEAD_D), const2),
        pl.BlockSpec((1, ROWS, ROWS), lambda i, h: (h, 0, 0)),
        pl.BlockSpec((ROWS, HEAD_D), per_head2),
        pl.BlockSpec((ROWS, HEAD_D), per_head2),
        pl.BlockSpec((1, 8, 128), lambda i, h: (h, 0, 0)),
        pl.BlockSpec((CONV_A_W, BRANCH_W), const2),
        pl.BlockSpec((CONV_C_W, BRANCH_W), const2),
        pl.BlockSpec((1, BRANCH_W), const2),
        pl.BlockSpec((1, BRANCH_W), const2),
        pl.BlockSpec((1, BRANCH_W), const2),
        pl.BlockSpec((1, HEAD_D), per_head2),
        pl.BlockSpec((1, HEAD_D), per_head2),
        pl.BlockSpec((1, 128), per_head2),
        pl.BlockSpec((1, 128), per_head2),
        conv_a_state, matrix_state, conv_c_state, matrix_state,
        pl.BlockSpec((ROWS, HEAD_D), lambda i, h: (i, h)),
        pl.BlockSpec((ROWS, HEAD_D), lambda i, h: (i, h)),
    ]
    out_specs = [
        pl.BlockSpec((ROWS, N_BRANCH * BRANCH_W), lambda i, h: (i, 0)),
        conv_a_state, matrix_state, conv_c_state, matrix_state,
        pl.BlockSpec((ROWS, HEAD_D), lambda i, h: (i, h)),
        pl.BlockSpec((ROWS, HEAD_D), lambda i, h: (i, h)),
    ]
    nrow = ntile * ROWS
    out_shape = [
        jax.ShapeDtypeStruct((nrow, N_BRANCH * BRANCH_W), BF16),
        jax.ShapeDtypeStruct((depth, nseq, CONV_A_W - 1, BRANCH_W), F32),
        jax.ShapeDtypeStruct((depth, nseq, HEADS, HEAD_D, HEAD_D), F32),
        jax.ShapeDtypeStruct((depth, nseq, CONV_C_W - 1, BRANCH_W), F32),
        jax.ShapeDtypeStruct((depth, nseq, HEADS, HEAD_D, HEAD_D), F32),
        jax.ShapeDtypeStruct((nrow, HEADS * HEAD_D), F32),
        jax.ShapeDtypeStruct((nrow, HEADS * HEAD_D), F32),
    ]
    args = [p] * 11 + [
        pif, consts["c2"], consts["s2"], consts["decay"], consts["qdec"], consts["kdec"],
        consts["cdec"], prm["wa"], prm["wc"], prm["cb"], prm["lng"], prm["lnb"], prm["rgn"],
        prm["mgn"], prm["bi_h"], prm["bf_h"],
        st["bufa"], st["sret"], st["bufc"], st["cml"], st["nrows"], st["mrows"]]
    assert len(args) == len(in_specs)
    n_in = len(args)
    aliases = {}
    if stacked is not None:
        aliases = {n_in + k: 1 + k for k in range(4)}
        in_specs = in_specs + [pl.BlockSpec(memory_space=pl.ANY)] * 4
        args = args + list(stacked)

    def entry(*refs):
        _mix_sample_kernel(*refs[:n_in], *refs[n_in + len(aliases):])

    return pl.pallas_call(
        entry,
        grid=(ntile, HEADS),
        in_specs=in_specs, out_specs=out_specs, out_shape=out_shape,
        scratch_shapes=[pltpu.VMEM((ROWS, ROWS), F32)],
        input_output_aliases=aliases,
        compiler_params=pltpu.CompilerParams(
            dimension_semantics=("arbitrary", "arbitrary"), vmem_limit_bytes=VMEM_LIMIT),
        name="mix_sample",
    )(*args)


def _rope_tables(pos):
    inv = ROPE_BASE ** (-jnp.arange(0, HEAD_D, 2, dtype=F32) / HEAD_D)
    ang = pos.astype(F32)[:, None] * inv[None, :]
    cos = jnp.cos(ang)
    sin = jnp.sin(ang)
    return jnp.concatenate([cos, cos], axis=1), jnp.concatenate([-sin, sin], axis=1)


def _decay_tables(t_of_row, seq_of_row, chunk_len):
    log_g = np.log1p(-(2.0 ** (-5.0 - np.arange(HEADS, dtype=np.float64))))
    dt = t_of_row[:, None] - t_of_row[None, :]
    ok = (seq_of_row[:, None] == seq_of_row[None, :]) & (dt >= 0)
    decay = np.where(ok[None], np.exp(np.maximum(dt, 0)[None] * log_g[:, None, None]), 0.0)
    qdec = np.exp((t_of_row[:, None] + 1.0) * log_g[None, :])
    kdec = np.exp((chunk_len - 1.0 - t_of_row)[:, None] * log_g[None, :])
    cdec = np.exp(chunk_len * log_g)
    rep = lambda a: jnp.asarray(np.repeat(a, HEAD_D, axis=1), F32)
    return jnp.asarray(decay, F32), rep(qdec), rep(kdec), cdec


def kernel(x_prompt, x_sample, state_conv_a, state_ret, state_conv_c, state_mlstm_C,
           state_mlstm_n, state_mlstm_m, norm_g, w_in, conv_a_w, conv_c_w, conv_c_b,
           ln_c_g, ln_c_b, ret_gn_g, ml_gn_g, ml_gate_b, w_br, w_out, final_g):
    bp, tp, _ = x_prompt.shape
    bs, ts, _ = x_sample.shape
    depth = w_in.shape[0]
    sq = SAMPLE_SEQS
    ntile = bs // sq
    assert tp % STEP_ROWS == 0 and ts * sq == ROWS and bs % sq == 0

    r = np.arange(ROWS)
    c2p, s2p = _rope_tables(jnp.arange(tp, dtype=jnp.int32))
    decay_p, qdec_p, kdec_p, cdec_p = _decay_tables(r.astype(np.float64), np.zeros(ROWS), float(ROWS))
    consts_p = dict(c2=c2p, s2=s2p, decay=decay_p, qdec=qdec_p, kdec=kdec_p,
                    cdec=tuple(float(v) for v in cdec_p))
    t_s = (r // sq).astype(np.float64)
    c2s, s2s = _rope_tables(PAST_LEN + jnp.asarray(r // sq, jnp.int32))
    decay_s, qdec_s, kdec_s, cdec_s = _decay_tables(t_s, r % sq, float(ts))
    cdec_s_arr = jnp.asarray(np.broadcast_to(cdec_s[:, None, None], (HEADS, 8, 128)), F32)
    consts_s = dict(c2=c2s, s2=s2s, decay=decay_s, qdec=qdec_s, kdec=kdec_s, cdec=cdec_s_arr)

    xp = x_prompt.reshape(bp * tp, D_MODEL)
    xs = x_sample.reshape(ntile, sq, ts, D_MODEL).transpose(0, 2, 1, 3).reshape(bs * ts, D_MODEL)

    fg = final_g.reshape(1, D_MODEL)
    outs_p = [[] for _ in range(6)]
    n_s, m_s = [], []
    stacked = None
    sel_p = np.zeros((2 * HEADS, N_IF), np.float32)
    sel_s = np.zeros((2 * HEADS, HEADS * N_IF), np.float32)
    for h in range(HEADS):
        sel_p[h, h] = 1.0
        sel_p[HEADS + h, 128 + h] = 1.0
        sel_s[h, h * N_IF] = 1.0
        sel_s[HEADS + h, h * N_IF + 128] = 1.0
    w_bf = w_in.astype(BF16)
    pad_row = lambda v: jnp.pad(v, (0, 128 - v.shape[0])).reshape(1, 128)
    for l in range(depth):
        w_if = w_in[l, :, N_MAIN:N_MAIN + 2 * HEADS]
        spread = lambda sel: jnp.dot(w_if, jnp.asarray(sel), precision=lax.Precision.HIGHEST).astype(BF16)
        w_gate = w_bf[l, :, N_MAIN + 2 * HEADS:]
        gb = ml_gate_b[l]
        prm = dict(
            wa=conv_a_w[l], wc=conv_c_w[l], cb=conv_c_b[l].reshape(1, -1),
            lng=ln_c_g[l].reshape(1, -1), lnb=ln_c_b[l].reshape(1, -1),
            rgn=ret_gn_g[l].reshape(1, -1), mgn=ml_gn_g[l].reshape(1, -1),
            bi=pad_row(gb[:HEADS]), bf=pad_row(gb[HEADS:]),
            bi_h=jnp.concatenate([pad_row(gb[h:h + 1]) for h in range(HEADS)], axis=1),
            bf_h=jnp.concatenate([pad_row(gb[HEADS + h:HEADS + h + 1]) for h in range(HEADS)], axis=1),
        )
        gnorm = norm_g[l].reshape(1, D_MODEL)
        wbr = w_br[l].astype(BF16)
        wout = w_out[l].astype(BF16)
        final = l == depth - 1

        res = _xmix_prompt(xp, gnorm, w_bf, spread(sel_p), l, consts_p, prm, bp, tp)
        for k in range(5):
            outs_p[k].append(res[1 + k])
        outs_p[5].append(res[6][:, 0, :HEADS])
        xp = _merge(xp, gnorm, res[0], w_gate, wbr, wout, fg, final, tm=512)

        ps, pifs = _inproj(xs, gnorm, w_bf, spread(sel_s), l, tm=bs * ts, tn=2048)
        n_rows = jnp.broadcast_to(
            state_mlstm_n[l].reshape(ntile, 1, sq, HEADS * HEAD_D),
            (ntile, ts, sq, HEADS * HEAD_D)).reshape(bs * ts, HEADS * HEAD_D)
        m_rows = jnp.broadcast_to(
            state_mlstm_m[l].reshape(ntile, 1, sq, HEADS, 1),
            (ntile, ts, sq, HEADS, HEAD_D)).reshape(bs * ts, HEADS * HEAD_D)
        st = dict(bufa=state_conv_a, sret=state_ret, bufc=state_conv_c, cml=state_mlstm_C,
                  nrows=n_rows, mrows=m_rows)
        res = _mix_sample(ps, pifs, consts_s, prm, st, bs, l, depth, stacked)
        stacked = tuple(res[1:5])
        n_s.append(res[5].reshape(ntile, ts, sq, HEADS, HEAD_D)[:, 0].reshape(bs, HEADS, HEAD_D))
        m_s.append(res[6].reshape(ntile, ts, sq, HEADS, HEAD_D)[:, 0, :, :, 0].reshape(bs, HEADS))
        xs = _merge(xs, gnorm, res[0], w_gate, wbr, wout, fg, final, tm=bs * ts)

    y_prompt = xp.reshape(bp, tp, D_MODEL)
    y_sample = xs.reshape(ntile, ts, sq, D_MODEL).transpose(0, 2, 1, 3).reshape(bs, ts, D_MODEL)
    sp = [jnp.stack(a, axis=0) for a in outs_p]
    return (y_prompt, y_sample, sp[0], stacked[0], sp[1], stacked[1], sp[2], stacked[2],
            sp[3], stacked[3], sp[4], jnp.stack(n_s, axis=0), sp[5], jnp.stack(m_s, axis=0))
```
